```python
import jax, jax.numpy as jnp
from jax import lax
import numpy as np

D_MODEL = 1024
BATCH = 8
SEQ = 2048
DEPTH = 2
DEC_BATCH = 128
DEC_SEQ = 8
PAST_LEN = 16384
PAGE_SIZE = 128

N_HEADS_A = 8
HEAD_DK = 128
HEAD_DV = 128
D_QK = N_HEADS_A * HEAD_DK
D_VA = N_HEADS_A * HEAD_DV
D_CONV = 2 * D_QK + D_VA
CONV_W = 4
DELTA_CHUNK = 64
MLP_CHUNK = 128
N_GROUPS_B = 8
D_B = 1024
GROUP_DIM_B = D_B // N_GROUPS_B
D_FF = ((8 * D_MODEL // 3 + 255) // 256) * 256
SPLIT_POINTS = (D_CONV,
                D_CONV + D_VA,
                D_CONV + D_VA + N_HEADS_A,
                D_CONV + D_VA + 2 * N_HEADS_A,
                D_CONV + D_VA + 2 * N_HEADS_A + 2 * D_B)
D_IN = D_CONV + D_VA + 2 * N_HEADS_A + 2 * D_B + 2 * D_MODEL

kernel_name = "hybrid_gdn_chunkmlp_decoder_step"


def rms_norm(x, w, eps=1e-6):
    xf = x.astype(jnp.float32)
    y = xf * lax.rsqrt(jnp.mean(jnp.square(xf), axis=-1, keepdims=True) + eps)
    return (y * w.astype(jnp.float32)).astype(x.dtype)


def layer_norm(x, w, b, eps=1e-5):
    xf = x.astype(jnp.float32)
    xc = xf - jnp.mean(xf, axis=-1, keepdims=True)
    var = jnp.mean(jnp.square(xc), axis=-1, keepdims=True)
    return (xc * lax.rsqrt(var + eps) * w.astype(jnp.float32) + b.astype(jnp.float32)).astype(x.dtype)


def l2_normalize(x, eps=1e-6):
    xf = x.astype(jnp.float32)
    return xf * lax.rsqrt(jnp.sum(jnp.square(xf), axis=-1, keepdims=True) + eps)


def causal_short_conv(x, buf, w):
    t = x.shape[1]
    xp = jnp.concatenate([buf.astype(x.dtype), x], axis=1)
    y = sum(w[i] * xp[:, i:i + t] for i in range(CONV_W))
    return jax.nn.silu(y), xp[:, -(CONV_W - 1):]


def gated_delta_chunked(q, k, v, g, beta, s0):
    n, t, h, dk = q.shape
    dv = v.shape[-1]
    c = min(DELTA_CHUNK, t)
    pad = (-t) % c
    n_c = (t + pad) // c

    def prep(arr):
        arr = jnp.pad(arr, [(0, 0), (0, pad)] + [(0, 0)] * (arr.ndim - 2))
        arr = arr.reshape((n, n_c, c, h) + arr.shape[3:])
        return jnp.swapaxes(jnp.swapaxes(arr, 0, 1), 2, 3)

    q = prep(q * HEAD_DK ** -0.5)
    k = prep(k)
    v = prep(v)
    g = prep(g)
    beta = prep(beta)
    gc = jnp.cumsum(g, axis=-1)
    idx = jnp.arange(c)
    causal = idx[:, None] >= idx[None, :]
    strict = idx[:, None] > idx[None, :]
    decay = jnp.exp(jnp.where(causal, gc[..., :, None] - gc[..., None, :], -jnp.inf))
    k_beta = k * beta[..., None]
    a_mat = jnp.where(strict, jnp.einsum('mnhik,mnhjk->mnhij', k_beta, k) * decay, 0.0)
    rhs = jnp.concatenate([v * beta[..., None], k_beta * jnp.exp(gc)[..., None]], axis=-1)
    sol = lax.linalg.triangular_solve(a_mat + jnp.eye(c, dtype=jnp.float32), rhs,
                                      left_side=True, lower=True, unit_diagonal=True)
    u, w = sol[..., :dv], sol[..., dv:]
    qk = jnp.einsum('mnhik,mnhjk->mnhij', q, k) * decay

    def step(s, inp):
        q_c, k_c, u_c, w_c, g_c, qk_c = inp
        v_new = u_c - jnp.einsum('nhck,nhkv->nhcv', w_c, s)
        o_c = (jnp.einsum('nhck,nhkv->nhcv', q_c * jnp.exp(g_c)[..., None], s)
               + jnp.einsum('nhij,nhjv->nhiv', qk_c, v_new))
        g_last = g_c[..., -1:]
        s = (s * jnp.exp(g_last)[..., None]
             + jnp.einsum('nhck,nhcv->nhkv', k_c * jnp.exp(g_last - g_c)[..., None], v_new))
        return s, o_c

    s_final, o = lax.scan(step, s0, (q, k, u, w, gc, qk))
    o = jnp.swapaxes(jnp.swapaxes(o, 2, 3), 0, 1).reshape(n, n_c * c, h, dv)[:, :t]
    return o, s_final


def delta_branch(qkv, z, a, b, conv_buf, s0, conv_w, a_log, dt_bias, norm_w):
    n, t, _ = qkv.shape
    qkv_c, new_buf = causal_short_conv(qkv, conv_buf, conv_w)
    q, k, v = jnp.split(qkv_c, [D_QK, 2 * D_QK], axis=-1)
    q = l2_normalize(q.reshape(n, t, N_HEADS_A, HEAD_DK))
    k = l2_normalize(k.reshape(n, t, N_HEADS_A, HEAD_DK))
    v = v.reshape(n, t, N_HEADS_A, HEAD_DV).astype(jnp.float32)
    g = -jnp.exp(a_log.astype(jnp.float32)) * jax.nn.softplus(a.astype(jnp.float32) + dt_bias.astype(jnp.float32))
    beta = jax.nn.sigmoid(b.astype(jnp.float32))
    o, s_new = gated_delta_chunked(q, k, v, g, beta, s0.astype(jnp.float32))
    o = rms_norm(o, norm_w) * jax.nn.silu(z.reshape(n, t, N_HEADS_A, HEAD_DV).astype(jnp.float32))
    return o.reshape(n, t, D_VA).astype(qkv.dtype), new_buf, s_new.astype(s0.dtype)


def chunk_mlp_branch(uv, ln_w, ln_b, w_spatial, b_spatial):
    n, t, _ = uv.shape
    u, v = jnp.split(jax.nn.gelu(uv, approximate=False), 2, axis=-1)
    v = layer_norm(v, ln_w, ln_b)
    c = min(MLP_CHUNK, t)
    pad = (-t) % c
    n_c = (t + pad) // c
    vc = jnp.pad(v, [(0, 0), (0, pad), (0, 0)]).reshape(n, n_c, c, N_GROUPS_B, GROUP_DIM_B)
    idx = jnp.arange(c)
    ws = jnp.where(idx[:, None] >= idx[None, :], w_spatial[:, :c, :c], 0.0)
    mixed = (jnp.einsum('gij,nmjgd->nmigd', ws, vc)
             + jnp.swapaxes(b_spatial[:, :c], 0, 1)[:, :, None])
    mixed = mixed.reshape(n, n_c * c, D_B)[:, :t]
    last_start = ((t - 1) // MLP_CHUNK) * MLP_CHUNK
    return u * mixed, v[:, last_start:]


def trunk_layer(x, conv_buf, s0, norm_pre_mix, w_in, conv_w, a_log, dt_bias, delta_norm_w,
                sgu_ln_w, sgu_ln_b, w_spatial, b_spatial, w_proj_a, w_proj_b, w_out,
                norm_post_mix, norm_pre_ffn, w_ffn_in, w_ffn_out, norm_post_ffn):
    h = rms_norm(x, norm_pre_mix)
    proj = jnp.einsum('btd,de->bte', h, w_in)
    qkv, z, a, b, uv, gates = jnp.split(proj, SPLIT_POINTS, axis=-1)
    o_a, new_buf, s_new = delta_branch(qkv, z, a, b, conv_buf, s0, conv_w, a_log, dt_bias, delta_norm_w)
    o_b, v_rows = chunk_mlp_branch(uv, sgu_ln_w, sgu_ln_b, w_spatial, b_spatial)
    g_a, g_b = jnp.split(gates, 2, axis=-1)
    merged = (jax.nn.sigmoid(g_a) * jnp.einsum('bte,ed->btd', o_a, w_proj_a)
              + jax.nn.sigmoid(g_b) * jnp.einsum('bte,ed->btd', o_b, w_proj_b))
    x = x + rms_norm(jnp.einsum('btd,de->bte', merged, w_out), norm_post_mix)
    h = rms_norm(x, norm_pre_ffn)
    gate, up = jnp.split(jnp.einsum('btd,df->btf', h, w_ffn_in), 2, axis=-1)
    x = x + rms_norm(jnp.einsum('btf,fd->btd', jax.nn.silu(gate) * up, w_ffn_out), norm_post_ffn)
    return x, s_new, new_buf, v_rows


def setup_inputs(seed: int = 0) -> dict:
    key = jax.random.key(seed)
    ks = jax.random.split(key, 24)
    f32 = jnp.float32

    def nrm(k, shape, scale):
        return jax.random.normal(k, shape, f32) * scale

    def gain(k, shape):
        return 1.0 + 0.05 * jax.random.normal(k, shape, f32)

    dt = jax.random.uniform(ks[8], (DEPTH, N_HEADS_A), f32, 0.001, 0.1)
    return {
        "x_prompt": nrm(ks[0], (BATCH, SEQ, D_MODEL), 1.0),
        "x_sample": nrm(ks[1], (DEC_BATCH, DEC_SEQ, D_MODEL), 1.0),
        "state_delta": nrm(ks[2], (DEPTH, DEC_BATCH, N_HEADS_A, HEAD_DK, HEAD_DV), 0.1),
        "state_conv": nrm(ks[3], (DEPTH, DEC_BATCH, CONV_W - 1, D_CONV), 1.0),
        "norm_pre_mix": gain(ks[4], (DEPTH, D_MODEL)),
        "w_in": nrm(ks[5], (DEPTH, D_MODEL, D_IN), D_MODEL ** -0.5),
        "conv_w": nrm(ks[6], (DEPTH, CONV_W, D_CONV), CONV_W ** -0.5),
        "a_log": jnp.log(jax.random.uniform(ks[7], (DEPTH, N_HEADS_A), f32, 1.0, 16.0)),
        "dt_bias": jnp.log(jnp.expm1(dt)),
        "delta_norm_w": gain(ks[9], (DEPTH, HEAD_DV)),
        "sgu_ln_w": gain(ks[10], (DEPTH, D_B)),
        "sgu_ln_b": nrm(ks[11], (DEPTH, D_B), 0.02),
        "w_spatial": nrm(ks[12], (DEPTH, N_GROUPS_B, MLP_CHUNK, MLP_CHUNK), MLP_CHUNK ** -0.5),
        "b_spatial": 1.0 + 0.1 * jax.random.normal(ks[13], (DEPTH, N_GROUPS_B, MLP_CHUNK), f32),
        "w_proj_a": nrm(ks[14], (DEPTH, D_VA, D_MODEL), D_VA ** -0.5),
        "w_proj_b": nrm(ks[15], (DEPTH, D_B, D_MODEL), D_B ** -0.5),
        "w_out": nrm(ks[16], (DEPTH, D_MODEL, D_MODEL), D_MODEL ** -0.5),
        "norm_post_mix": gain(ks[17], (DEPTH, D_MODEL)),
        "norm_pre_ffn": gain(ks[18], (DEPTH, D_MODEL)),
        "w_ffn_in": nrm(ks[19], (DEPTH, D_MODEL, 2 * D_FF), D_MODEL ** -0.5),
        "w_ffn_out": nrm(ks[20], (DEPTH, D_FF, D_MODEL), D_FF ** -0.5),
        "norm_post_ffn": gain(ks[21], (DEPTH, D_MODEL)),
    }


def reference(x_prompt, x_sample, state_delta, state_conv, norm_pre_mix, w_in, conv_w, a_log,
              dt_bias, delta_norm_w, sgu_ln_w, sgu_ln_b, w_spatial, b_spatial, w_proj_a, w_proj_b,
              w_out, norm_post_mix, norm_pre_ffn, w_ffn_in, w_ffn_out, norm_post_ffn):
    y_p, y_s = x_prompt, x_sample
    conv0 = jnp.zeros((BATCH, CONV_W - 1, D_CONV), x_prompt.dtype)
    s_zero = jnp.zeros((BATCH, N_HEADS_A, HEAD_DK, HEAD_DV), state_delta.dtype)
    sd_p, sc_p, cv_p, sd_s, sc_s, cv_s = [], [], [], [], [], []
    for l in range(DEPTH):
        p = dict(norm_pre_mix=norm_pre_mix[l], w_in=w_in[l], conv_w=conv_w[l], a_log=a_log[l],
                 dt_bias=dt_bias[l], delta_norm_w=delta_norm_w[l], sgu_ln_w=sgu_ln_w[l],
                 sgu_ln_b=sgu_ln_b[l], w_spatial=w_spatial[l], b_spatial=b_spatial[l],
                 w_proj_a=w_proj_a[l], w_proj_b=w_proj_b[l], w_out=w_out[l],
                 norm_post_mix=norm_post_mix[l], norm_pre_ffn=norm_pre_ffn[l],
                 w_ffn_in=w_ffn_in[l], w_ffn_out=w_ffn_out[l], norm_post_ffn=norm_post_ffn[l])
        y_p, s_new, buf_new, v_rows = trunk_layer(y_p, conv0, s_zero, **p)
        sd_p.append(s_new)
        sc_p.append(buf_new)
        cv_p.append(v_rows)
        y_s, s_new, buf_new, v_rows = trunk_layer(y_s, state_conv[l], state_delta[l], **p)
        sd_s.append(s_new)
        sc_s.append(buf_new)
        cv_s.append(v_rows)
    return (y_p, y_s, jnp.stack(sd_p), jnp.stack(sc_p), jnp.stack(cv_p),
            jnp.stack(sd_s), jnp.stack(sc_s), jnp.stack(cv_s))
```

```python
import functools

import jax
import jax.numpy as jnp
from jax import lax
from jax.experimental import pallas as pl
from jax.experimental.pallas import tpu as pltpu

F32 = jnp.float32
BF16 = jnp.bfloat16

D_MODEL = 1024
N_HEADS = 8
HEAD_D = 128
D_QK = N_HEADS * HEAD_D
D_CONV = 3 * D_QK
CONV_W = 4
DELTA_CHUNK = 64
MLP_CHUNK = 128
N_GROUPS = 8
D_B = 1024
D_FF = 2816
D_MAIN = D_CONV + D_QK + 2 * D_B + 2 * D_MODEL
AB_OFF = D_CONV + D_QK
LANES = 128
VMEM_LIMIT = 48 * 1024 * 1024

_HI = lax.Precision.HIGHEST


def _mm(a, b, dims=(((1,), (0,)), ((), ()))):
    return lax.dot_general(a.astype(BF16), b.astype(BF16), dims, preferred_element_type=F32)


def _mm_hi(a, b, dims=(((1,), (0,)), ((), ()))):
    return lax.dot_general(a, b, dims, precision=_HI, preferred_element_type=F32)


_NT = (((1,), (1,)), ((), ()))
_TN = (((0,), (0,)), ((), ()))


def _rms(x, w, eps=1e-6):
    return x * lax.rsqrt(jnp.mean(jnp.square(x), axis=-1, keepdims=True) + eps) * w


def _inproj_kernel(x_ref, nw_ref, w_ref, o_ref):
    h = _rms(x_ref[...], nw_ref[...])
    o_ref[...] = _mm(h, w_ref[...])


def _inproj(x2d, nw, w_main, tm, tn):
    m = x2d.shape[0]
    return pl.pallas_call(
        _inproj_kernel,
        grid=(D_MAIN // tn, m // tm),
        in_specs=[
            pl.BlockSpec((tm, D_MODEL), lambda j, i: (i, 0)),
            pl.BlockSpec((1, D_MODEL), lambda j, i: (0, 0)),
            pl.BlockSpec((D_MODEL, tn), lambda j, i: (0, j)),
        ],
        out_specs=pl.BlockSpec((tm, tn), lambda j, i: (i, j)),
        out_shape=jax.ShapeDtypeStruct((m, D_MAIN), F32),
        compiler_params=pltpu.CompilerParams(
            dimension_semantics=("arbitrary", "arbitrary"), vmem_limit_bytes=VMEM_LIMIT),
        name="inproj",
    )(x2d, nw, w_main)


def _inproj_ab_kernel(x_ref, nw_ref, w_ref, o_ref):
    h = _rms(x_ref[...], nw_ref[...])
    o_ref[...] = _mm_hi(h, w_ref[...])


def _inproj_ab(x2d, nw, w_ab, tm):
    m = x2d.shape[0]
    return pl.pallas_call(
        _inproj_ab_kernel,
        grid=(m // tm,),
        in_specs=[
            pl.BlockSpec((tm, D_MODEL), lambda i: (i, 0)),
            pl.BlockSpec((1, D_MODEL), lambda i: (0, 0)),
            pl.BlockSpec((D_MODEL, 2 * LANES), lambda i: (0, 0)),
        ],
        out_specs=pl.BlockSpec((tm, 2 * LANES), lambda i: (i, 0)),
        out_shape=jax.ShapeDtypeStruct((m, 2 * LANES), F32),
        compiler_params=pltpu.CompilerParams(
            dimension_semantics=("arbitrary",), vmem_limit_bytes=VMEM_LIMIT),
        name="inproj_ab",
    )(x2d, nw, w_ab)


def _delta_kernel(qkv_ref, z_ref, ab_ref, cbuf_ref, s0_ref, cw_ref, alog_ref, dtb_ref, nw_ref,
                  o_ref, nbuf_ref, snew_ref,
                  s_ref, xbuf_ref, qkvc_ref, g_ref, beta_ref, *, tt, c):
    t = pl.program_id(1)
    nt = pl.num_programs(1)
    pad = 8
    hist = CONV_W - 1

    @pl.when(t == 0)
    def _():
        s_ref[...] = s0_ref[0]
        xbuf_ref[pad - hist:pad, :] = cbuf_ref[0]

    xbuf_ref[pad:pad + tt, :] = qkv_ref[0]

    for part in range(3):
        for h in range(N_HEADS):
            c0 = part * D_QK + h * HEAD_D
            cols = slice(c0, c0 + HEAD_D)
            y = cw_ref[0:1, cols] * xbuf_ref[pad - 3:pad - 3 + tt, cols]
            for i in range(1, CONV_W):
                y = y + cw_ref[i:i + 1, cols] * xbuf_ref[pad - 3 + i:pad - 3 + i + tt, cols]
            y = y * jax.nn.sigmoid(y)
            if part < 2:
                y = y * lax.rsqrt(jnp.sum(jnp.square(y), axis=-1, keepdims=True) + 1e-6)
            if part == 0:
                y = y * (HEAD_D ** -0.5)
            qkvc_ref[:, cols] = y

    tail = xbuf_ref[pad + tt - hist:pad + tt, :]

    @pl.when(t == nt - 1)
    def _():
        nbuf_ref[0] = tail

    xbuf_ref[pad - hist:pad, :] = tail

    a = ab_ref[0, :, 0:LANES]
    b = ab_ref[0, :, LANES:2 * LANES]
    g_ref[...] = -jnp.exp(alog_ref[...]) * jax.nn.softplus(a + dtb_ref[...])
    beta_ref[...] = jax.nn.sigmoid(b)

    ri = lax.broadcasted_iota(jnp.int32, (c, c), 0)
    ci = lax.broadcasted_iota(jnp.int32, (c, c), 1)
    causal = ri >= ci
    strict = ri > ci
    tri = jnp.where(causal, 1.0, 0.0).astype(F32)
    eye = jnp.where(ri == ci, 1.0, 0.0).astype(F32)
    eye8 = jnp.where(lax.broadcasted_iota(jnp.int32, (8, LANES), 0)
                     == lax.broadcasted_iota(jnp.int32, (8, LANES), 1), 1.0, 0.0).astype(F32)
    n_double = c.bit_length() - 2

    def chunk_body(ic, carry):
        r0 = pl.multiple_of(ic * c, c)
        rows = pl.ds(r0, c)
        gc = _mm_hi(tri, g_ref[rows, :])
        gct = _mm_hi(eye8, gc, _NT)
        beta = beta_ref[rows, :]
        for h in range(N_HEADS):
            cols_q = slice(h * HEAD_D, (h + 1) * HEAD_D)
            cols_k = slice(D_QK + h * HEAD_D, D_QK + (h + 1) * HEAD_D)
            cols_v = slice(2 * D_QK + h * HEAD_D, 2 * D_QK + (h + 1) * HEAD_D)
            q = qkvc_ref[rows, cols_q]
            k = qkvc_ref[rows, cols_k]
            v = qkvc_ref[rows, cols_v]
            gcol = gc[:, h:h + 1]
            grow = gct[h:h + 1, :]
            bcol = beta[:, h:h + 1]
            decay = jnp.where(causal, jnp.exp(jnp.where(causal, gcol - grow, 0.0)), 0.0)
            kb = k * bcol
            kq = _mm(jnp.concatenate([kb, q], axis=0), k, _NT)
            x = jnp.where(strict, -(kq[:c] * decay), 0.0)
            qk = kq[c:] * decay
            tinv = eye + x
            p = x
            for _ in range(n_double):
                p = _mm_hi(p, p)
                tinv = tinv + _mm_hi(tinv, p)
            egc = jnp.exp(gcol)
            rhs = jnp.concatenate([v * bcol, kb * egc], axis=1)
            sol = _mm_hi(tinv, rhs)
            u = sol[:, :HEAD_D]
            w = sol[:, HEAD_D:]
            s = s_ref[h]
            wq = _mm(jnp.concatenate([w, q * egc], axis=0), s)
            v_new = u - wq[:c]
            o = wq[c:] + _mm(qk, v_new)
            glast = gc[c - 1:c, h:h + 1]
            kdec = k * jnp.exp(glast - gcol)
            s_ref[h] = s * jnp.exp(glast) + _mm(kdec, v_new, _TN)
            o = _rms(o, nw_ref[...])
            zz = z_ref[0, rows, cols_q]
            o_ref[0, rows, cols_q] = o * (zz * jax.nn.sigmoid(zz))
        return carry

    lax.fori_loop(0, tt // c, chunk_body, 0)

    @pl.when(t == nt - 1)
    def _():
        snew_ref[0] = s_ref[...]


def _delta(proj3, ab3, conv_buf, s0, conv_w, alog, dtb, norm_w, tt, c):
    n, t, _ = proj3.shape
    kern = functools.partial(_delta_kernel, tt=tt, c=c)
    const2 = lambda i, j: (0, 0)
    return pl.pallas_call(
        kern,
        grid=(n, t // tt),
        in_specs=[
            pl.BlockSpec((1, tt, D_CONV), lambda i, j: (i, j, 0)),
            pl.BlockSpec((1, tt, D_QK), lambda i, j: (i, j, D_CONV // D_QK)),
            pl.BlockSpec((1, tt, 2 * LANES), lambda i, j: (i, j, 0)),
            pl.BlockSpec((1, CONV_W - 1, D_CONV), lambda i, j: (i, 0, 0)),
            pl.BlockSpec((1, N_HEADS, HEAD_D, HEAD_D), lambda i, j: (i, 0, 0, 0)),
            pl.BlockSpec((CONV_W, D_CONV), const2),
            pl.BlockSpec((1, LANES), const2),
            pl.BlockSpec((1, LANES), const2),
            pl.BlockSpec((1, HEAD_D), const2),
        ],
        out_specs=[
            pl.BlockSpec((1, tt, D_QK), lambda i, j: (i, j, 0)),
            pl.BlockSpec((1, CONV_W - 1, D_CONV), lambda i, j: (i, 0, 0)),
            pl.BlockSpec((1, N_HEADS, HEAD_D, HEAD_D), lambda i, j: (i, 0, 0, 0)),
        ],
        out_shape=[
            jax.ShapeDtypeStruct((n, t, D_QK), F32),
            jax.ShapeDtypeStruct((n, CONV_W - 1, D_CONV), F32),
            jax.ShapeDtypeStruct((n, N_HEADS, HEAD_D, HEAD_D), F32),
        ],
        scratch_shapes=[
            pltpu.VMEM((N_HEADS, HEAD_D, HEAD_D), F32),
            pltpu.VMEM((tt + 8, D_CONV), F32),
            pltpu.VMEM((tt, D_CONV), F32),
            pltpu.VMEM((tt, LANES), F32),
            pltpu.VMEM((tt, LANES), F32),
        ],
        compiler_params=pltpu.CompilerParams(
            dimension_semantics=("arbitrary", "arbitrary"), vmem_limit_bytes=VMEM_LIMIT),
        name="delta",
    )(proj3, proj3, ab3, conv_buf, s0, conv_w, alog, dtb, norm_w)


def _cmlp_kernel(uv_ref, lnw_ref, lnb_ref, ws_ref, bs_ref, o_ref, v_ref, *, c):
    t = pl.program_id(1)
    nt = pl.num_programs(1)
    x = uv_ref[0]
    gel = 0.5 * x * (1.0 + lax.erf(x * (2.0 ** -0.5)))
    u = gel[:, :D_B]
    v = gel[:, D_B:]
    vc = v - jnp.mean(v, axis=-1, keepdims=True)
    var = jnp.mean(jnp.square(vc), axis=-1, keepdims=True)
    v = vc * lax.rsqrt(var + 1e-5) * lnw_ref[...] + lnb_ref[...]

    @pl.when(t == nt - 1)
    def _():
        v_ref[0] = v

    ri = lax.broadcasted_iota(jnp.int32, (c, c), 0)
    ci = lax.broadcasted_iota(jnp.int32, (c, c), 1)
    gd = D_B // N_GROUPS
    for g in range(N_GROUPS):
        cols = slice(g * gd, (g + 1) * gd)
        ws = jnp.where(ri >= ci, ws_ref[g, 0:c, 0:c], 0.0)
        mixed = _mm(ws, v[:, cols]) + bs_ref[0:c, g:g + 1]
        o_ref[0, :, cols] = u[:, cols] * mixed


def _cmlp(proj3, ln_w, ln_b, w_spatial, bs_t, c):
    n, t, _ = proj3.shape
    kern = functools.partial(_cmlp_kernel, c=c)
    const2 = lambda i, j: (0, 0)
    return pl.pallas_call(
        kern,
        grid=(n, t // c),
        in_specs=[
            pl.BlockSpec((1, c, 2 * D_B), lambda i, j: (i, j, (D_CONV + D_QK) // (2 * D_B))),
            pl.BlockSpec((1, D_B), const2),
            pl.BlockSpec((1, D_B), const2),
            pl.BlockSpec((N_GROUPS, MLP_CHUNK, MLP_CHUNK), lambda i, j: (0, 0, 0)),
            pl.BlockSpec((MLP_CHUNK, LANES), const2),
        ],
        out_specs=[
            pl.BlockSpec((1, c, D_B), lambda i, j: (i, j, 0)),
            pl.BlockSpec((1, c, D_B), lambda i, j: (i, 0, 0)),
        ],
        out_shape=[
            jax.ShapeDtypeStruct((n, t, D_B), F32),
            jax.ShapeDtypeStruct((n, c, D_B), F32),
        ],
        compiler_params=pltpu.CompilerParams(
            dimension_semantics=("arbitrary", "arbitrary"), vmem_limit_bytes=VMEM_LIMIT),
        name="cmlp",
    )(proj3, ln_w, ln_b, w_spatial, bs_t)


def _merge_kernel(oa_ref, ob_ref, ga_ref, gb_ref, x_ref, wa_ref, wb_ref, wo_ref, nw_ref, y_ref):
    pa = _mm(oa_ref[...], wa_ref[...])
    pb = _mm(ob_ref[...], wb_ref[...])
    merged = jax.nn.sigmoid(ga_ref[...]) * pa + jax.nn.sigmoid(gb_ref[...]) * pb
    y = _mm(merged, wo_ref[...])
    y_ref[...] = x_ref[...] + _rms(y, nw_ref[...])


def _merge(oa, ob, proj, x2d, wa, wb, wo, nw, tm):
    m = x2d.shape[0]
    row = lambda i: (i, 0)
    const = lambda i: (0, 0)
    g0 = (D_CONV + D_QK + 2 * D_B) // D_MODEL
    return pl.pallas_call(
        _merge_kernel,
        grid=(m // tm,),
        in_specs=[
            pl.BlockSpec((tm, D_QK), row),
            pl.BlockSpec((tm, D_B), row),
            pl.BlockSpec((tm, D_MODEL), lambda i: (i, g0)),
            pl.BlockSpec((tm, D_MODEL), lambda i: (i, g0 + 1)),
            pl.BlockSpec((tm, D_MODEL), row),
            pl.BlockSpec((D_QK, D_MODEL), const),
            pl.BlockSpec((D_B, D_MODEL), const),
            pl.BlockSpec((D_MODEL, D_MODEL), const),
            pl.BlockSpec((1, D_MODEL), const),
        ],
        out_specs=pl.BlockSpec((tm, D_MODEL), row),
        out_shape=jax.ShapeDtypeStruct((m, D_MODEL), F32),
        compiler_params=pltpu.CompilerParams(
            dimension_semantics=("arbitrary",), vmem_limit_bytes=VMEM_LIMIT),
        name="merge",
    )(oa, ob, proj, proj, x2d, wa, wb, wo, nw)


def _ffn_kernel(x_ref, npre_ref, wg_ref, wu_ref, wd_ref, npost_ref, y_ref, h_ref, acc_ref):
    f = pl.program_id(1)

    @pl.when(f == 0)
    def _():
        h_ref[...] = _rms(x_ref[...], npre_ref[...]).astype(BF16)

    h = h_ref[...]
    gate = _mm(h, wg_ref[...])
    up = _mm(h, wu_ref[...])
    part = _mm(gate * jax.nn.sigmoid(gate) * up, wd_ref[...])

    @pl.when(f == 0)
    def _():
        acc_ref[...] = part

    @pl.when(f > 0)
    def _():
        acc_ref[...] += part

    @pl.when(f == pl.num_programs(1) - 1)
    def _():
        y_ref[...] = x_ref[...] + _rms(acc_ref[...], npost_ref[...])


def _ffn(x2d, npre, w_in, w_out, npost, tm, tf):
    m = x2d.shape[0]
    nf = D_FF // tf
    return pl.pallas_call(
        _ffn_kernel,
        grid=(m // tm, nf),
        in_specs=[
            pl.BlockSpec((tm, D_MODEL), lambda i, f: (i, 0)),
            pl.BlockSpec((1, D_MODEL), lambda i, f: (0, 0)),
            pl.BlockSpec((D_MODEL, tf), lambda i, f: (0, f)),
            pl.BlockSpec((D_MODEL, tf), lambda i, f: (0, f + nf)),
            pl.BlockSpec((tf, D_MODEL), lambda i, f: (f, 0)),
            pl.BlockSpec((1, D_MODEL), lambda i, f: (0, 0)),
        ],
        out_specs=pl.BlockSpec((tm, D_MODEL), lambda i, f: (i, 0)),
        out_shape=jax.ShapeDtypeStruct((m, D_MODEL), F32),
        scratch_shapes=[pltpu.VMEM((tm, D_MODEL), BF16), pltpu.VMEM((tm, D_MODEL), F32)],
        compiler_params=pltpu.CompilerParams(
            dimension_semantics=("arbitrary", "arbitrary"), vmem_limit_bytes=VMEM_LIMIT),
        name="ffn",
    )(x2d, npre, w_in, w_in, w_out, npost)


def _trunk_layer(x, conv_buf, s0, p, *, tt, c_delta, c_mlp, tm):
    n, t, _ = x.shape
    m = n * t
    x2d = x.reshape(m, D_MODEL)
    proj = _inproj(x2d, p["norm_pre_mix"], p["w_main"], tm, 2048)
    ab = _inproj_ab(x2d, p["norm_pre_mix"], p["w_ab"], tm)
    proj3 = proj.reshape(n, t, D_MAIN)
    o_a, new_buf, s_new = _delta(proj3, ab.reshape(n, t, 2 * LANES), conv_buf, s0, p["conv_w"],
                                 p["a_log"], p["dt_bias"], p["delta_norm_w"], tt, c_delta)
    o_b, v_rows = _cmlp(proj3, p["sgu_ln_w"], p["sgu_ln_b"], p["w_spatial"], p["bs_t"], c_mlp)
    x1 = _merge(o_a.reshape(m, D_QK), o_b.reshape(m, D_B), proj, x2d, p["w_proj_a"], p["w_proj_b"],
                p["w_out"], p["norm_post_mix"], tm)
    x2 = _ffn(x1, p["norm_pre_ffn"], p["w_ffn_in"], p["w_ffn_out"], p["norm_post_ffn"], tm, 256)
    return x2.reshape(n, t, D_MODEL), s_new, new_buf, v_rows


def _pad_lanes(v):
    return jnp.pad(v, (0, LANES - v.shape[0])).reshape(1, LANES)


def _layer_params(l, norm_pre_mix, w_in, conv_w, a_log, dt_bias, delta_norm_w, sgu_ln_w, sgu_ln_b,
                  w_spatial, b_spatial, w_proj_a, w_proj_b, w_out, norm_post_mix, norm_pre_ffn,
                  w_ffn_in, w_ffn_out, norm_post_ffn):
    wi = w_in[l]
    w_main = jnp.concatenate([wi[:, :AB_OFF], wi[:, AB_OFF + 2 * N_HEADS:]], axis=1).astype(BF16)
    wa = jnp.pad(wi[:, AB_OFF:AB_OFF + N_HEADS], ((0, 0), (0, LANES - N_HEADS)))
    wb = jnp.pad(wi[:, AB_OFF + N_HEADS:AB_OFF + 2 * N_HEADS], ((0, 0), (0, LANES - N_HEADS)))
    row = lambda v: v.reshape(1, -1)
    return dict(
        norm_pre_mix=row(norm_pre_mix[l]), w_main=w_main, w_ab=jnp.concatenate([wa, wb], axis=1),
        conv_w=conv_w[l], a_log=_pad_lanes(a_log[l]), dt_bias=_pad_lanes(dt_bias[l]),
        delta_norm_w=row(delta_norm_w[l]), sgu_ln_w=row(sgu_ln_w[l]), sgu_ln_b=row(sgu_ln_b[l]),
        w_spatial=w_spatial[l],
        bs_t=jnp.pad(b_spatial[l].T, ((0, 0), (0, LANES - N_GROUPS))),
        w_proj_a=w_proj_a[l].astype(BF16), w_proj_b=w_proj_b[l].astype(BF16),
        w_out=w_out[l].astype(BF16), norm_post_mix=row(norm_post_mix[l]),
        norm_pre_ffn=row(norm_pre_ffn[l]), w_ffn_in=w_ffn_in[l].astype(BF16),
        w_ffn_out=w_ffn_out[l].astype(BF16), norm_post_ffn=row(norm_post_ffn[l]))


def kernel(x_prompt, x_sample, state_delta, state_conv, norm_pre_mix, w_in, conv_w, a_log, dt_bias,
           delta_norm_w, sgu_ln_w, sgu_ln_b, w_spatial, b_spatial, w_proj_a, w_proj_b, w_out,
           norm_post_mix, norm_pre_ffn, w_ffn_in, w_ffn_out, norm_post_ffn):
    depth = w_in.shape[0]
    nb, seq, _ = x_prompt.shape
    ndec, dec_seq, _ = x_sample.shape
    y_p, y_s = x_prompt, x_sample
    conv0 = jnp.zeros((nb, CONV_W - 1, D_CONV), x_prompt.dtype)
    s_zero = jnp.zeros((nb, N_HEADS, HEAD_D, HEAD_D), state_delta.dtype)
    sd_p, sc_p, cv_p, sd_s, sc_s, cv_s = [], [], [], [], [], []
    for l in range(depth):
        p = _layer_params(l, norm_pre_mix, w_in, conv_w, a_log, dt_bias, delta_norm_w, sgu_ln_w,
                          sgu_ln_b, w_spatial, b_spatial, w_proj_a, w_proj_b, w_out, norm_post_mix,
                          norm_pre_ffn, w_ffn_in, w_ffn_out, norm_post_ffn)
        y_p, s_new, buf_new, v_rows = _trunk_layer(
            y_p, conv0, s_zero, p, tt=min(256, seq), c_delta=min(DELTA_CHUNK, seq),
            c_mlp=min(MLP_CHUNK, seq), tm=512)
        sd_p.append(s_new)
        sc_p.append(buf_new)
        cv_p.append(v_rows)
        y_s, s_new, buf_new, v_rows = _trunk_layer(
            y_s, state_conv[l], state_delta[l], p, tt=min(256, dec_seq),
            c_delta=min(DELTA_CHUNK, dec_seq), c_mlp=min(MLP_CHUNK, dec_seq), tm=512)
        sd_s.append(s_new)
        sc_s.append(buf_new)
        cv_s.append(v_rows)
    return (y_p, y_s, jnp.stack(sd_p), jnp.stack(sc_p), jnp.stack(cv_p),
            jnp.stack(sd_s), jnp.stack(sc_s), jnp.stack(cv_s))
```

```python
import functools

import jax
import jax.numpy as jnp
from jax import lax
from jax.experimental import pallas as pl
from jax.experimental.pallas import tpu as pltpu

F32 = jnp.float32
BF16 = jnp.bfloat16

D_MODEL = 1024
N_HEADS = 8
HEAD_D = 128
D_QK = N_HEADS * HEAD_D
D_CONV = 3 * D_QK
CONV_W = 4
DELTA_CHUNK = 64
MLP_CHUNK = 128
N_GROUPS = 8
D_B = 1024
D_FF = 2816
D_MAIN = D_CONV + D_QK + 2 * D_B + 2 * D_MODEL
AB_OFF = D_CONV + D_QK
LANES = 128
DEC_SEQS_PER_STEP = 4
VMEM_LIMIT = 48 * 1024 * 1024

_HI = lax.Precision.HIGHEST


def _mm(a, b, dims=(((1,), (0,)), ((), ()))):
    return lax.dot_general(a.astype(BF16), b.astype(BF16), dims, preferred_element_type=F32)


def _mm_hi(a, b, dims=(((1,), (0,)), ((), ()))):
    return lax.dot_general(a, b, dims, precision=_HI, preferred_element_type=F32)


_NT = (((1,), (1,)), ((), ()))
_TN = (((0,), (0,)), ((), ()))


def _rms(x, w, eps=1e-6):
    return x * lax.rsqrt(jnp.mean(jnp.square(x), axis=-1, keepdims=True) + eps) * w


def _inproj_kernel(x_ref, nw_ref, w_ref, o_ref):
    h = _rms(x_ref[...], nw_ref[...])
    o_ref[...] = _mm(h, w_ref[...])


def _inproj(x2d, nw, w_main, tm, tn):
    m = x2d.shape[0]
    return pl.pallas_call(
        _inproj_kernel,
        grid=(D_MAIN // tn, m // tm),
        in_specs=[
            pl.BlockSpec((tm, D_MODEL), lambda j, i: (i, 0)),
            pl.BlockSpec((1, D_MODEL), lambda j, i: (0, 0)),
            pl.BlockSpec((D_MODEL, tn), lambda j, i: (0, j)),
        ],
        out_specs=pl.BlockSpec((tm, tn), lambda j, i: (i, j)),
        out_shape=jax.ShapeDtypeStruct((m, D_MAIN), F32),
        compiler_params=pltpu.CompilerParams(
            dimension_semantics=("arbitrary", "arbitrary"), vmem_limit_bytes=VMEM_LIMIT),
        name="inproj",
    )(x2d, nw, w_main)


def _inproj_ab_kernel(x_ref, nw_ref, w_ref, o_ref):
    h = _rms(x_ref[...], nw_ref[...])
    o_ref[...] = _mm_hi(h, w_ref[...])


def _inproj_ab(x2d, nw, w_ab, tm):
    m = x2d.shape[0]
    return pl.pallas_call(
        _inproj_ab_kernel,
        grid=(m // tm,),
        in_specs=[
            pl.BlockSpec((tm, D_MODEL), lambda i: (i, 0)),
            pl.BlockSpec((1, D_MODEL), lambda i: (0, 0)),
            pl.BlockSpec((D_MODEL, 2 * LANES), lambda i: (0, 0)),
        ],
        out_specs=pl.BlockSpec((tm, 2 * LANES), lambda i: (i, 0)),
        out_shape=jax.ShapeDtypeStruct((m, 2 * LANES), F32),
        compiler_params=pltpu.CompilerParams(
            dimension_semantics=("arbitrary",), vmem_limit_bytes=VMEM_LIMIT),
        name="inproj_ab",
    )(x2d, nw, w_ab)


def _conv_silu_norm(y, part):
    y = y * jax.nn.sigmoid(y)
    if part < 2:
        y = y * lax.rsqrt(jnp.sum(jnp.square(y), axis=-1, keepdims=True) + 1e-6)
    if part == 0:
        y = y * (HEAD_D ** -0.5)
    return y


def _delta_dec_kernel(qkv_ref, z_ref, ab_ref, cbuf_ref, s0_ref, cw_ref, alog_ref, dtb_ref, nw_ref,
                      o_ref, nbuf_ref, snew_ref, xbuf_ref, qkvc_ref, *, nb, c):
    pad = 8
    hist = CONV_W - 1
    xbuf_ref[:, pad - hist:pad, :] = cbuf_ref[...]
    xbuf_ref[:, pad:pad + c, :] = qkv_ref[...]
    for part in range(3):
        for h in range(N_HEADS):
            c0 = part * D_QK + h * HEAD_D
            cols = slice(c0, c0 + HEAD_D)
            y = cw_ref[0:1, cols] * xbuf_ref[:, pad - 3:pad - 3 + c, cols]
            for i in range(1, CONV_W):
                y = y + cw_ref[i:i + 1, cols] * xbuf_ref[:, pad - 3 + i:pad - 3 + i + c, cols]
            qkvc_ref[:, :, cols] = _conv_silu_norm(y, part)
    nbuf_ref[...] = xbuf_ref[:, pad + c - hist:pad + c, :]

    ri = lax.broadcasted_iota(jnp.int32, (c, c), 0)
    ci = lax.broadcasted_iota(jnp.int32, (c, c), 1)
    causal = ri >= ci
    strict = ri > ci
    diag = ri == ci
    tri = jnp.where(causal, 1.0, 0.0).astype(F32)
    eye = jnp.where(diag, 1.0, 0.0).astype(F32)
    n_levels = c.bit_length() - 2
    items = [(i, h) for i in range(nb) for h in range(N_HEADS)]
    idx = {it: n for n, it in enumerate(items)}

    gc, beta = [], []
    for i in range(nb):
        g = -jnp.exp(alog_ref[...]) * jax.nn.softplus(ab_ref[i, :, 0:LANES] + dtb_ref[...])
        gc.append(_mm_hi(tri, g))
        beta.append(jax.nn.sigmoid(ab_ref[i, :, LANES:2 * LANES]))

    def col(part, i, h):
        return qkvc_ref[i, :, part * D_QK + h * HEAD_D:part * D_QK + (h + 1) * HEAD_D]

    gcol = [gc[i][:, h:h + 1] for i, h in items]
    bcol = [beta[i][:, h:h + 1] for i, h in items]
    kb = [col(1, i, h) * bcol[idx[i, h]] for i, h in items]
    kq = [_mm(jnp.concatenate([kb[idx[i, h]], col(0, i, h)], axis=0), col(1, i, h), _NT)
          for i, h in items]
    x, qk, tk = [], [], []
    for n in range(len(items)):
        grow = jnp.sum(jnp.where(diag, gcol[n], 0.0), axis=0, keepdims=True)
        dec = jnp.where(causal, jnp.exp(jnp.where(causal, gcol[n] - grow, 0.0)), 0.0)
        x.append(jnp.where(strict, -(kq[n][:c] * dec), 0.0))
        qk.append(kq[n][c:] * dec)
        tk.append(eye + x[n])
    pk = [_mm(x[n], x[n]) for n in range(len(items))]
    for lvl in range(1, n_levels + 1):
        if lvl < n_levels:
            r = [_mm(jnp.concatenate([pk[n], tk[n]], axis=0), pk[n]) for n in range(len(items))]
            pk = [r[n][:c] for n in range(len(items))]
            tk = [tk[n] + r[n][c:] for n in range(len(items))]
        else:
            tk = [tk[n] + _mm(tk[n], pk[n]) for n in range(len(items))]
    e = [jnp.exp(gcol[n]) for n in range(len(items))]
    sol = [_mm(tk[idx[i, h]], jnp.concatenate([col(2, i, h) * bcol[idx[i, h]],
                                                kb[idx[i, h]] * e[idx[i, h]]], axis=1))
           for i, h in items]
    wq = [_mm(jnp.concatenate([sol[idx[i, h]][:, HEAD_D:], col(0, i, h) * e[idx[i, h]]], axis=0),
              s0_ref[i, h]) for i, h in items]
    vn = [sol[n][:, :HEAD_D] - wq[n][:c] for n in range(len(items))]
    op = [_mm(qk[n], vn[n]) for n in range(len(items))]
    for i, h in items:
        n = idx[i, h]
        glast = gc[i][c - 1:c, h:h + 1]
        kdec = col(1, i, h) * jnp.exp(glast - gcol[n])
        snew_ref[i, h] = s0_ref[i, h] * jnp.exp(glast) + _mm(kdec, vn[n], _TN)
    for i, h in items:
        n = idx[i, h]
        o = _rms(wq[n][c:] + op[n], nw_ref[...])
        cols = slice(h * HEAD_D, (h + 1) * HEAD_D)
        zz = z_ref[i, :, cols]
        o_ref[i, :, cols] = o * (zz * jax.nn.sigmoid(zz))


def _delta_dec(proj3, ab3, conv_buf, s0, conv_w, alog, dtb, norm_w, nb):
    n, c, _ = proj3.shape
    kern = functools.partial(_delta_dec_kernel, nb=nb, c=c)
    const2 = lambda i: (0, 0)
    return pl.pallas_call(
        kern,
        grid=(n // nb,),
        in_specs=[
            pl.BlockSpec((nb, c, D_CONV), lambda i: (i, 0, 0)),
            pl.BlockSpec((nb, c, D_QK), lambda i: (i, 0, D_CONV // D_QK)),
            pl.BlockSpec((nb, c, 2 * LANES), lambda i: (i, 0, 0)),
            pl.BlockSpec((nb, CONV_W - 1, D_CONV), lambda i: (i, 0, 0)),
            pl.BlockSpec((nb, N_HEADS, HEAD_D, HEAD_D), lambda i: (i, 0, 0, 0)),
            pl.BlockSpec((CONV_W, D_CONV), const2),
            pl.BlockSpec((1, LANES), const2),
            pl.BlockSpec((1, LANES), const2),
            pl.BlockSpec((1, HEAD_D), const2),
        ],
        out_specs=[
            pl.BlockSpec((nb, c, D_QK), lambda i: (i, 0, 0)),
            pl.BlockSpec((nb, CONV_W - 1, D_CONV), lambda i: (i, 0, 0)),
            pl.BlockSpec((nb, N_HEADS, HEAD_D, HEAD_D), lambda i: (i, 0, 0, 0)),
        ],
        out_shape=[
            jax.ShapeDtypeStruct((n, c, D_QK), F32),
            jax.ShapeDtypeStruct((n, CONV_W - 1, D_CONV), F32),
            jax.ShapeDtypeStruct((n, N_HEADS, HEAD_D, HEAD_D), F32),
        ],
        scratch_shapes=[
            pltpu.VMEM((nb, c + 8, D_CONV), F32),
            pltpu.VMEM((nb, c, D_CONV), F32),
        ],
        compiler_params=pltpu.CompilerParams(
            dimension_semantics=("arbitrary",), vmem_limit_bytes=VMEM_LIMIT),
        name="delta_dec",
    )(proj3, proj3, ab3, conv_buf, s0, conv_w, alog, dtb, norm_w)


def _chunk_pairs(refs, c):
    qkvc_ref, g_ref, beta_ref, s_ref, z_ref, nw_ref, o_ref = refs
    assert 2 * c == LANES
    ri = lax.broadcasted_iota(jnp.int32, (c, LANES), 0)
    lane = lax.broadcasted_iota(jnp.int32, (c, LANES), 1)
    half = lane >= c
    cj = jnp.where(half, lane - c, lane)
    causal = ri >= cj
    strict = ri > cj
    diag = ri == cj
    eye = jnp.where(diag, 1.0, 0.0).astype(F32)
    rt = lax.broadcasted_iota(jnp.int32, (c, c), 0)
    ct = lax.broadcasted_iota(jnp.int32, (c, c), 1)
    tri = jnp.where(rt >= ct, 1.0, 0.0).astype(F32)
    zeros = jnp.zeros((c, HEAD_D), F32)
    zeros2 = jnp.zeros((c, 2 * HEAD_D), F32)
    n_levels = c.bit_length() - 2

    def blockdiag(m):
        return jnp.concatenate([jnp.where(half, 0.0, m), jnp.where(half, m, 0.0)], axis=0)

    def side_by_side(a, b):
        return jnp.concatenate([jnp.concatenate([a, zeros], axis=1),
                                jnp.concatenate([zeros, b], axis=1)], axis=0)

    def chunk_body(ic, carry):
        r0 = pl.multiple_of(ic * c, c)
        rows = pl.ds(r0, c)
        gc = _mm_hi(tri, g_ref[rows, :])
        beta = beta_ref[rows, :]
        pairs = range(N_HEADS // 2)
        heads = range(N_HEADS)

        def col(part, h):
            return qkvc_ref[rows, part * D_QK + h * HEAD_D:part * D_QK + (h + 1) * HEAD_D]

        g = [jnp.broadcast_to(gc[:, h:h + 1], (c, HEAD_D)) for h in heads]
        b = [jnp.broadcast_to(beta[:, h:h + 1], (c, HEAD_D)) for h in heads]
        kb = [col(1, h) * b[h] for h in heads]
        kq = []
        for pr in pairs:
            h1, h2 = 2 * pr, 2 * pr + 1
            lhs = jnp.concatenate([jnp.concatenate([kb[h1], kb[h2]], axis=1),
                                   jnp.concatenate([col(0, h1), col(0, h2)], axis=1)], axis=0)
            kq.append(_mm(lhs, side_by_side(col(1, h1), col(1, h2)), _NT))
        x, qkd, tk = [], [], []
        for pr in pairs:
            gcp = jnp.where(half, g[2 * pr + 1], g[2 * pr])
            rowp = jnp.sum(jnp.where(diag, gcp, 0.0), axis=0, keepdims=True)
            dec = jnp.where(causal, jnp.exp(jnp.where(causal, gcp - rowp, 0.0)), 0.0)
            x.append(jnp.where(strict, -(kq[pr][:c] * dec), 0.0))
            qkd.append(kq[pr][c:] * dec)
            tk.append(eye + x[pr])
        pk = [_mm(x[pr], blockdiag(x[pr])) for pr in pairs]
        for lvl in range(1, n_levels + 1):
            if lvl < n_levels:
                r = [_mm(jnp.concatenate([pk[pr], tk[pr]], axis=0), blockdiag(pk[pr])) for pr in pairs]
                pk = [r[pr][:c] for pr in pairs]
                tk = [tk[pr] + r[pr][c:] for pr in pairs]
            else:
                tk = [tk[pr] + _mm(tk[pr], blockdiag(pk[pr])) for pr in pairs]
        e = [jnp.exp(g[h]) for h in heads]
        sol = []
        for h in heads:
            rhs = jnp.concatenate([col(2, h) * b[h], kb[h] * e[h]], axis=1)
            rhs = jnp.concatenate([rhs, zeros2] if h % 2 == 0 else [zeros2, rhs], axis=0)
            sol.append(_mm(tk[h // 2], rhs))
        wq = [_mm(jnp.concatenate([sol[h][:, HEAD_D:], col(0, h) * e[h]], axis=0), s_ref[h])
              for h in heads]
        vn = [sol[h][:, :HEAD_D] - wq[h][:c] for h in heads]
        op = [_mm(qkd[pr], side_by_side(vn[2 * pr], vn[2 * pr + 1])) for pr in pairs]
        for h in heads:
            glast = gc[c - 1:c, h:h + 1]
            kdec = col(1, h) * jnp.exp(glast - g[h])
            s_ref[h] = s_ref[h] * jnp.exp(glast) + _mm(kdec, vn[h], _TN)
        for h in heads:
            o = _rms(wq[h][c:] + op[h // 2][:, (h % 2) * HEAD_D:(h % 2 + 1) * HEAD_D], nw_ref[...])
            cols = slice(h * HEAD_D, (h + 1) * HEAD_D)
            zz = z_ref[0, rows, cols]
            o_ref[0, rows, cols] = o * (zz * jax.nn.sigmoid(zz))
        return carry

    return chunk_body


def _delta_kernel(qkv_ref, z_ref, ab_ref, cbuf_ref, s0_ref, cw_ref, alog_ref, dtb_ref, nw_ref,
                  o_ref, nbuf_ref, snew_ref,
                  s_ref, xbuf_ref, qkvc_ref, g_ref, beta_ref, *, tt, c):
    t = pl.program_id(1)
    nt = pl.num_programs(1)
    pad = 8
    hist = CONV_W - 1

    @pl.when(t == 0)
    def _():
        s_ref[...] = s0_ref[0]
        xbuf_ref[pad - hist:pad, :] = cbuf_ref[0]

    xbuf_ref[pad:pad + tt, :] = qkv_ref[0]

    for part in range(3):
        for h in range(N_HEADS):
            c0 = part * D_QK + h * HEAD_D
            cols = slice(c0, c0 + HEAD_D)
            y = cw_ref[0:1, cols] * xbuf_ref[pad - 3:pad - 3 + tt, cols]
            for i in range(1, CONV_W):
                y = y + cw_ref[i:i + 1, cols] * xbuf_ref[pad - 3 + i:pad - 3 + i + tt, cols]
            qkvc_ref[:, cols] = _conv_silu_norm(y, part)

    tail = xbuf_ref[pad + tt - hist:pad + tt, :]

    @pl.when(t == nt - 1)
    def _():
        nbuf_ref[0] = tail

    xbuf_ref[pad - hist:pad, :] = tail

    a = ab_ref[0, :, 0:LANES]
    b = ab_ref[0, :, LANES:2 * LANES]
    g_ref[...] = -jnp.exp(alog_ref[...]) * jax.nn.softplus(a + dtb_ref[...])
    beta_ref[...] = jax.nn.sigmoid(b)

    refs = (qkvc_ref, g_ref, beta_ref, s_ref, z_ref, nw_ref, o_ref)
    lax.fori_loop(0, tt // c, _chunk_pairs(refs, c), 0)

    @pl.when(t == nt - 1)
    def _():
        snew_ref[0] = s_ref[...]


def _delta(proj3, ab3, conv_buf, s0, conv_w, alog, dtb, norm_w, tt, c):
    n, t, _ = proj3.shape
    kern = functools.partial(_delta_kernel, tt=tt, c=c)
    const2 = lambda i, j: (0, 0)
    return pl.pallas_call(
        kern,
        grid=(n, t // tt),
        in_specs=[
            pl.BlockSpec((1, tt, D_CONV), lambda i, j: (i, j, 0)),
            pl.BlockSpec((1, tt, D_QK), lambda i, j: (i, j, D_CONV // D_QK)),
            pl.BlockSpec((1, tt, 2 * LANES), lambda i, j: (i, j, 0)),
            pl.BlockSpec((1, CONV_W - 1, D_CONV), lambda i, j: (i, 0, 0)),
            pl.BlockSpec((1, N_HEADS, HEAD_D, HEAD_D), lambda i, j: (i, 0, 0, 0)),
            pl.BlockSpec((CONV_W, D_CONV), const2),
            pl.BlockSpec((1, LANES), const2),
            pl.BlockSpec((1, LANES), const2),
            pl.BlockSpec((1, HEAD_D), const2),
        ],
        out_specs=[
            pl.BlockSpec((1, tt, D_QK), lambda i, j: (i, j, 0)),
            pl.BlockSpec((1, CONV_W - 1, D_CONV), lambda i, j: (i, 0, 0)),
            pl.BlockSpec((1, N_HEADS, HEAD_D, HEAD_D), lambda i, j: (i, 0, 0, 0)),
        ],
        out_shape=[
            jax.ShapeDtypeStruct((n, t, D_QK), F32),
            jax.ShapeDtypeStruct((n, CONV_W - 1, D_CONV), F32),
            jax.ShapeDtypeStruct((n, N_HEADS, HEAD_D, HEAD_D), F32),
        ],
        scratch_shapes=[
            pltpu.VMEM((N_HEADS, HEAD_D, HEAD_D), F32),
            pltpu.VMEM((tt + 8, D_CONV), F32),
            pltpu.VMEM((tt, D_CONV), F32),
            pltpu.VMEM((tt, LANES), F32),
            pltpu.VMEM((tt, LANES), F32),
        ],
        compiler_params=pltpu.CompilerParams(
            dimension_semantics=("arbitrary", "arbitrary"), vmem_limit_bytes=VMEM_LIMIT),
        name="delta",
    )(proj3, proj3, ab3, conv_buf, s0, conv_w, alog, dtb, norm_w)


def _cmlp_kernel(uv_ref, lnw_ref, lnb_ref, ws_ref, bs_ref, o_ref, v_ref, *, c):
    t = pl.program_id(1)
    nt = pl.num_programs(1)
    x = uv_ref[0]
    gel = 0.5 * x * (1.0 + lax.erf(x * (2.0 ** -0.5)))
    u = gel[:, :D_B]
    v = gel[:, D_B:]
    vc = v - jnp.mean(v, axis=-1, keepdims=True)
    var = jnp.mean(jnp.square(vc), axis=-1, keepdims=True)
    v = vc * lax.rsqrt(var + 1e-5) * lnw_ref[...] + lnb_ref[...]

    @pl.when(t == nt - 1)
    def _():
        v_ref[0] = v

    ri = lax.broadcasted_iota(jnp.int32, (c, c), 0)
    ci = lax.broadcasted_iota(jnp.int32, (c, c), 1)
    gd = D_B // N_GROUPS
    for g in range(N_GROUPS):
        cols = slice(g * gd, (g + 1) * gd)
        ws = jnp.where(ri >= ci, ws_ref[g, 0:c, 0:c], 0.0)
        mixed = _mm(ws, v[:, cols]) + bs_ref[0:c, g:g + 1]
        o_ref[0, :, cols] = u[:, cols] * mixed


def _cmlp(proj3, ln_w, ln_b, w_spatial, bs_t, c):
    n, t, _ = proj3.shape
    kern = functools.partial(_cmlp_kernel, c=c)
    const2 = lambda i, j: (0, 0)
    return pl.pallas_call(
        kern,
        grid=(n, t // c),
        in_specs=[
            pl.BlockSpec((1, c, 2 * D_B), lambda i, j: (i, j, (D_CONV + D_QK) // (2 * D_B))),
            pl.BlockSpec((1, D_B), const2),
            pl.BlockSpec((1, D_B), const2),
            pl.BlockSpec((N_GROUPS, MLP_CHUNK, MLP_CHUNK), lambda i, j: (0, 0, 0)),
            pl.BlockSpec((MLP_CHUNK, LANES), const2),
        ],
        out_specs=[
            pl.BlockSpec((1, c, D_B), lambda i, j: (i, j, 0)),
            pl.BlockSpec((1, c, D_B), lambda i, j: (i, 0, 0)),
        ],
        out_shape=[
            jax.ShapeDtypeStruct((n, t, D_B), F32),
            jax.ShapeDtypeStruct((n, c, D_B), F32),
        ],
        compiler_params=pltpu.CompilerParams(
            dimension_semantics=("arbitrary", "arbitrary"), vmem_limit_bytes=VMEM_LIMIT),
        name="cmlp",
    )(proj3, ln_w, ln_b, w_spatial, bs_t)


def _merge_kernel(oa_ref, ob_ref, ga_ref, gb_ref, x_ref, wa_ref, wb_ref, wo_ref, nw_ref, y_ref):
    pa = _mm(oa_ref[...], wa_ref[...])
    pb = _mm(ob_ref[...], wb_ref[...])
    merged = jax.nn.sigmoid(ga_ref[...]) * pa + jax.nn.sigmoid(gb_ref[...]) * pb
    y = _mm(merged, wo_ref[...])
    y_ref[...] = x_ref[...] + _rms(y, nw_ref[...])


def _merge(oa, ob, proj, x2d, wa, wb, wo, nw, tm):
    m = x2d.shape[0]
    row = lambda i: (i, 0)
    const = lambda i: (0, 0)
    g0 = (D_CONV + D_QK + 2 * D_B) // D_MODEL
    return pl.pallas_call(
        _merge_kernel,
        grid=(m // tm,),
        in_specs=[
            pl.BlockSpec((tm, D_QK), row),
            pl.BlockSpec((tm, D_B), row),
            pl.BlockSpec((tm, D_MODEL), lambda i: (i, g0)),
            pl.BlockSpec((tm, D_MODEL), lambda i: (i, g0 + 1)),
            pl.BlockSpec((tm, D_MODEL), row),
            pl.BlockSpec((D_QK, D_MODEL), const),
            pl.BlockSpec((D_B, D_MODEL), const),
            pl.BlockSpec((D_MODEL, D_MODEL), const),
            pl.BlockSpec((1, D_MODEL), const),
        ],
        out_specs=pl.BlockSpec((tm, D_MODEL), row),
        out_shape=jax.ShapeDtypeStruct((m, D_MODEL), F32),
        compiler_params=pltpu.CompilerParams(
            dimension_semantics=("arbitrary",), vmem_limit_bytes=VMEM_LIMIT),
        name="merge",
    )(oa, ob, proj, proj, x2d, wa, wb, wo, nw)


def _ffn_kernel(x_ref, npre_ref, wg_ref, wu_ref, wd_ref, npost_ref, y_ref, h_ref, acc_ref):
    f = pl.program_id(1)

    @pl.when(f == 0)
    def _():
        h_ref[...] = _rms(x_ref[...], npre_ref[...]).astype(BF16)

    h = h_ref[...]
    gate = _mm(h, wg_ref[...])
    up = _mm(h, wu_ref[...])
    part = _mm(gate * jax.nn.sigmoid(gate) * up, wd_ref[...])

    @pl.when(f == 0)
    def _():
        acc_ref[...] = part

    @pl.when(f > 0)
    def _():
        acc_ref[...] += part

    @pl.when(f == pl.num_programs(1) - 1)
    def _():
        y_ref[...] = x_ref[...] + _rms(acc_ref[...], npost_ref[...])


def _ffn(x2d, npre, w_in, w_out, npost, tm, tf):
    m = x2d.shape[0]
    nf = D_FF // tf
    return pl.pallas_call(
        _ffn_kernel,
        grid=(m // tm, nf),
        in_specs=[
            pl.BlockSpec((tm, D_MODEL), lambda i, f: (i, 0)),
            pl.BlockSpec((1, D_MODEL), lambda i, f: (0, 0)),
            pl.BlockSpec((D_MODEL, tf), lambda i, f: (0, f)),
            pl.BlockSpec((D_MODEL, tf), lambda i, f: (0, f + nf)),
            pl.BlockSpec((tf, D_MODEL), lambda i, f: (f, 0)),
            pl.BlockSpec((1, D_MODEL), lambda i, f: (0, 0)),
        ],
        out_specs=pl.BlockSpec((tm, D_MODEL), lambda i, f: (i, 0)),
        out_shape=jax.ShapeDtypeStruct((m, D_MODEL), F32),
        scratch_shapes=[pltpu.VMEM((tm, D_MODEL), BF16), pltpu.VMEM((tm, D_MODEL), F32)],
        compiler_params=pltpu.CompilerParams(
            dimension_semantics=("arbitrary", "arbitrary"), vmem_limit_bytes=VMEM_LIMIT),
        name="ffn",
    )(x2d, npre, w_in, w_in, w_out, npost)


def _trunk_layer(x, conv_buf, s0, p, *, tt, c_delta, c_mlp, tm):
    n, t, _ = x.shape
    m = n * t
    x2d = x.reshape(m, D_MODEL)
    proj = _inproj(x2d, p["norm_pre_mix"], p["w_main"], tm, 2048)
    ab = _inproj_ab(x2d, p["norm_pre_mix"], p["w_ab"], tm)
    proj3 = proj.reshape(n, t, D_MAIN)
    delta_args = (proj3, ab.reshape(n, t, 2 * LANES), conv_buf, s0, p["conv_w"], p["a_log"],
                  p["dt_bias"], p["delta_norm_w"])
    if t == c_delta and t % 8 == 0 and t < DELTA_CHUNK:
        o_a, new_buf, s_new = _delta_dec(*delta_args, nb=DEC_SEQS_PER_STEP)
    else:
        o_a, new_buf, s_new = _delta(*delta_args, tt, c_delta)
    o_b, v_rows = _cmlp(proj3, p["sgu_ln_w"], p["sgu_ln_b"], p["w_spatial"], p["bs_t"], c_mlp)
    x1 = _merge(o_a.reshape(m, D_QK), o_b.reshape(m, D_B), proj, x2d, p["w_proj_a"], p["w_proj_b"],
                p["w_out"], p["norm_post_mix"], tm)
    x2 = _ffn(x1, p["norm_pre_ffn"], p["w_ffn_in"], p["w_ffn_out"], p["norm_post_ffn"], tm, 256)
    return x2.reshape(n, t, D_MODEL), s_new, new_buf, v_rows


def _pad_lanes(v):
    return jnp.pad(v, (0, LANES - v.shape[0])).reshape(1, LANES)


def _layer_params(l, norm_pre_mix, w_in, conv_w, a_log, dt_bias, delta_norm_w, sgu_ln_w, sgu_ln_b,
                  w_spatial, b_spatial, w_proj_a, w_proj_b, w_out, norm_post_mix, norm_pre_ffn,
                  w_ffn_in, w_ffn_out, norm_post_ffn):
    wi = w_in[l]
    w_main = jnp.concatenate([wi[:, :AB_OFF], wi[:, AB_OFF + 2 * N_HEADS:]], axis=1).astype(BF16)
    wa = jnp.pad(wi[:, AB_OFF:AB_OFF + N_HEADS], ((0, 0), (0, LANES - N_HEADS)))
    wb = jnp.pad(wi[:, AB_OFF + N_HEADS:AB_OFF + 2 * N_HEADS], ((0, 0), (0, LANES - N_HEADS)))
    row = lambda v: v.reshape(1, -1)
    return dict(
        norm_pre_mix=row(norm_pre_mix[l]), w_main=w_main, w_ab=jnp.concatenate([wa, wb], axis=1),
        conv_w=conv_w[l], a_log=_pad_lanes(a_log[l]), dt_bias=_pad_lanes(dt_bias[l]),
        delta_norm_w=row(delta_norm_w[l]), sgu_ln_w=row(sgu_ln_w[l]), sgu_ln_b=row(sgu_ln_b[l]),
        w_spatial=w_spatial[l],
        bs_t=jnp.pad(b_spatial[l].T, ((0, 0), (0, LANES - N_GROUPS))),
        w_proj_a=w_proj_a[l].astype(BF16), w_proj_b=w_proj_b[l].astype(BF16),
        w_out=w_out[l].astype(BF16), norm_post_mix=row(norm_post_mix[l]),
        norm_pre_ffn=row(norm_pre_ffn[l]), w_ffn_in=w_ffn_in[l].astype(BF16),
        w_ffn_out=w_ffn_out[l].astype(BF16), norm_post_ffn=row(norm_post_ffn[l]))


def kernel(x_prompt, x_sample, state_delta, state_conv, norm_pre_mix, w_in, conv_w, a_log, dt_bias,
           delta_norm_w, sgu_ln_w, sgu_ln_b, w_spatial, b_spatial, w_proj_a, w_proj_b, w_out,
           norm_post_mix, norm_pre_ffn, w_ffn_in, w_ffn_out, norm_post_ffn):
    depth = w_in.shape[0]
    nb, seq, _ = x_prompt.shape
    ndec, dec_seq, _ = x_sample.shape
    y_p, y_s = x_prompt, x_sample
    conv0 = jnp.zeros((nb, CONV_W - 1, D_CONV), x_prompt.dtype)
    s_zero = jnp.zeros((nb, N_HEADS, HEAD_D, HEAD_D), state_delta.dtype)
    sd_p, sc_p, cv_p, sd_s, sc_s, cv_s = [], [], [], [], [], []
    for l in range(depth):
        p = _layer_params(l, norm_pre_mix, w_in, conv_w, a_log, dt_bias, delta_norm_w, sgu_ln_w,
                          sgu_ln_b, w_spatial, b_spatial, w_proj_a, w_proj_b, w_out, norm_post_mix,
                          norm_pre_ffn, w_ffn_in, w_ffn_out, norm_post_ffn)
        y_p, s_new, buf_new, v_rows = _trunk_layer(
            y_p, conv0, s_zero, p, tt=min(256, seq), c_delta=min(DELTA_CHUNK, seq),
            c_mlp=min(MLP_CHUNK, seq), tm=512)
        sd_p.append(s_new)
        sc_p.append(buf_new)
        cv_p.append(v_rows)
        y_s, s_new, buf_new, v_rows = _trunk_layer(
            y_s, state_conv[l], state_delta[l], p, tt=min(256, dec_seq),
            c_delta=min(DELTA_CHUNK, dec_seq), c_mlp=min(MLP_CHUNK, dec_seq), tm=512)
        sd_s.append(s_new)
        sc_s.append(buf_new)
        cv_s.append(v_rows)
    return (y_p, y_s, jnp.stack(sd_p), jnp.stack(sc_p), jnp.stack(cv_p),
            jnp.stack(sd_s), jnp.stack(sc_s), jnp.stack(cv_s))
```

```python
import functools

import jax
import jax.numpy as jnp
from jax import lax
from jax.experimental import pallas as pl
from jax.experimental.pallas import tpu as pltpu

F32 = jnp.float32
BF16 = jnp.bfloat16

D_MODEL = 1024
N_HEADS = 8
HEAD_D = 128
D_QK = N_HEADS * HEAD_D
D_CONV = 3 * D_QK
CONV_W = 4
DELTA_CHUNK = 64
MLP_CHUNK = 128
N_GROUPS = 8
D_B = 1024
D_FF = 2816
D_MAIN = D_CONV + D_QK + 2 * D_B + 2 * D_MODEL
AB_OFF = D_CONV + D_QK
LANES = 128
SUBLANES = 8

TM = 512
TN_INPROJ = 2048
TF_FFN = 256
TT_DELTA = 256
DEC_SEQS_PER_STEP = 4
VMEM_LIMIT = 48 * 1024 * 1024

_HI = lax.Precision.HIGHEST
_NT = (((1,), (1,)), ((), ()))
_TN = (((0,), (0,)), ((), ()))
_NN = (((1,), (0,)), ((), ()))


def _mm(a, b, dims=_NN):
    return lax.dot_general(a.astype(BF16), b.astype(BF16), dims, preferred_element_type=F32)


def _mm_hi(a, b, dims=_NN):
    return lax.dot_general(a, b, dims, precision=_HI, preferred_element_type=F32)


def _rms(x, w, eps=1e-6):
    return x * lax.rsqrt(jnp.mean(jnp.square(x), axis=-1, keepdims=True) + eps) * w


def _params(sem):
    return pltpu.CompilerParams(dimension_semantics=sem, vmem_limit_bytes=VMEM_LIMIT)


def _layer_spec(shape, l, ngrid):
    zeros = (0,) * len(shape)
    if ngrid == 1:
        return pl.BlockSpec((None,) + shape, lambda i: (l,) + zeros)
    return pl.BlockSpec((None,) + shape, lambda i, j: (l,) + zeros)


def _inproj_kernel(x_ref, nw_ref, w_ref, wab_ref, o_ref, ab_ref, h_ref):
    @pl.when(pl.program_id(1) == 0)
    def _():
        h = _rms(x_ref[...], nw_ref[...]).astype(BF16)
        h_ref[...] = h
        ab_ref[...] = _mm(h, wab_ref[...])

    o_ref[...] = _mm(h_ref[...], w_ref[...])


def _inproj(x2d, p, l):
    m = x2d.shape[0]
    tm, tn = min(TM, m), TN_INPROJ
    return pl.pallas_call(
        _inproj_kernel,
        grid=(m // tm, D_MAIN // tn),
        in_specs=[
            pl.BlockSpec((tm, D_MODEL), lambda i, j: (i, 0)),
            _layer_spec((1, D_MODEL), l, 2),
            pl.BlockSpec((None, D_MODEL, tn), lambda i, j: (l, 0, j)),
            _layer_spec((D_MODEL, LANES), l, 2),
        ],
        out_specs=[
            pl.BlockSpec((tm, tn), lambda i, j: (i, j)),
            pl.BlockSpec((tm, LANES), lambda i, j: (i, 0)),
        ],
        out_shape=[
            jax.ShapeDtypeStruct((m, D_MAIN), F32),
            jax.ShapeDtypeStruct((m, LANES), F32),
        ],
        scratch_shapes=[pltpu.VMEM((tm, D_MODEL), BF16)],
        compiler_params=_params(("arbitrary", "arbitrary")),
        name="inproj",
    )(x2d, p["norm_pre_mix"], p["w_main"], p["w_ab"])


def _conv_silu_norm(y, part):
    y = y * jax.nn.sigmoid(y)
    if part < 2:
        y = y * lax.rsqrt(jnp.sum(jnp.square(y), axis=-1, keepdims=True) + 1e-6)
    if part == 0:
        y = y * (HEAD_D ** -0.5)
    return y


def _delta_dec_kernel(qkv_ref, z_ref, ab_ref, cbuf_ref, s0_ref, cw_ref, alog_ref, dtb_ref, nw_ref,
                      *rest, nb, c, aliased):
    if aliased:
        rest = rest[1:]
    o_ref, nbuf_ref, snew_ref, xbuf_ref, qkvc_ref = rest
    pad = SUBLANES
    hist = CONV_W - 1
    xbuf_ref[:, pad - hist:pad, :] = cbuf_ref[...]
    xbuf_ref[:, pad:pad + c, :] = qkv_ref[...]
    for part in range(3):
        for h in range(N_HEADS):
            c0 = part * D_QK + h * HEAD_D
            cols = slice(c0, c0 + HEAD_D)
            y = cw_ref[0:1, cols] * xbuf_ref[:, pad - 3:pad - 3 + c, cols]
            for i in range(1, CONV_W):
                y = y + cw_ref[i:i + 1, cols] * xbuf_ref[:, pad - 3 + i:pad - 3 + i + c, cols]
            qkvc_ref[:, :, cols] = _conv_silu_norm(y, part)
    nbuf_ref[...] = xbuf_ref[:, pad + c - hist:pad + c, :]

    ri = lax.broadcasted_iota(jnp.int32, (c, c), 0)
    ci = lax.broadcasted_iota(jnp.int32, (c, c), 1)
    causal = ri >= ci
    strict = ri > ci
    diag = ri == ci
    tri = jnp.where(causal, 1.0, 0.0).astype(F32)
    eye = jnp.where(diag, 1.0, 0.0).astype(F32)
    n_levels = c.bit_length() - 2
    items = [(i, h) for i in range(nb) for h in range(N_HEADS)]
    idx = {it: n for n, it in enumerate(items)}
    every = range(len(items))

    gc, beta = [], []
    for i in range(nb):
        g = -jnp.exp(alog_ref[...]) * jax.nn.softplus(ab_ref[i] + dtb_ref[...])
        gc.append(_mm_hi(tri, g))
        beta.append(jax.nn.sigmoid(ab_ref[i]))

    def col(part, i, h):
        return qkvc_ref[i, :, part * D_QK + h * HEAD_D:part * D_QK + (h + 1) * HEAD_D]

    gcol = [gc[i][:, h:h + 1] for i, h in items]
    bcol = [beta[i][:, N_HEADS + h:N_HEADS + h + 1] for i, h in items]
    kb = [col(1, i, h) * bcol[idx[i, h]] for i, h in items]
    kq = [_mm(jnp.concatenate([kb[idx[i, h]], col(0, i, h)], axis=0), col(1, i, h), _NT)
          for i, h in items]
    x, qk, tk = [], [], []
    for n in every:
        grow = jnp.sum(jnp.where(diag, gcol[n], 0.0), axis=0, keepdims=True)
        dec = jnp.where(causal, jnp.exp(jnp.where(causal, gcol[n] - grow, 0.0)), 0.0)
        x.append(jnp.where(strict, -(kq[n][:c] * dec), 0.0))
        qk.append(kq[n][c:] * dec)
        tk.append(eye + x[n])
    pk = [_mm(x[n], x[n]) for n in every]
    for lvl in range(1, n_levels + 1):
        if lvl < n_levels:
            r = [_mm(jnp.concatenate([pk[n], tk[n]], axis=0), pk[n]) for n in every]
            pk = [r[n][:c] for n in every]
            tk = [tk[n] + r[n][c:] for n in every]
        else:
            tk = [tk[n] + _mm(tk[n], pk[n]) for n in every]
    e = [jnp.exp(gcol[n]) for n in every]
    sol = [_mm(tk[idx[i, h]], jnp.concatenate([col(2, i, h) * bcol[idx[i, h]],
                                                kb[idx[i, h]] * e[idx[i, h]]], axis=1))
           for i, h in items]
    wq = [_mm(jnp.concatenate([sol[idx[i, h]][:, HEAD_D:], col(0, i, h) * e[idx[i, h]]], axis=0),
              s0_ref[i, h]) for i, h in items]
    vn = [sol[n][:, :HEAD_D] - wq[n][:c] for n in every]
    op = [_mm(qk[n], vn[n]) for n in every]
    for i, h in items:
        n = idx[i, h]
        glast = gc[i][c - 1:c, h:h + 1]
        kdec = col(1, i, h) * jnp.exp(glast - gcol[n])
        snew_ref[i, h] = s0_ref[i, h] * jnp.exp(glast) + _mm(kdec, vn[n], _TN)
    for i, h in items:
        n = idx[i, h]
        o = _rms(wq[n][c:] + op[n], nw_ref[...])
        cols = slice(h * HEAD_D, (h + 1) * HEAD_D)
        zz = z_ref[i, :, cols]
        o_ref[i, :, cols] = o * (zz * jax.nn.sigmoid(zz))


def _delta_dec(proj3, ab3, conv_all, s_all, p, l, s_out_prev):
    n, c, _ = proj3.shape
    depth = s_all.shape[0]
    nb = DEC_SEQS_PER_STEP
    aliased = s_out_prev is not None
    kern = functools.partial(_delta_dec_kernel, nb=nb, c=c, aliased=aliased)
    state_blk = (None, nb, N_HEADS, HEAD_D, HEAD_D)
    in_specs = [
        pl.BlockSpec((nb, c, D_CONV), lambda i: (i, 0, 0)),
        pl.BlockSpec((nb, c, D_QK), lambda i: (i, 0, D_CONV // D_QK)),
        pl.BlockSpec((nb, c, LANES), lambda i: (i, 0, 0)),
        pl.BlockSpec((None, nb, CONV_W - 1, D_CONV), lambda i: (l, i, 0, 0)),
        pl.BlockSpec(state_blk, lambda i: (l, i, 0, 0, 0)),
        _layer_spec((CONV_W, D_CONV), l, 1),
        _layer_spec((1, LANES), l, 1),
        _layer_spec((1, LANES), l, 1),
        _layer_spec((1, HEAD_D), l, 1),
    ]
    args = [proj3, proj3, ab3, conv_all, s_all, p["conv_w"], p["a_log"], p["dt_bias"], p["delta_norm_w"]]
    aliases = {}
    if aliased:
        in_specs.append(pl.BlockSpec(memory_space=pl.ANY))
        args.append(s_out_prev)
        aliases = {len(args) - 1: 2}
    return pl.pallas_call(
        kern,
        grid=(n // nb,),
        in_specs=in_specs,
        out_specs=[
            pl.BlockSpec((nb, c, D_QK), lambda i: (i, 0, 0)),
            pl.BlockSpec((nb, CONV_W - 1, D_CONV), lambda i: (i, 0, 0)),
            pl.BlockSpec(state_blk, lambda i: (l, i, 0, 0, 0)),
        ],
        out_shape=[
            jax.ShapeDtypeStruct((n, c, D_QK), F32),
            jax.ShapeDtypeStruct((n, CONV_W - 1, D_CONV), F32),
            jax.ShapeDtypeStruct((depth, n, N_HEADS, HEAD_D, HEAD_D), F32),
        ],
        scratch_shapes=[
            pltpu.VMEM((nb, c + SUBLANES, D_CONV), F32),
            pltpu.VMEM((nb, c, D_CONV), F32),
        ],
        input_output_aliases=aliases,
        compiler_params=_params(("arbitrary",)),
        name="delta_dec",
    )(*args)


def _chunk_pairs(refs, c):
    qkvc_ref, g_ref, beta_ref, s_ref, z_ref, nw_ref, o_ref = refs
    assert 2 * c == LANES
    ri = lax.broadcasted_iota(jnp.int32, (c, LANES), 0)
    lane = lax.broadcasted_iota(jnp.int32, (c, LANES), 1)
    half = lane >= c
    cj = jnp.where(half, lane - c, lane)
    causal = ri >= cj
    strict = ri > cj
    diag = ri == cj
    eye = jnp.where(diag, 1.0, 0.0).astype(F32)
    rt = lax.broadcasted_iota(jnp.int32, (c, c), 0)
    ct = lax.broadcasted_iota(jnp.int32, (c, c), 1)
    tri = jnp.where(rt >= ct, 1.0, 0.0).astype(F32)
    zeros = jnp.zeros((c, HEAD_D), F32)
    zeros2 = jnp.zeros((c, 2 * HEAD_D), F32)
    n_levels = c.bit_length() - 2

    def blockdiag(m):
        return jnp.concatenate([jnp.where(half, 0.0, m), jnp.where(half, m, 0.0)], axis=0)

    def side_by_side(a, b):
        return jnp.concatenate([jnp.concatenate([a, zeros], axis=1),
                                jnp.concatenate([zeros, b], axis=1)], axis=0)

    def chunk_body(ic, carry):
        r0 = pl.multiple_of(ic * c, c)
        rows = pl.ds(r0, c)
        gc = _mm_hi(tri, g_ref[rows, :])
        beta = beta_ref[rows, :]
        pairs = range(N_HEADS // 2)
        heads = range(N_HEADS)

        def col(part, h):
            return qkvc_ref[rows, part * D_QK + h * HEAD_D:part * D_QK + (h + 1) * HEAD_D]

        g = [jnp.broadcast_to(gc[:, h:h + 1], (c, HEAD_D)) for h in heads]
        b = [jnp.broadcast_to(beta[:, N_HEADS + h:N_HEADS + h + 1], (c, HEAD_D)) for h in heads]
        kb = [col(1, h) * b[h] for h in heads]
        kq = []
        for pr in pairs:
            h1, h2 = 2 * pr, 2 * pr + 1
            lhs = jnp.concatenate([jnp.concatenate([kb[h1], kb[h2]], axis=1),
                                   jnp.concatenate([col(0, h1), col(0, h2)], axis=1)], axis=0)
            kq.append(_mm(lhs, side_by_side(col(1, h1), col(1, h2)), _NT))
        x, qkd, tk = [], [], []
        for pr in pairs:
            gcp = jnp.where(half, g[2 * pr + 1], g[2 * pr])
            rowp = jnp.sum(jnp.where(diag, gcp, 0.0), axis=0, keepdims=True)
            dec = jnp.where(causal, jnp.exp(jnp.where(causal, gcp - rowp, 0.0)), 0.0)
            x.append(jnp.where(strict, -(kq[pr][:c] * dec), 0.0))
            qkd.append(kq[pr][c:] * dec)
            tk.append(eye + x[pr])
        pk = [_mm(x[pr], blockdiag(x[pr])) for pr in pairs]
        for lvl in range(1, n_levels + 1):
            if lvl < n_levels:
                r = [_mm(jnp.concatenate([pk[pr], tk[pr]], axis=0), blockdiag(pk[pr])) for pr in pairs]
                pk = [r[pr][:c] for pr in pairs]
                tk = [tk[pr] + r[pr][c:] for pr in pairs]
            else:
                tk = [tk[pr] + _mm(tk[pr], blockdiag(pk[pr])) for pr in pairs]
        e = [jnp.exp(g[h]) for h in heads]
        sol = []
        for h in heads:
            rhs = jnp.concatenate([col(2, h) * b[h], kb[h] * e[h]], axis=1)
            rhs = jnp.concatenate([rhs, zeros2] if h % 2 == 0 else [zeros2, rhs], axis=0)
            sol.append(_mm(tk[h // 2], rhs))
        wq = [_mm(jnp.concatenate([sol[h][:, HEAD_D:], col(0, h) * e[h]], axis=0), s_ref[h])
              for h in heads]
        vn = [sol[h][:, :HEAD_D] - wq[h][:c] for h in heads]
        op = [_mm(qkd[pr], side_by_side(vn[2 * pr], vn[2 * pr + 1])) for pr in pairs]
        for h in heads:
            glast = gc[c - 1:c, h:h + 1]
            kdec = col(1, h) * jnp.exp(glast - g[h])
            s_ref[h] = s_ref[h] * jnp.exp(glast) + _mm(kdec, vn[h], _TN)
        for h in heads:
            o = _rms(wq[h][c:] + op[h // 2][:, (h % 2) * HEAD_D:(h % 2 + 1) * HEAD_D], nw_ref[...])
            cols = slice(h * HEAD_D, (h + 1) * HEAD_D)
            zz = z_ref[0, rows, cols]
            o_ref[0, rows, cols] = o * (zz * jax.nn.sigmoid(zz))
        return carry

    return chunk_body


def _delta_kernel(qkv_ref, z_ref, ab_ref, cbuf_ref, s0_ref, cw_ref, alog_ref, dtb_ref, nw_ref,
                  o_ref, nbuf_ref, snew_ref,
                  s_ref, xbuf_ref, qkvc_ref, g_ref, beta_ref, *, tt, c):
    t = pl.program_id(1)
    nt = pl.num_programs(1)
    pad = SUBLANES
    hist = CONV_W - 1

    @pl.when(t == 0)
    def _():
        s_ref[...] = s0_ref[0]
        xbuf_ref[pad - hist:pad, :] = cbuf_ref[0]

    xbuf_ref[pad:pad + tt, :] = qkv_ref[0]

    for part in range(3):
        for h in range(N_HEADS):
            c0 = part * D_QK + h * HEAD_D
            cols = slice(c0, c0 + HEAD_D)
            y = cw_ref[0:1, cols] * xbuf_ref[pad - 3:pad - 3 + tt, cols]
            for i in range(1, CONV_W):
                y = y + cw_ref[i:i + 1, cols] * xbuf_ref[pad - 3 + i:pad - 3 + i + tt, cols]
            qkvc_ref[:, cols] = _conv_silu_norm(y, part)

    tail = xbuf_ref[pad + tt - hist:pad + tt, :]

    @pl.when(t == nt - 1)
    def _():
        nbuf_ref[0] = tail

    xbuf_ref[pad - hist:pad, :] = tail

    ab = ab_ref[0]
    g_ref[...] = -jnp.exp(alog_ref[...]) * jax.nn.softplus(ab + dtb_ref[...])
    beta_ref[...] = jax.nn.sigmoid(ab)

    refs = (qkvc_ref, g_ref, beta_ref, s_ref, z_ref, nw_ref, o_ref)
    lax.fori_loop(0, tt // c, _chunk_pairs(refs, c), 0)

    @pl.when(t == nt - 1)
    def _():
        snew_ref[0] = s_ref[...]


def _delta(proj3, ab3, conv_buf, s0, p, l):
    n, t, _ = proj3.shape
    tt, c = min(TT_DELTA, t), DELTA_CHUNK
    kern = functools.partial(_delta_kernel, tt=tt, c=c)
    return pl.pallas_call(
        kern,
        grid=(n, t // tt),
        in_specs=[
            pl.BlockSpec((1, tt, D_CONV), lambda i, j: (i, j, 0)),
            pl.BlockSpec((1, tt, D_QK), lambda i, j: (i, j, D_CONV // D_QK)),
            pl.BlockSpec((1, tt, LANES), lambda i, j: (i, j, 0)),
            pl.BlockSpec((1, CONV_W - 1, D_CONV), lambda i, j: (i, 0, 0)),
            pl.BlockSpec((1, N_HEADS, HEAD_D, HEAD_D), lambda i, j: (i, 0, 0, 0)),
            _layer_spec((CONV_W, D_CONV), l, 2),
            _layer_spec((1, LANES), l, 2),
            _layer_spec((1, LANES), l, 2),
            _layer_spec((1, HEAD_D), l, 2),
        ],
        out_specs=[
            pl.BlockSpec((1, tt, D_QK), lambda i, j: (i, j, 0)),
            pl.BlockSpec((1, CONV_W - 1, D_CONV), lambda i, j: (i, 0, 0)),
            pl.BlockSpec((1, N_HEADS, HEAD_D, HEAD_D), lambda i, j: (i, 0, 0, 0)),
        ],
        out_shape=[
            jax.ShapeDtypeStruct((n, t, D_QK), F32),
            jax.ShapeDtypeStruct((n, CONV_W - 1, D_CONV), F32),
            jax.ShapeDtypeStruct((n, N_HEADS, HEAD_D, HEAD_D), F32),
        ],
        scratch_shapes=[
            pltpu.VMEM((N_HEADS, HEAD_D, HEAD_D), F32),
            pltpu.VMEM((tt + SUBLANES, D_CONV), F32),
            pltpu.VMEM((tt, D_CONV), F32),
            pltpu.VMEM((tt, LANES), F32),
            pltpu.VMEM((tt, LANES), F32),
        ],
        compiler_params=_params(("arbitrary", "arbitrary")),
        name="delta",
    )(proj3, proj3, ab3, conv_buf, s0, p["conv_w"], p["a_log"], p["dt_bias"], p["delta_norm_w"])


def _cmlp_kernel(uv_ref, lnw_ref, lnb_ref, ws_ref, bs_ref, o_ref, v_ref):
    t = pl.program_id(1)
    nt = pl.num_programs(1)
    c = MLP_CHUNK
    x = uv_ref[0]
    gel = 0.5 * x * (1.0 + lax.erf(x * (2.0 ** -0.5)))
    u = gel[:, :D_B]
    v = gel[:, D_B:]
    vc = v - jnp.mean(v, axis=-1, keepdims=True)
    var = jnp.mean(jnp.square(vc), axis=-1, keepdims=True)
    v = vc * lax.rsqrt(var + 1e-5) * lnw_ref[...] + lnb_ref[...]

    @pl.when(t == nt - 1)
    def _():
        v_ref[0] = v

    ri = lax.broadcasted_iota(jnp.int32, (c, c), 0)
    ci = lax.broadcasted_iota(jnp.int32, (c, c), 1)
    gd = D_B // N_GROUPS
    for g in range(N_GROUPS):
        cols = slice(g * gd, (g + 1) * gd)
        ws = jnp.where(ri >= ci, ws_ref[g], 0.0)
        mixed = _mm(ws, v[:, cols]) + bs_ref[:, g:g + 1]
        o_ref[0, :, cols] = u[:, cols] * mixed


def _cmlp(proj3, p, l, ws_key, bs_key):
    n, t, _ = proj3.shape
    c = MLP_CHUNK
    return pl.pallas_call(
        _cmlp_kernel,
        grid=(n, t // c),
        in_specs=[
            pl.BlockSpec((1, c, 2 * D_B), lambda i, j: (i, j, (D_CONV + D_QK) // (2 * D_B))),
            _layer_spec((1, D_B), l, 2),
            _layer_spec((1, D_B), l, 2),
            _layer_spec((N_GROUPS, c, c), l, 2),
            _layer_spec((c, LANES), l, 2),
        ],
        out_specs=[
            pl.BlockSpec((1, c, D_B), lambda i, j: (i, j, 0)),
            pl.BlockSpec((1, c, D_B), lambda i, j: (i, 0, 0)),
        ],
        out_shape=[
            jax.ShapeDtypeStruct((n, t, D_B), F32),
            jax.ShapeDtypeStruct((n, c, D_B), F32),
        ],
        compiler_params=_params(("arbitrary", "arbitrary")),
        name="cmlp",
    )(proj3, p["sgu_ln_w"], p["sgu_ln_b"], p[ws_key], p[bs_key])


def _merge_kernel(oa_ref, ob_ref, ga_ref, gb_ref, x_ref, wa_ref, wb_ref, wo_ref, nw_ref, y_ref):
    pa = _mm(oa_ref[...], wa_ref[...])
    pb = _mm(ob_ref[...], wb_ref[...])
    merged = jax.nn.sigmoid(ga_ref[...]) * pa + jax.nn.sigmoid(gb_ref[...]) * pb
    y = _mm(merged, wo_ref[...])
    y_ref[...] = x_ref[...] + _rms(y, nw_ref[...])


def _merge(oa, ob, proj, x2d, p, l):
    m = x2d.shape[0]
    tm = min(TM, m)
    row = lambda i: (i, 0)
    g0 = (D_CONV + D_QK + 2 * D_B) // D_MODEL
    return pl.pallas_call(
        _merge_kernel,
        grid=(m // tm,),
        in_specs=[
            pl.BlockSpec((tm, D_QK), row),
            pl.BlockSpec((tm, D_B), row),
            pl.BlockSpec((tm, D_MODEL), lambda i: (i, g0)),
            pl.BlockSpec((tm, D_MODEL), lambda i: (i, g0 + 1)),
            pl.BlockSpec((tm, D_MODEL), row),
            _layer_spec((D_QK, D_MODEL), l, 1),
            _layer_spec((D_B, D_MODEL), l, 1),
            _layer_spec((D_MODEL, D_MODEL), l, 1),
            _layer_spec((1, D_MODEL), l, 1),
        ],
        out_specs=pl.BlockSpec((tm, D_MODEL), row),
        out_shape=jax.ShapeDtypeStruct((m, D_MODEL), F32),
        compiler_params=_params(("arbitrary",)),
        name="merge",
    )(oa, ob, proj, proj, x2d, p["w_proj_a"], p["w_proj_b"], p["w_out"], p["norm_post_mix"])


def _ffn_kernel(x_ref, npre_ref, wi_ref, wd_ref, npost_ref, y_ref, act_ref):
    x = x_ref[...]
    h = _rms(x, npre_ref[...]).astype(BF16)
    tf = TF_FFN
    for f in range(D_FF // tf):
        gate = _mm(h, wi_ref[:, f * tf:(f + 1) * tf])
        up = _mm(h, wi_ref[:, D_FF + f * tf:D_FF + (f + 1) * tf])
        act_ref[:, f * tf:(f + 1) * tf] = (gate * jax.nn.sigmoid(gate) * up).astype(BF16)
    y = _mm(act_ref[...], wd_ref[...])
    y_ref[...] = x + _rms(y, npost_ref[...])


def _ffn(x2d, p, l):
    m = x2d.shape[0]
    tm = min(TM, m)
    resident = dict(pipeline_mode=pl.Buffered(1))
    return pl.pallas_call(
        _ffn_kernel,
        grid=(m // tm,),
        in_specs=[
            pl.BlockSpec((tm, D_MODEL), lambda i: (i, 0)),
            _layer_spec((1, D_MODEL), l, 1),
            pl.BlockSpec((None, D_MODEL, 2 * D_FF), lambda i: (l, 0, 0), **resident),
            pl.BlockSpec((None, D_FF, D_MODEL), lambda i: (l, 0, 0), **resident),
            _layer_spec((1, D_MODEL), l, 1),
        ],
        out_specs=pl.BlockSpec((tm, D_MODEL), lambda i: (i, 0)),
        out_shape=jax.ShapeDtypeStruct((m, D_MODEL), F32),
        scratch_shapes=[pltpu.VMEM((tm, D_FF), BF16)],
        compiler_params=_params(("arbitrary",)),
        name="ffn",
    )(x2d, p["norm_pre_ffn"], p["w_ffn_in"], p["w_ffn_out"], p["norm_post_ffn"])


def _trunk_layer(x, p, l, delta_fn):
    n, t, _ = x.shape
    m = n * t
    x2d = x.reshape(m, D_MODEL)
    proj, ab = _inproj(x2d, p, l)
    o_a, new_buf, s_new = delta_fn(proj.reshape(n, t, D_MAIN), ab.reshape(n, t, LANES))
    if t % MLP_CHUNK == 0:
        o_b, v_rows = _cmlp(proj.reshape(n, t, D_MAIN), p, l, "w_spatial", "bs_t")
    else:
        assert MLP_CHUNK % t == 0 and m % MLP_CHUNK == 0
        o_b, v_rows = _cmlp(proj.reshape(m // MLP_CHUNK, MLP_CHUNK, D_MAIN), p, l, "ws_short", "bs_short")
        v_rows = v_rows.reshape(n, t, D_B)
    x1 = _merge(o_a.reshape(m, D_QK), o_b.reshape(m, D_B), proj, x2d, p, l)
    x2 = _ffn(x1, p, l)
    return x2.reshape(n, t, D_MODEL), s_new, new_buf, v_rows


def _prepare_params(t_short, norm_pre_mix, w_in, conv_w, a_log, dt_bias, delta_norm_w, sgu_ln_w,
                    sgu_ln_b, w_spatial, b_spatial, w_proj_a, w_proj_b, w_out, norm_post_mix,
                    norm_pre_ffn, w_ffn_in, w_ffn_out, norm_post_ffn):
    depth = w_in.shape[0]
    w_main = jnp.concatenate([w_in[:, :, :AB_OFF], w_in[:, :, AB_OFF + 2 * N_HEADS:]], axis=2)
    w_ab = jnp.pad(w_in[:, :, AB_OFF:AB_OFF + 2 * N_HEADS], ((0, 0), (0, 0), (0, LANES - 2 * N_HEADS)))
    row = lambda v: v.reshape(depth, 1, -1)
    lanes = lambda v: jnp.pad(v, ((0, 0), (0, LANES - v.shape[1]))).reshape(depth, 1, LANES)
    bs_t = jnp.pad(jnp.swapaxes(b_spatial, 1, 2), ((0, 0), (0, 0), (0, LANES - N_GROUPS)))
    rep = MLP_CHUNK // t_short
    eye = jnp.eye(rep, dtype=F32)
    ws_short = jnp.einsum("ab,lgij->lgaibj", eye, w_spatial[:, :, :t_short, :t_short]).reshape(
        depth, N_GROUPS, MLP_CHUNK, MLP_CHUNK)
    return dict(
        norm_pre_mix=row(norm_pre_mix), w_main=w_main.astype(BF16), w_ab=w_ab.astype(BF16),
        conv_w=conv_w, a_log=lanes(a_log), dt_bias=lanes(dt_bias), delta_norm_w=row(delta_norm_w),
        sgu_ln_w=row(sgu_ln_w), sgu_ln_b=row(sgu_ln_b), w_spatial=w_spatial, bs_t=bs_t,
        ws_short=ws_short, bs_short=jnp.tile(bs_t[:, :t_short], (1, rep, 1)),
        w_proj_a=w_proj_a.astype(BF16), w_proj_b=w_proj_b.astype(BF16), w_out=w_out.astype(BF16),
        norm_post_mix=row(norm_post_mix), norm_pre_ffn=row(norm_pre_ffn),
        w_ffn_in=w_ffn_in.astype(BF16), w_ffn_out=w_ffn_out.astype(BF16),
        norm_post_ffn=row(norm_post_ffn))


def kernel(x_prompt, x_sample, state_delta, state_conv, norm_pre_mix, w_in, conv_w, a_log, dt_bias,
           delta_norm_w, sgu_ln_w, sgu_ln_b, w_spatial, b_spatial, w_proj_a, w_proj_b, w_out,
           norm_post_mix, norm_pre_ffn, w_ffn_in, w_ffn_out, norm_post_ffn):
    depth = w_in.shape[0]
    nb, seq, _ = x_prompt.shape
    ndec, dec_seq, _ = x_sample.shape
    assert seq % DELTA_CHUNK == 0 and seq % MLP_CHUNK == 0
    assert dec_seq % SUBLANES == 0 and dec_seq < DELTA_CHUNK and ndec % DEC_SEQS_PER_STEP == 0
    p = _prepare_params(dec_seq, norm_pre_mix, w_in, conv_w, a_log, dt_bias, delta_norm_w, sgu_ln_w,
                        sgu_ln_b, w_spatial, b_spatial, w_proj_a, w_proj_b, w_out, norm_post_mix,
                        norm_pre_ffn, w_ffn_in, w_ffn_out, norm_post_ffn)
    y_p, y_s = x_prompt, x_sample
    conv0 = jnp.zeros((nb, CONV_W - 1, D_CONV), x_prompt.dtype)
    s_zero = jnp.zeros((nb, N_HEADS, HEAD_D, HEAD_D), state_delta.dtype)
    sd_p, sc_p, cv_p, sc_s, cv_s = [], [], [], [], []
    sd_s = None
    for l in range(depth):
        y_p, s_new, buf_new, v_rows = _trunk_layer(
            y_p, p, l, lambda proj3, ab3: _delta(proj3, ab3, conv0, s_zero, p, l))
        sd_p.append(s_new)
        sc_p.append(buf_new)
        cv_p.append(v_rows)
        y_s, sd_s, buf_new, v_rows = _trunk_layer(
            y_s, p, l, lambda proj3, ab3: _delta_dec(proj3, ab3, state_conv, state_delta, p, l, sd_s))
        sc_s.append(buf_new)
        cv_s.append(v_rows)
    return (y_p, y_s, jnp.stack(sd_p), jnp.stack(sc_p), jnp.stack(cv_p),
            sd_s, jnp.stack(sc_s), jnp.stack(cv_s))
```

```python
import functools

import jax
import jax.numpy as jnp
from jax import lax
from jax.experimental import pallas as pl
from jax.experimental.pallas import tpu as pltpu

F32 = jnp.float32
BF16 = jnp.bfloat16

D_MODEL = 1024
N_HEADS = 8
HEAD_D = 128
D_QK = N_HEADS * HEAD_D
D_CONV = 3 * D_QK
CONV_W = 4
DELTA_CHUNK = 64
MLP_CHUNK = 128
N_GROUPS = 8
D_B = 1024
D_FF = 2816
D_MAIN = D_CONV + D_QK + 2 * D_B + 2 * D_MODEL
AB_OFF = D_CONV + D_QK
LANES = 128
SUBLANES = 8

TM = 512
TN_INPROJ = 2048
TF_FFN = 256
TT_DELTA = 128
SEQS_DELTA = 4
DEC_SEQS_PER_STEP = 4
VMEM_LIMIT = 48 * 1024 * 1024

_HI = lax.Precision.HIGHEST
_NT = (((1,), (1,)), ((), ()))
_TN = (((0,), (0,)), ((), ()))
_NN = (((1,), (0,)), ((), ()))


def _mm(a, b, dims=_NN):
    return lax.dot_general(a.astype(BF16), b.astype(BF16), dims, preferred_element_type=F32)


def _mm_hi(a, b, dims=_NN):
    return lax.dot_general(a, b, dims, precision=_HI, preferred_element_type=F32)


def _rms(x, w, eps=1e-6):
    return x * lax.rsqrt(jnp.mean(jnp.square(x), axis=-1, keepdims=True) + eps) * w


def _params(sem):
    return pltpu.CompilerParams(dimension_semantics=sem, vmem_limit_bytes=VMEM_LIMIT)


def _layer_spec(shape, l, ngrid):
    zeros = (0,) * len(shape)
    if ngrid == 1:
        return pl.BlockSpec((None,) + shape, lambda i: (l,) + zeros)
    return pl.BlockSpec((None,) + shape, lambda i, j: (l,) + zeros)


def _inproj_kernel(x_ref, nw_ref, wlo_ref, whi_ref, wab_ref, o_ref, ab_ref, h_ref, *, tn):
    j = pl.program_id(1)
    n_lo = AB_OFF // tn

    @pl.when(j == 0)
    def _():
        h = _rms(x_ref[...], nw_ref[...]).astype(BF16)
        h_ref[...] = h
        ab_ref[...] = _mm(h, wab_ref[...])

    @pl.when(j < n_lo)
    def _():
        o_ref[...] = _mm(h_ref[...], wlo_ref[:, pl.ds(pl.multiple_of(j * tn, tn), tn)])

    @pl.when(j >= n_lo)
    def _():
        o_ref[...] = _mm(h_ref[...], whi_ref[:, pl.ds(pl.multiple_of((j - n_lo) * tn, tn), tn)])


def _inproj(x2d, p, l):
    m = x2d.shape[0]
    tm, tn = min(TM, m), TN_INPROJ
    assert AB_OFF % tn == 0 and (D_MAIN - AB_OFF) % tn == 0
    resident = dict(pipeline_mode=pl.Buffered(1))
    return pl.pallas_call(
        functools.partial(_inproj_kernel, tn=tn),
        grid=(m // tm, D_MAIN // tn),
        in_specs=[
            pl.BlockSpec((tm, D_MODEL), lambda i, j: (i, 0)),
            _layer_spec((1, D_MODEL), l, 2),
            pl.BlockSpec((None, D_MODEL, AB_OFF), lambda i, j: (l, 0, 0), **resident),
            pl.BlockSpec((None, D_MODEL, D_MAIN - AB_OFF), lambda i, j: (l, 0, 0), **resident),
            _layer_spec((D_MODEL, LANES), l, 2),
        ],
        out_specs=[
            pl.BlockSpec((tm, tn), lambda i, j: (i, j)),
            pl.BlockSpec((tm, LANES), lambda i, j: (i, 0)),
        ],
        out_shape=[
            jax.ShapeDtypeStruct((m, D_MAIN), F32),
            jax.ShapeDtypeStruct((m, LANES), F32),
        ],
        scratch_shapes=[pltpu.VMEM((tm, D_MODEL), BF16)],
        compiler_params=_params(("arbitrary", "arbitrary")),
        name="inproj",
    )(x2d, p["norm_pre_mix"], p["w_lo"], p["w_hi"], p["w_ab"])


def _conv_silu_norm(y, part):
    y = y * jax.nn.sigmoid(y)
    if part < 2:
        y = y * lax.rsqrt(jnp.sum(jnp.square(y), axis=-1, keepdims=True) + 1e-6)
    if part == 0:
        y = y * (HEAD_D ** -0.5)
    return y


def _delta_dec_kernel(qkv_ref, z_ref, ab_ref, cbuf_ref, s0_ref, cw_ref, alog_ref, dtb_ref, nw_ref,
                      *rest, nb, c, aliased):
    if aliased:
        rest = rest[1:]
    o_ref, nbuf_ref, snew_ref, xbuf_ref, qkvc_ref = rest
    pad = SUBLANES
    hist = CONV_W - 1
    xbuf_ref[:, pad - hist:pad, :] = cbuf_ref[...]
    xbuf_ref[:, pad:pad + c, :] = qkv_ref[...]
    for part in range(3):
        for h in range(N_HEADS):
            c0 = part * D_QK + h * HEAD_D
            cols = slice(c0, c0 + HEAD_D)
            y = cw_ref[0:1, cols] * xbuf_ref[:, pad - 3:pad - 3 + c, cols]
            for i in range(1, CONV_W):
                y = y + cw_ref[i:i + 1, cols] * xbuf_ref[:, pad - 3 + i:pad - 3 + i + c, cols]
            qkvc_ref[:, :, cols] = _conv_silu_norm(y, part)
    nbuf_ref[...] = xbuf_ref[:, pad + c - hist:pad + c, :]

    ri = lax.broadcasted_iota(jnp.int32, (c, c), 0)
    ci = lax.broadcasted_iota(jnp.int32, (c, c), 1)
    causal = ri >= ci
    strict = ri > ci
    diag = ri == ci
    tri = jnp.where(causal, 1.0, 0.0).astype(F32)
    eye = jnp.where(diag, 1.0, 0.0).astype(F32)
    n_levels = c.bit_length() - 2
    items = [(i, h) for i in range(nb) for h in range(N_HEADS)]
    idx = {it: n for n, it in enumerate(items)}
    every = range(len(items))

    gc, beta = [], []
    for i in range(nb):
        g = -jnp.exp(alog_ref[...]) * jax.nn.softplus(ab_ref[i] + dtb_ref[...])
        gc.append(_mm_hi(tri, g))
        beta.append(jax.nn.sigmoid(ab_ref[i]))

    def col(part, i, h):
        return qkvc_ref[i, :, part * D_QK + h * HEAD_D:part * D_QK + (h + 1) * HEAD_D]

    gcol = [gc[i][:, h:h + 1] for i, h in items]
    bcol = [beta[i][:, N_HEADS + h:N_HEADS + h + 1] for i, h in items]
    kb = [col(1, i, h) * bcol[idx[i, h]] for i, h in items]
    kq = [_mm(jnp.concatenate([kb[idx[i, h]], col(0, i, h)], axis=0), col(1, i, h), _NT)
          for i, h in items]
    x, qk, tk = [], [], []
    for n in every:
        grow = jnp.sum(jnp.where(diag, gcol[n], 0.0), axis=0, keepdims=True)
        dec = jnp.where(causal, jnp.exp(jnp.where(causal, gcol[n] - grow, 0.0)), 0.0)
        x.append(jnp.where(strict, -(kq[n][:c] * dec), 0.0))
        qk.append(kq[n][c:] * dec)
        tk.append(eye + x[n])
    pk = [_mm(x[n], x[n]) for n in every]
    for lvl in range(1, n_levels + 1):
        if lvl < n_levels:
            r = [_mm(jnp.concatenate([pk[n], tk[n]], axis=0), pk[n]) for n in every]
            pk = [r[n][:c] for n in every]
            tk = [tk[n] + r[n][c:] for n in every]
        else:
            tk = [tk[n] + _mm(tk[n], pk[n]) for n in every]
    e = [jnp.exp(gcol[n]) for n in every]
    sol = [_mm(tk[idx[i, h]], jnp.concatenate([col(2, i, h) * bcol[idx[i, h]],
                                                kb[idx[i, h]] * e[idx[i, h]]], axis=1))
           for i, h in items]
    wq = [_mm(jnp.concatenate([sol[idx[i, h]][:, HEAD_D:], col(0, i, h) * e[idx[i, h]]], axis=0),
              s0_ref[i, h]) for i, h in items]
    vn = [sol[n][:, :HEAD_D] - wq[n][:c] for n in every]
    op = [_mm(qk[n], vn[n]) for n in every]
    for i, h in items:
        n = idx[i, h]
        glast = gc[i][c - 1:c, h:h + 1]
        kdec = col(1, i, h) * jnp.exp(glast - gcol[n])
        snew_ref[i, h] = s0_ref[i, h] * jnp.exp(glast) + _mm(kdec, vn[n], _TN)
    for i, h in items:
        n = idx[i, h]
        o = _rms(wq[n][c:] + op[n], nw_ref[...])
        cols = slice(h * HEAD_D, (h + 1) * HEAD_D)
        zz = z_ref[i, :, cols]
        o_ref[i, :, cols] = o * (zz * jax.nn.sigmoid(zz))


def _delta_dec(proj3, ab3, conv_all, s_all, p, l, s_out_prev):
    n, c, _ = proj3.shape
    depth = s_all.shape[0]
    nb = DEC_SEQS_PER_STEP
    aliased = s_out_prev is not None
    kern = functools.partial(_delta_dec_kernel, nb=nb, c=c, aliased=aliased)
    state_blk = (None, nb, N_HEADS, HEAD_D, HEAD_D)
    in_specs = [
        pl.BlockSpec((nb, c, D_CONV), lambda i: (i, 0, 0)),
        pl.BlockSpec((nb, c, D_QK), lambda i: (i, 0, D_CONV // D_QK)),
        pl.BlockSpec((nb, c, LANES), lambda i: (i, 0, 0)),
        pl.BlockSpec((None, nb, CONV_W - 1, D_CONV), lambda i: (l, i, 0, 0)),
        pl.BlockSpec(state_blk, lambda i: (l, i, 0, 0, 0)),
        _layer_spec((CONV_W, D_CONV), l, 1),
        _layer_spec((1, LANES), l, 1),
        _layer_spec((1, LANES), l, 1),
        _layer_spec((1, HEAD_D), l, 1),
    ]
    args = [proj3, proj3, ab3, conv_all, s_all, p["conv_w"], p["a_log"], p["dt_bias"], p["delta_norm_w"]]
    aliases = {}
    if aliased:
        in_specs.append(pl.BlockSpec(memory_space=pl.ANY))
        args.append(s_out_prev)
        aliases = {len(args) - 1: 2}
    return pl.pallas_call(
        kern,
        grid=(n // nb,),
        in_specs=in_specs,
        out_specs=[
            pl.BlockSpec((nb, c, D_QK), lambda i: (i, 0, 0)),
            pl.BlockSpec((nb, CONV_W - 1, D_CONV), lambda i: (i, 0, 0)),
            pl.BlockSpec(state_blk, lambda i: (l, i, 0, 0, 0)),
        ],
        out_shape=[
            jax.ShapeDtypeStruct((n, c, D_QK), F32),
            jax.ShapeDtypeStruct((n, CONV_W - 1, D_CONV), F32),
            jax.ShapeDtypeStruct((depth, n, N_HEADS, HEAD_D, HEAD_D), F32),
        ],
        scratch_shapes=[
            pltpu.VMEM((nb, c + SUBLANES, D_CONV), F32),
            pltpu.VMEM((nb, c, D_CONV), F32),
        ],
        input_output_aliases=aliases,
        compiler_params=_params(("arbitrary",)),
        name="delta_dec",
    )(*args)


def _chunk_pairs(refs, c):
    qkvc_ref, gc_ref, beta_ref, s_ref, z_ref, nw_ref, o_ref = refs
    assert 2 * c == LANES
    ns = qkvc_ref.shape[0]
    ri = lax.broadcasted_iota(jnp.int32, (c, LANES), 0)
    lane = lax.broadcasted_iota(jnp.int32, (c, LANES), 1)
    half = lane >= c
    cj = jnp.where(half, lane - c, lane)
    causal = ri >= cj
    strict = ri > cj
    diag = ri == cj
    eye = jnp.where(diag, 1.0, 0.0).astype(F32)
    zeros = jnp.zeros((c, HEAD_D), F32)
    zeros2 = jnp.zeros((c, 2 * HEAD_D), F32)
    n_levels = c.bit_length() - 2

    def blockdiag(m):
        return jnp.concatenate([jnp.where(half, 0.0, m), jnp.where(half, m, 0.0)], axis=0)

    def side_by_side(a, b):
        return jnp.concatenate([jnp.concatenate([a, zeros], axis=1),
                                jnp.concatenate([zeros, b], axis=1)], axis=0)

    def chunk_body(ic, carry):
        r0 = pl.multiple_of(ic * c, c)
        rows = pl.ds(r0, c)
        heads = [(s, h) for s in range(ns) for h in range(N_HEADS)]
        pairs = [(s, pr) for s in range(ns) for pr in range(N_HEADS // 2)]
        gc = [gc_ref[s, rows, :] for s in range(ns)]
        beta = [beta_ref[s, rows, :] for s in range(ns)]

        def col(part, s, h):
            return qkvc_ref[s, rows, part * D_QK + h * HEAD_D:part * D_QK + (h + 1) * HEAD_D]

        g = {(s, h): jnp.broadcast_to(gc[s][:, h:h + 1], (c, HEAD_D)) for s, h in heads}
        b = {(s, h): jnp.broadcast_to(beta[s][:, N_HEADS + h:N_HEADS + h + 1], (c, HEAD_D))
             for s, h in heads}
        kb = {(s, h): col(1, s, h) * b[s, h] for s, h in heads}
        kq = {}
        for s, pr in pairs:
            h1, h2 = 2 * pr, 2 * pr + 1
            lhs = jnp.concatenate([jnp.concatenate([kb[s, h1], kb[s, h2]], axis=1),
                                   jnp.concatenate([col(0, s, h1), col(0, s, h2)], axis=1)], axis=0)
            kq[s, pr] = _mm(lhs, side_by_side(col(1, s, h1), col(1, s, h2)), _NT)
        x, qkd, tk = {}, {}, {}
        for s, pr in pairs:
            gcp = jnp.where(half, g[s, 2 * pr + 1], g[s, 2 * pr])
            rowp = jnp.sum(jnp.where(diag, gcp, 0.0), axis=0, keepdims=True)
            dec = jnp.where(causal, jnp.exp(jnp.where(causal, gcp - rowp, 0.0)), 0.0)
            x[s, pr] = jnp.where(strict, -(kq[s, pr][:c] * dec), 0.0)
            qkd[s, pr] = kq[s, pr][c:] * dec
            tk[s, pr] = eye + x[s, pr]
        pk = {sp: _mm(x[sp], blockdiag(x[sp])) for sp in pairs}
        for lvl in range(1, n_levels + 1):
            if lvl < n_levels:
                r = {sp: _mm(jnp.concatenate([pk[sp], tk[sp]], axis=0), blockdiag(pk[sp])) for sp in pairs}
                pk = {sp: r[sp][:c] for sp in pairs}
                tk = {sp: tk[sp] + r[sp][c:] for sp in pairs}
            else:
                tk = {sp: tk[sp] + _mm(tk[sp], blockdiag(pk[sp])) for sp in pairs}
        e = {sh: jnp.exp(g[sh]) for sh in heads}
        sol = {}
        for s, h in heads:
            rhs = jnp.concatenate([col(2, s, h) * b[s, h], kb[s, h] * e[s, h]], axis=1)
            rhs = jnp.concatenate([rhs, zeros2] if h % 2 == 0 else [zeros2, rhs], axis=0)
            sol[s, h] = _mm(tk[s, h // 2], rhs)
        wq = {(s, h): _mm(jnp.concatenate([sol[s, h][:, HEAD_D:], col(0, s, h) * e[s, h]], axis=0),
                          s_ref[s, h]) for s, h in heads}
        vn = {sh: sol[sh][:, :HEAD_D] - wq[sh][:c] for sh in heads}
        op = {(s, pr): _mm(qkd[s, pr], side_by_side(vn[s, 2 * pr], vn[s, 2 * pr + 1]))
              for s, pr in pairs}
        for s, h in heads:
            glast = gc[s][c - 1:c, h:h + 1]
            kdec = col(1, s, h) * jnp.exp(glast - g[s, h])
            s_ref[s, h] = s_ref[s, h] * jnp.exp(glast) + _mm(kdec, vn[s, h], _TN)
        for s, h in heads:
            o = _rms(wq[s, h][c:] + op[s, h // 2][:, (h % 2) * HEAD_D:(h % 2 + 1) * HEAD_D], nw_ref[...])
            cols = slice(h * HEAD_D, (h + 1) * HEAD_D)
            zz = z_ref[s, rows, cols]
            o_ref[s, rows, cols] = (o * (zz * jax.nn.sigmoid(zz))).astype(o_ref.dtype)
        return carry

    return chunk_body


def _delta_kernel(qkv_ref, z_ref, ab_ref, cbuf_ref, s0_ref, cw_ref, alog_ref, dtb_ref, nw_ref,
                  o_ref, nbuf_ref, snew_ref,
                  s_ref, xbuf_ref, qkvc_ref, gc_ref, beta_ref, *, tt, c):
    t = pl.program_id(1)
    nt = pl.num_programs(1)
    ns = qkv_ref.shape[0]
    pad = SUBLANES
    hist = CONV_W - 1

    @pl.when(t == 0)
    def _():
        s_ref[...] = s0_ref[...]
        xbuf_ref[:, pad - hist:pad, :] = cbuf_ref[...]

    xbuf_ref[:, pad:pad + tt, :] = qkv_ref[...]

    nv = tt // SUBLANES
    sub = lax.broadcasted_iota(jnp.int32, (nv, SUBLANES, HEAD_D), 1)
    for s in range(ns):
        for part in range(3):
            for h in range(N_HEADS):
                c0 = part * D_QK + h * HEAD_D
                cols = slice(c0, c0 + HEAD_D)
                xg = xbuf_ref[s, :, cols].reshape(nv + 1, SUBLANES, HEAD_D)
                y = None
                for i in range(CONV_W):
                    k = CONV_W - 1 - i
                    if k == 0:
                        tap = xg[1:]
                    else:
                        rot = pltpu.roll(xg, k, axis=1)
                        tap = jnp.where(sub >= k, rot[1:], rot[:-1])
                    term = cw_ref[i:i + 1, cols] * tap
                    y = term if y is None else y + term
                qkvc_ref[s, :, cols] = _conv_silu_norm(y, part).reshape(tt, HEAD_D)

    tail = xbuf_ref[:, pad + tt - hist:pad + tt, :]

    @pl.when(t == nt - 1)
    def _():
        nbuf_ref[...] = tail

    xbuf_ref[:, pad - hist:pad, :] = tail

    rt = lax.broadcasted_iota(jnp.int32, (c, c), 0)
    ct = lax.broadcasted_iota(jnp.int32, (c, c), 1)
    tri = jnp.where(rt >= ct, 1.0, 0.0).astype(F32)
    for s in range(ns):
        ab = ab_ref[s]
        g = -jnp.exp(alog_ref[...]) * jax.nn.softplus(ab + dtb_ref[...])
        beta_ref[s] = jax.nn.sigmoid(ab)
        for ic in range(tt // c):
            gc_ref[s, ic * c:(ic + 1) * c, :] = _mm_hi(tri, g[ic * c:(ic + 1) * c, :])

    refs = (qkvc_ref, gc_ref, beta_ref, s_ref, z_ref, nw_ref, o_ref)
    lax.fori_loop(0, tt // c, _chunk_pairs(refs, c), 0)

    @pl.when(t == nt - 1)
    def _():
        snew_ref[...] = s_ref[...]


def _delta(proj3, ab3, conv_buf, s0, p, l):
    n, t, _ = proj3.shape
    tt, c = min(TT_DELTA, t), DELTA_CHUNK
    ns = min(SEQS_DELTA, n)
    kern = functools.partial(_delta_kernel, tt=tt, c=c)
    return pl.pallas_call(
        kern,
        grid=(n // ns, t // tt),
        in_specs=[
            pl.BlockSpec((ns, tt, D_CONV), lambda i, j: (i, j, 0)),
            pl.BlockSpec((ns, tt, D_QK), lambda i, j: (i, j, D_CONV // D_QK)),
            pl.BlockSpec((ns, tt, LANES), lambda i, j: (i, j, 0)),
            pl.BlockSpec((ns, CONV_W - 1, D_CONV), lambda i, j: (i, 0, 0)),
            pl.BlockSpec((ns, N_HEADS, HEAD_D, HEAD_D), lambda i, j: (i, 0, 0, 0)),
            _layer_spec((CONV_W, D_CONV), l, 2),
            _layer_spec((1, LANES), l, 2),
            _layer_spec((1, LANES), l, 2),
            _layer_spec((1, HEAD_D), l, 2),
        ],
        out_specs=[
            pl.BlockSpec((ns, tt, D_QK), lambda i, j: (i, j, 0)),
            pl.BlockSpec((ns, CONV_W - 1, D_CONV), lambda i, j: (i, 0, 0)),
            pl.BlockSpec((ns, N_HEADS, HEAD_D, HEAD_D), lambda i, j: (i, 0, 0, 0)),
        ],
        out_shape=[
            jax.ShapeDtypeStruct((n, t, D_QK), BF16),
            jax.ShapeDtypeStruct((n, CONV_W - 1, D_CONV), F32),
            jax.ShapeDtypeStruct((n, N_HEADS, HEAD_D, HEAD_D), F32),
        ],
        scratch_shapes=[
            pltpu.VMEM((ns, N_HEADS, HEAD_D, HEAD_D), F32),
            pltpu.VMEM((ns, tt + SUBLANES, D_CONV), F32),
            pltpu.VMEM((ns, tt, D_CONV), F32),
            pltpu.VMEM((ns, tt, LANES), F32),
            pltpu.VMEM((ns, tt, LANES), F32),
        ],
        compiler_params=_params(("arbitrary", "arbitrary")),
        name="delta",
    )(proj3, proj3, ab3, conv_buf, s0, p["conv_w"], p["a_log"], p["dt_bias"], p["delta_norm_w"])


def _cmlp_kernel(uv_ref, lnw_ref, lnb_ref, ws_ref, bs_ref, o_ref, v_ref):
    t = pl.program_id(1)
    nt = pl.num_programs(1)
    c = MLP_CHUNK
    x = uv_ref[0]
    gel = 0.5 * x * (1.0 + lax.erf(x * (2.0 ** -0.5)))
    u = gel[:, :D_B]
    v = gel[:, D_B:]
    vc = v - jnp.mean(v, axis=-1, keepdims=True)
    var = jnp.mean(jnp.square(vc), axis=-1, keepdims=True)
    v = vc * lax.rsqrt(var + 1e-5) * lnw_ref[...] + lnb_ref[...]

    @pl.when(t == nt - 1)
    def _():
        v_ref[0] = v

    ri = lax.broadcasted_iota(jnp.int32, (c, c), 0)
    ci = lax.broadcasted_iota(jnp.int32, (c, c), 1)
    gd = D_B // N_GROUPS
    for g in range(N_GROUPS):
        cols = slice(g * gd, (g + 1) * gd)
        ws = jnp.where(ri >= ci, ws_ref[g], 0.0)
        mixed = _mm(ws, v[:, cols]) + bs_ref[:, g:g + 1]
        o_ref[0, :, cols] = (u[:, cols] * mixed).astype(o_ref.dtype)


def _cmlp(proj3, p, l, ws_key, bs_key):
    n, t, _ = proj3.shape
    c = MLP_CHUNK
    return pl.pallas_call(
        _cmlp_kernel,
        grid=(n, t // c),
        in_specs=[
            pl.BlockSpec((1, c, 2 * D_B), lambda i, j: (i, j, (D_CONV + D_QK) // (2 * D_B))),
            _layer_spec((1, D_B), l, 2),
            _layer_spec((1, D_B), l, 2),
            _layer_spec((N_GROUPS, c, c), l, 2),
            _layer_spec((c, LANES), l, 2),
        ],
        out_specs=[
            pl.BlockSpec((1, c, D_B), lambda i, j: (i, j, 0)),
            pl.BlockSpec((1, c, D_B), lambda i, j: (i, 0, 0)),
        ],
        out_shape=[
            jax.ShapeDtypeStruct((n, t, D_B), BF16),
            jax.ShapeDtypeStruct((n, c, D_B), F32),
        ],
        compiler_params=_params(("arbitrary", "arbitrary")),
        name="cmlp",
    )(proj3, p["sgu_ln_w"], p["sgu_ln_b"], p[ws_key], p[bs_key])


def _merge_kernel(oa_ref, ob_ref, ga_ref, gb_ref, x_ref, wa_ref, wb_ref, wo_ref, nw_ref, y_ref):
    pa = _mm(oa_ref[...], wa_ref[...])
    pb = _mm(ob_ref[...], wb_ref[...])
    merged = jax.nn.sigmoid(ga_ref[...]) * pa + jax.nn.sigmoid(gb_ref[...]) * pb
    y = _mm(merged, wo_ref[...])
    y_ref[...] = x_ref[...] + _rms(y, nw_ref[...])


def _merge(oa, ob, proj, x2d, p, l):
    m = x2d.shape[0]
    tm = min(TM, m)
    row = lambda i: (i, 0)
    g0 = (D_CONV + D_QK + 2 * D_B) // D_MODEL
    return pl.pallas_call(
        _merge_kernel,
        grid=(m // tm,),
        in_specs=[
            pl.BlockSpec((tm, D_QK), row),
            pl.BlockSpec((tm, D_B), row),
            pl.BlockSpec((tm, D_MODEL), lambda i: (i, g0)),
            pl.BlockSpec((tm, D_MODEL), lambda i: (i, g0 + 1)),
            pl.BlockSpec((tm, D_MODEL), row),
            _layer_spec((D_QK, D_MODEL), l, 1),
            _layer_spec((D_B, D_MODEL), l, 1),
            _layer_spec((D_MODEL, D_MODEL), l, 1),
            _layer_spec((1, D_MODEL), l, 1),
        ],
        out_specs=pl.BlockSpec((tm, D_MODEL), row),
        out_shape=jax.ShapeDtypeStruct((m, D_MODEL), F32),
        compiler_params=_params(("arbitrary",)),
        name="merge",
    )(oa, ob, proj, proj, x2d, p["w_proj_a"], p["w_proj_b"], p["w_out"], p["norm_post_mix"])


def _ffn_kernel(x_ref, npre_ref, wi_ref, wd_ref, npost_ref, y_ref, act_ref):
    x = x_ref[...]
    h = _rms(x, npre_ref[...]).astype(BF16)
    tf = TF_FFN
    for f in range(D_FF // tf):
        gate = _mm(h, wi_ref[:, f * tf:(f + 1) * tf])
        up = _mm(h, wi_ref[:, D_FF + f * tf:D_FF + (f + 1) * tf])
        act_ref[:, f * tf:(f + 1) * tf] = (gate * jax.nn.sigmoid(gate) * up).astype(BF16)
    y = _mm(act_ref[...], wd_ref[...])
    y_ref[...] = x + _rms(y, npost_ref[...])


def _ffn(x2d, p, l):
    m = x2d.shape[0]
    tm = min(TM, m)
    resident = dict(pipeline_mode=pl.Buffered(1))
    return pl.pallas_call(
        _ffn_kernel,
        grid=(m // tm,),
        in_specs=[
            pl.BlockSpec((tm, D_MODEL), lambda i: (i, 0)),
            _layer_spec((1, D_MODEL), l, 1),
            pl.BlockSpec((None, D_MODEL, 2 * D_FF), lambda i: (l, 0, 0), **resident),
            pl.BlockSpec((None, D_FF, D_MODEL), lambda i: (l, 0, 0), **resident),
            _layer_spec((1, D_MODEL), l, 1),
        ],
        out_specs=pl.BlockSpec((tm, D_MODEL), lambda i: (i, 0)),
        out_shape=jax.ShapeDtypeStruct((m, D_MODEL), F32),
        scratch_shapes=[pltpu.VMEM((tm, D_FF), BF16)],
        compiler_params=_params(("arbitrary",)),
        name="ffn",
    )(x2d, p["norm_pre_ffn"], p["w_ffn_in"], p["w_ffn_out"], p["norm_post_ffn"])


def _trunk_layer(x, p, l, delta_fn):
    n, t, _ = x.shape
    m = n * t
    x2d = x.reshape(m, D_MODEL)
    proj, ab = _inproj(x2d, p, l)
    o_a, new_buf, s_new = delta_fn(proj.reshape(n, t, D_MAIN), ab.reshape(n, t, LANES))
    if t % MLP_CHUNK == 0:
        o_b, v_rows = _cmlp(proj.reshape(n, t, D_MAIN), p, l, "w_spatial", "bs_t")
    else:
        assert MLP_CHUNK % t == 0 and m % MLP_CHUNK == 0
        o_b, v_rows = _cmlp(proj.reshape(m // MLP_CHUNK, MLP_CHUNK, D_MAIN), p, l, "ws_short", "bs_short")
        v_rows = v_rows.reshape(n, t, D_B)
    x1 = _merge(o_a.reshape(m, D_QK), o_b.reshape(m, D_B), proj, x2d, p, l)
    x2 = _ffn(x1, p, l)
    return x2.reshape(n, t, D_MODEL), s_new, new_buf, v_rows


def _prepare_params(t_short, norm_pre_mix, w_in, conv_w, a_log, dt_bias, delta_norm_w, sgu_ln_w,
                    sgu_ln_b, w_spatial, b_spatial, w_proj_a, w_proj_b, w_out, norm_post_mix,
                    norm_pre_ffn, w_ffn_in, w_ffn_out, norm_post_ffn):
    depth = w_in.shape[0]
    w_lo = w_in[:, :, :AB_OFF].astype(BF16)
    w_hi = w_in[:, :, AB_OFF + 2 * N_HEADS:].astype(BF16)
    w_ab = jnp.pad(w_in[:, :, AB_OFF:AB_OFF + 2 * N_HEADS], ((0, 0), (0, 0), (0, LANES - 2 * N_HEADS)))
    row = lambda v: v.reshape(depth, 1, -1)
    lanes = lambda v: jnp.pad(v, ((0, 0), (0, LANES - v.shape[1]))).reshape(depth, 1, LANES)
    bs_t = jnp.pad(jnp.swapaxes(b_spatial, 1, 2), ((0, 0), (0, 0), (0, LANES - N_GROUPS)))
    rep = MLP_CHUNK // t_short
    eye = jnp.eye(rep, dtype=F32)
    ws_short = jnp.einsum("ab,lgij->lgaibj", eye, w_spatial[:, :, :t_short, :t_short]).reshape(
        depth, N_GROUPS, MLP_CHUNK, MLP_CHUNK)
    return dict(
        norm_pre_mix=row(norm_pre_mix), w_lo=w_lo, w_hi=w_hi, w_ab=w_ab.astype(BF16),
        conv_w=conv_w, a_log=lanes(a_log), dt_bias=lanes(dt_bias), delta_norm_w=row(delta_norm_w),
        sgu_ln_w=row(sgu_ln_w), sgu_ln_b=row(sgu_ln_b), w_spatial=w_spatial, bs_t=bs_t,
        ws_short=ws_short, bs_short=jnp.tile(bs_t[:, :t_short], (1, rep, 1)),
        w_proj_a=w_proj_a.astype(BF16), w_proj_b=w_proj_b.astype(BF16), w_out=w_out.astype(BF16),
        norm_post_mix=row(norm_post_mix), norm_pre_ffn=row(norm_pre_ffn),
        w_ffn_in=w_ffn_in.astype(BF16), w_ffn_out=w_ffn_out.astype(BF16),
        norm_post_ffn=row(norm_post_ffn))


def kernel(x_prompt, x_sample, state_delta, state_conv, norm_pre_mix, w_in, conv_w, a_log, dt_bias,
           delta_norm_w, sgu_ln_w, sgu_ln_b, w_spatial, b_spatial, w_proj_a, w_proj_b, w_out,
           norm_post_mix, norm_pre_ffn, w_ffn_in, w_ffn_out, norm_post_ffn):
    depth = w_in.shape[0]
    nb, seq, _ = x_prompt.shape
    ndec, dec_seq, _ = x_sample.shape
    assert seq % DELTA_CHUNK == 0 and seq % MLP_CHUNK == 0
    assert dec_seq % SUBLANES == 0 and dec_seq < DELTA_CHUNK and ndec % DEC_SEQS_PER_STEP == 0
    p = _prepare_params(dec_seq, norm_pre_mix, w_in, conv_w, a_log, dt_bias, delta_norm_w, sgu_ln_w,
                        sgu_ln_b, w_spatial, b_spatial, w_proj_a, w_proj_b, w_out, norm_post_mix,
                        norm_pre_ffn, w_ffn_in, w_ffn_out, norm_post_ffn)
    y_p, y_s = x_prompt, x_sample
    conv0 = jnp.zeros((nb, CONV_W - 1, D_CONV), x_prompt.dtype)
    s_zero = jnp.zeros((nb, N_HEADS, HEAD_D, HEAD_D), state_delta.dtype)
    sd_p, sc_p, cv_p, sc_s, cv_s = [], [], [], [], []
    sd_s = None
    for l in range(depth):
        y_p, s_new, buf_new, v_rows = _trunk_layer(
            y_p, p, l, lambda proj3, ab3: _delta(proj3, ab3, conv0, s_zero, p, l))
        sd_p.append(s_new)
        sc_p.append(buf_new)
        cv_p.append(v_rows)
        y_s, sd_s, buf_new, v_rows = _trunk_layer(
            y_s, p, l, lambda proj3, ab3: _delta_dec(proj3, ab3, state_conv, state_delta, p, l, sd_s))
        sc_s.append(buf_new)
        cv_s.append(v_rows)
    return (y_p, y_s, jnp.stack(sd_p), jnp.stack(sc_p), jnp.stack(cv_p),
            sd_s, jnp.stack(sc_s), jnp.stack(cv_s))
```

```python
import functools

import jax
import jax.numpy as jnp
from jax import lax
from jax.experimental import pallas as pl
from jax.experimental.pallas import tpu as pltpu

F32 = jnp.float32
BF16 = jnp.bfloat16

D_MODEL = 1024
N_HEADS = 8
HEAD_D = 128
D_QK = N_HEADS * HEAD_D
D_CONV = 3 * D_QK
CONV_W = 4
DELTA_CHUNK = 64
MLP_CHUNK = 128
N_GROUPS = 8
D_B = 1024
D_FF = 2816
D_MAIN = D_CONV + D_QK + 2 * D_B + 2 * D_MODEL
AB_OFF = D_CONV + D_QK
LANES = 128
SUBLANES = 8

TM = 512
TN_INPROJ = 2048
TF_FFN = 256
TT_DELTA = 128
SEQS_DELTA = 4
DEC_SEQS_PER_STEP = 8
VMEM_LIMIT = 48 * 1024 * 1024

_HI = lax.Precision.HIGHEST
_NT = (((1,), (1,)), ((), ()))
_TN = (((0,), (0,)), ((), ()))
_NN = (((1,), (0,)), ((), ()))


def _mm(a, b, dims=_NN):
    return lax.dot_general(a.astype(BF16), b.astype(BF16), dims, preferred_element_type=F32)


def _mm_hi(a, b, dims=_NN):
    return lax.dot_general(a, b, dims, precision=_HI, preferred_element_type=F32)


def _rms(x, w, eps=1e-6):
    return x * lax.rsqrt(jnp.mean(jnp.square(x), axis=-1, keepdims=True) + eps) * w


def _params(sem):
    return pltpu.CompilerParams(dimension_semantics=sem, vmem_limit_bytes=VMEM_LIMIT)


def _layer_spec(shape, l, ngrid):
    zeros = (0,) * len(shape)
    if ngrid == 1:
        return pl.BlockSpec((None,) + shape, lambda i: (l,) + zeros)
    return pl.BlockSpec((None,) + shape, lambda i, j: (l,) + zeros)


def _inproj_kernel(x_ref, nw_ref, wlo_ref, whi_ref, wab_ref, o_ref, ab_ref, h_ref, *, tn):
    j = pl.program_id(1)
    n_lo = AB_OFF // tn

    @pl.when(j == 0)
    def _():
        h = _rms(x_ref[...], nw_ref[...]).astype(BF16)
        h_ref[...] = h
        ab_ref[...] = _mm(h, wab_ref[...])

    @pl.when(j < n_lo)
    def _():
        o_ref[...] = _mm(h_ref[...], wlo_ref[:, pl.ds(pl.multiple_of(j * tn, tn), tn)])

    @pl.when(j >= n_lo)
    def _():
        o_ref[...] = _mm(h_ref[...], whi_ref[:, pl.ds(pl.multiple_of((j - n_lo) * tn, tn), tn)])


def _inproj(x2d, p, l):
    m = x2d.shape[0]
    tm, tn = min(TM, m), TN_INPROJ
    assert AB_OFF % tn == 0 and (D_MAIN - AB_OFF) % tn == 0
    resident = dict(pipeline_mode=pl.Buffered(1))
    return pl.pallas_call(
        functools.partial(_inproj_kernel, tn=tn),
        grid=(m // tm, D_MAIN // tn),
        in_specs=[
            pl.BlockSpec((tm, D_MODEL), lambda i, j: (i, 0)),
            _layer_spec((1, D_MODEL), l, 2),
            pl.BlockSpec((None, D_MODEL, AB_OFF), lambda i, j: (l, 0, 0), **resident),
            pl.BlockSpec((None, D_MODEL, D_MAIN - AB_OFF), lambda i, j: (l, 0, 0), **resident),
            _layer_spec((D_MODEL, LANES), l, 2),
        ],
        out_specs=[
            pl.BlockSpec((tm, tn), lambda i, j: (i, j)),
            pl.BlockSpec((tm, LANES), lambda i, j: (i, 0)),
        ],
        out_shape=[
            jax.ShapeDtypeStruct((m, D_MAIN), F32),
            jax.ShapeDtypeStruct((m, LANES), F32),
        ],
        scratch_shapes=[pltpu.VMEM((tm, D_MODEL), BF16)],
        compiler_params=_params(("arbitrary", "arbitrary")),
        name="inproj",
    )(x2d, p["norm_pre_mix"], p["w_lo"], p["w_hi"], p["w_ab"])


def _conv_silu_norm(y, part):
    y = y * jax.nn.sigmoid(y)
    if part < 2:
        y = y * lax.rsqrt(jnp.sum(jnp.square(y), axis=-1, keepdims=True) + 1e-6)
    if part == 0:
        y = y * (HEAD_D ** -0.5)
    return y


def _delta_dec_kernel(qkv_ref, z_ref, ab_ref, cbuf_ref, s0_ref, cw_ref, alog_ref, dtb_ref, nw_ref,
                      *rest, nb, c, aliased):
    if aliased:
        rest = rest[1:]
    o_ref, nbuf_ref, snew_ref, xbuf_ref, qkvc_ref = rest
    pad = SUBLANES
    hist = CONV_W - 1
    xbuf_ref[:, pad - hist:pad, :] = cbuf_ref[...]
    xbuf_ref[:, pad:pad + c, :] = qkv_ref[...]
    for part in range(3):
        for h in range(N_HEADS):
            c0 = part * D_QK + h * HEAD_D
            cols = slice(c0, c0 + HEAD_D)
            y = cw_ref[0:1, cols] * xbuf_ref[:, pad - 3:pad - 3 + c, cols]
            for i in range(1, CONV_W):
                y = y + cw_ref[i:i + 1, cols] * xbuf_ref[:, pad - 3 + i:pad - 3 + i + c, cols]
            qkvc_ref[:, :, cols] = _conv_silu_norm(y, part)
    nbuf_ref[...] = xbuf_ref[:, pad + c - hist:pad + c, :]

    ri = lax.broadcasted_iota(jnp.int32, (c, c), 0)
    ci = lax.broadcasted_iota(jnp.int32, (c, c), 1)
    causal = ri >= ci
    strict = ri > ci
    diag = ri == ci
    tri = jnp.where(causal, 1.0, 0.0).astype(F32)
    eye = jnp.where(diag, 1.0, 0.0).astype(F32)
    n_levels = c.bit_length() - 2
    items = [(i, h) for i in range(nb) for h in range(N_HEADS)]
    idx = {it: n for n, it in enumerate(items)}
    every = range(len(items))

    gc, beta = [], []
    for i in range(nb):
        g = -jnp.exp(alog_ref[...]) * jax.nn.softplus(ab_ref[i] + dtb_ref[...])
        gc.append(_mm_hi(tri, g))
        beta.append(jax.nn.sigmoid(ab_ref[i]))

    def col(part, i, h):
        return qkvc_ref[i, :, part * D_QK + h * HEAD_D:part * D_QK + (h + 1) * HEAD_D]

    gcol = [gc[i][:, h:h + 1] for i, h in items]
    bcol = [beta[i][:, N_HEADS + h:N_HEADS + h + 1] for i, h in items]
    kb = [col(1, i, h) * bcol[idx[i, h]] for i, h in items]
    kq = [_mm(jnp.concatenate([kb[idx[i, h]], col(0, i, h)], axis=0), col(1, i, h), _NT)
          for i, h in items]
    x, qk, tk = [], [], []
    for n in every:
        grow = jnp.sum(jnp.where(diag, gcol[n], 0.0), axis=0, keepdims=True)
        dec = jnp.where(causal, jnp.exp(jnp.where(causal, gcol[n] - grow, 0.0)), 0.0)
        x.append(jnp.where(strict, -(kq[n][:c] * dec), 0.0))
        qk.append(kq[n][c:] * dec)
        tk.append(eye + x[n])
    pk = [_mm(x[n], x[n]) for n in every]
    for lvl in range(1, n_levels + 1):
        if lvl < n_levels:
            r = [_mm(jnp.concatenate([pk[n], tk[n]], axis=0), pk[n]) for n in every]
            pk = [r[n][:c] for n in every]
            tk = [tk[n] + r[n][c:] for n in every]
        else:
            tk = [tk[n] + _mm(tk[n], pk[n]) for n in every]
    e = [jnp.exp(gcol[n]) for n in every]
    sol = [_mm(tk[idx[i, h]], jnp.concatenate([col(2, i, h) * bcol[idx[i, h]],
                                                kb[idx[i, h]] * e[idx[i, h]]], axis=1))
           for i, h in items]
    wq = [_mm(jnp.concatenate([sol[idx[i, h]][:, HEAD_D:], col(0, i, h) * e[idx[i, h]]], axis=0),
              s0_ref[i, h]) for i, h in items]
    vn = [sol[n][:, :HEAD_D] - wq[n][:c] for n in every]
    op = [_mm(qk[n], vn[n]) for n in every]
    for i, h in items:
        n = idx[i, h]
        glast = gc[i][c - 1:c, h:h + 1]
        kdec = col(1, i, h) * jnp.exp(glast - gcol[n])
        snew_ref[i, h] = s0_ref[i, h] * jnp.exp(glast) + _mm(kdec, vn[n], _TN)
    for i, h in items:
        n = idx[i, h]
        o = _rms(wq[n][c:] + op[n], nw_ref[...])
        cols = slice(h * HEAD_D, (h + 1) * HEAD_D)
        zz = z_ref[i, :, cols]
        o_ref[i, :, cols] = o * (zz * jax.nn.sigmoid(zz))


def _delta_dec(proj3, ab3, conv_all, s_all, p, l, s_out_prev):
    n, c, _ = proj3.shape
    depth = s_all.shape[0]
    nb = DEC_SEQS_PER_STEP
    aliased = s_out_prev is not None
    kern = functools.partial(_delta_dec_kernel, nb=nb, c=c, aliased=aliased)
    state_blk = (None, nb, N_HEADS, HEAD_D, HEAD_D)
    in_specs = [
        pl.BlockSpec((nb, c, D_CONV), lambda i: (i, 0, 0)),
        pl.BlockSpec((nb, c, D_QK), lambda i: (i, 0, D_CONV // D_QK)),
        pl.BlockSpec((nb, c, LANES), lambda i: (i, 0, 0)),
        pl.BlockSpec((None, nb, CONV_W - 1, D_CONV), lambda i: (l, i, 0, 0)),
        pl.BlockSpec(state_blk, lambda i: (l, i, 0, 0, 0)),
        _layer_spec((CONV_W, D_CONV), l, 1),
        _layer_spec((1, LANES), l, 1),
        _layer_spec((1, LANES), l, 1),
        _layer_spec((1, HEAD_D), l, 1),
    ]
    args = [proj3, proj3, ab3, conv_all, s_all, p["conv_w"], p["a_log"], p["dt_bias"], p["delta_norm_w"]]
    aliases = {}
    if aliased:
        in_specs.append(pl.BlockSpec(memory_space=pl.ANY))
        args.append(s_out_prev)
        aliases = {len(args) - 1: 2}
    return pl.pallas_call(
        kern,
        grid=(n // nb,),
        in_specs=in_specs,
        out_specs=[
            pl.BlockSpec((nb, c, D_QK), lambda i: (i, 0, 0)),
            pl.BlockSpec((nb, CONV_W - 1, D_CONV), lambda i: (i, 0, 0)),
            pl.BlockSpec(state_blk, lambda i: (l, i, 0, 0, 0)),
        ],
        out_shape=[
            jax.ShapeDtypeStruct((n, c, D_QK), F32),
            jax.ShapeDtypeStruct((n, CONV_W - 1, D_CONV), F32),
            jax.ShapeDtypeStruct((depth, n, N_HEADS, HEAD_D, HEAD_D), F32),
        ],
        scratch_shapes=[
            pltpu.VMEM((nb, c + SUBLANES, D_CONV), F32),
            pltpu.VMEM((nb, c, D_CONV), F32),
        ],
        input_output_aliases=aliases,
        compiler_params=_params(("arbitrary",)),
        name="delta_dec",
    )(*args)


def _chunk_pairs(refs, c):
    qkvc_ref, gc_ref, beta_ref, s_ref, z_ref, nw_ref, o_ref = refs
    assert 2 * c == LANES
    ns = qkvc_ref.shape[0]
    ri = lax.broadcasted_iota(jnp.int32, (c, LANES), 0)
    lane = lax.broadcasted_iota(jnp.int32, (c, LANES), 1)
    half = lane >= c
    cj = jnp.where(half, lane - c, lane)
    causal = ri >= cj
    strict = ri > cj
    diag = ri == cj
    eye = jnp.where(diag, 1.0, 0.0).astype(F32)
    zeros = jnp.zeros((c, HEAD_D), F32)
    zeros2 = jnp.zeros((c, 2 * HEAD_D), F32)
    n_levels = c.bit_length() - 2

    def blockdiag(m):
        return jnp.concatenate([jnp.where(half, 0.0, m), jnp.where(half, m, 0.0)], axis=0)

    def side_by_side(a, b):
        return jnp.concatenate([jnp.concatenate([a, zeros], axis=1),
                                jnp.concatenate([zeros, b], axis=1)], axis=0)

    def chunk_body(ic, carry):
        r0 = pl.multiple_of(ic * c, c)
        rows = pl.ds(r0, c)
        heads = [(s, h) for s in range(ns) for h in range(N_HEADS)]
        pairs = [(s, pr) for s in range(ns) for pr in range(N_HEADS // 2)]
        gc = [gc_ref[s, rows, :] for s in range(ns)]
        beta = [beta_ref[s, rows, :] for s in range(ns)]

        def col(part, s, h):
            return qkvc_ref[s, rows, part * D_QK + h * HEAD_D:part * D_QK + (h + 1) * HEAD_D]

        g = {(s, h): jnp.broadcast_to(gc[s][:, h:h + 1], (c, HEAD_D)) for s, h in heads}
        b = {(s, h): jnp.broadcast_to(beta[s][:, N_HEADS + h:N_HEADS + h + 1], (c, HEAD_D))
             for s, h in heads}
        kb = {(s, h): col(1, s, h) * b[s, h] for s, h in heads}
        kq = {}
        for s, pr in pairs:
            h1, h2 = 2 * pr, 2 * pr + 1
            lhs = jnp.concatenate([jnp.concatenate([kb[s, h1], kb[s, h2]], axis=1),
                                   jnp.concatenate([col(0, s, h1), col(0, s, h2)], axis=1)], axis=0)
            kq[s, pr] = _mm(lhs, side_by_side(col(1, s, h1), col(1, s, h2)), _NT)
        x, qkd, tk = {}, {}, {}
        for s, pr in pairs:
            gcp = jnp.where(half, g[s, 2 * pr + 1], g[s, 2 * pr])
            rowp = jnp.sum(jnp.where(diag, gcp, 0.0), axis=0, keepdims=True)
            dec = jnp.where(causal, jnp.exp(jnp.where(causal, gcp - rowp, 0.0)), 0.0)
            x[s, pr] = jnp.where(strict, -(kq[s, pr][:c] * dec), 0.0)
            qkd[s, pr] = kq[s, pr][c:] * dec
            tk[s, pr] = eye + x[s, pr]
        pk = {sp: _mm(x[sp], blockdiag(x[sp])) for sp in pairs}
        for lvl in range(1, n_levels + 1):
            if lvl < n_levels:
                r = {sp: _mm(jnp.concatenate([pk[sp], tk[sp]], axis=0), blockdiag(pk[sp])) for sp in pairs}
                pk = {sp: r[sp][:c] for sp in pairs}
                tk = {sp: tk[sp] + r[sp][c:] for sp in pairs}
            else:
                tk = {sp: tk[sp] + _mm(tk[sp], blockdiag(pk[sp])) for sp in pairs}
        e = {sh: jnp.exp(g[sh]) for sh in heads}
        sol = {}
        for s, h in heads:
            rhs = jnp.concatenate([col(2, s, h) * b[s, h], kb[s, h] * e[s, h]], axis=1)
            rhs = jnp.concatenate([rhs, zeros2] if h % 2 == 0 else [zeros2, rhs], axis=0)
            sol[s, h] = _mm(tk[s, h // 2], rhs)
        wq = {(s, h): _mm(jnp.concatenate([sol[s, h][:, HEAD_D:], col(0, s, h) * e[s, h]], axis=0),
                          s_ref[s, h]) for s, h in heads}
        vn = {sh: sol[sh][:, :HEAD_D] - wq[sh][:c] for sh in heads}
        op = {(s, pr): _mm(qkd[s, pr], side_by_side(vn[s, 2 * pr], vn[s, 2 * pr + 1]))
              for s, pr in pairs}
        for s, h in heads:
            glast = gc[s][c - 1:c, h:h + 1]
            kdec = col(1, s, h) * jnp.exp(glast - g[s, h])
            s_ref[s, h] = s_ref[s, h] * jnp.exp(glast) + _mm(kdec, vn[s, h], _TN)
        for s, h in heads:
            o = _rms(wq[s, h][c:] + op[s, h // 2][:, (h % 2) * HEAD_D:(h % 2 + 1) * HEAD_D], nw_ref[...])
            cols = slice(h * HEAD_D, (h + 1) * HEAD_D)
            zz = z_ref[s, rows, cols]
            o_ref[s, rows, cols] = (o * (zz * jax.nn.sigmoid(zz))).astype(o_ref.dtype)
        return carry

    return chunk_body


def _delta_kernel(qkv_ref, z_ref, ab_ref, cbuf_ref, s0_ref, cw_ref, alog_ref, dtb_ref, nw_ref,
                  o_ref, nbuf_ref, snew_ref,
                  s_ref, xbuf_ref, qkvc_ref, gc_ref, beta_ref, *, tt, c):
    t = pl.program_id(1)
    nt = pl.num_programs(1)
    ns = qkv_ref.shape[0]
    pad = SUBLANES
    hist = CONV_W - 1

    @pl.when(t == 0)
    def _():
        s_ref[...] = s0_ref[...]
        xbuf_ref[:, pad - hist:pad, :] = cbuf_ref[...]

    xbuf_ref[:, pad:pad + tt, :] = qkv_ref[...]

    nv = tt // SUBLANES
    sub = lax.broadcasted_iota(jnp.int32, (nv, SUBLANES, HEAD_D), 1)
    for s in range(ns):
        for part in range(3):
            for h in range(N_HEADS):
                c0 = part * D_QK + h * HEAD_D
                cols = slice(c0, c0 + HEAD_D)
                xg = xbuf_ref[s, :, cols].reshape(nv + 1, SUBLANES, HEAD_D)
                y = None
                for i in range(CONV_W):
                    k = CONV_W - 1 - i
                    if k == 0:
                        tap = xg[1:]
                    else:
                        rot = pltpu.roll(xg, k, axis=1)
                        tap = jnp.where(sub >= k, rot[1:], rot[:-1])
                    term = cw_ref[i:i + 1, cols] * tap
                    y = term if y is None else y + term
                qkvc_ref[s, :, cols] = _conv_silu_norm(y, part).reshape(tt, HEAD_D)

    tail = xbuf_ref[:, pad + tt - hist:pad + tt, :]

    @pl.when(t == nt - 1)
    def _():
        nbuf_ref[...] = tail

    xbuf_ref[:, pad - hist:pad, :] = tail

    rt = lax.broadcasted_iota(jnp.int32, (c, c), 0)
    ct = lax.broadcasted_iota(jnp.int32, (c, c), 1)
    tri = jnp.where(rt >= ct, 1.0, 0.0).astype(F32)
    for s in range(ns):
        ab = ab_ref[s]
        g = -jnp.exp(alog_ref[...]) * jax.nn.softplus(ab + dtb_ref[...])
        beta_ref[s] = jax.nn.sigmoid(ab)
        for ic in range(tt // c):
            gc_ref[s, ic * c:(ic + 1) * c, :] = _mm_hi(tri, g[ic * c:(ic + 1) * c, :])

    refs = (qkvc_ref, gc_ref, beta_ref, s_ref, z_ref, nw_ref, o_ref)
    lax.fori_loop(0, tt // c, _chunk_pairs(refs, c), 0)

    @pl.when(t == nt - 1)
    def _():
        snew_ref[...] = s_ref[...]


def _delta(proj3, ab3, conv_buf, s0, p, l):
    n, t, _ = proj3.shape
    tt, c = min(TT_DELTA, t), DELTA_CHUNK
    ns = min(SEQS_DELTA, n)
    kern = functools.partial(_delta_kernel, tt=tt, c=c)
    return pl.pallas_call(
        kern,
        grid=(n // ns, t // tt),
        in_specs=[
            pl.BlockSpec((ns, tt, D_CONV), lambda i, j: (i, j, 0)),
            pl.BlockSpec((ns, tt, D_QK), lambda i, j: (i, j, D_CONV // D_QK)),
            pl.BlockSpec((ns, tt, LANES), lambda i, j: (i, j, 0)),
            pl.BlockSpec((ns, CONV_W - 1, D_CONV), lambda i, j: (i, 0, 0)),
            pl.BlockSpec((ns, N_HEADS, HEAD_D, HEAD_D), lambda i, j: (i, 0, 0, 0)),
            _layer_spec((CONV_W, D_CONV), l, 2),
            _layer_spec((1, LANES), l, 2),
            _layer_spec((1, LANES), l, 2),
            _layer_spec((1, HEAD_D), l, 2),
        ],
        out_specs=[
            pl.BlockSpec((ns, tt, D_QK), lambda i, j: (i, j, 0)),
            pl.BlockSpec((ns, CONV_W - 1, D_CONV), lambda i, j: (i, 0, 0)),
            pl.BlockSpec((ns, N_HEADS, HEAD_D, HEAD_D), lambda i, j: (i, 0, 0, 0)),
        ],
        out_shape=[
            jax.ShapeDtypeStruct((n, t, D_QK), BF16),
            jax.ShapeDtypeStruct((n, CONV_W - 1, D_CONV), F32),
            jax.ShapeDtypeStruct((n, N_HEADS, HEAD_D, HEAD_D), F32),
        ],
        scratch_shapes=[
            pltpu.VMEM((ns, N_HEADS, HEAD_D, HEAD_D), F32),
            pltpu.VMEM((ns, tt + SUBLANES, D_CONV), F32),
            pltpu.VMEM((ns, tt, D_CONV), F32),
            pltpu.VMEM((ns, tt, LANES), F32),
            pltpu.VMEM((ns, tt, LANES), F32),
        ],
        compiler_params=_params(("arbitrary", "arbitrary")),
        name="delta",
    )(proj3, proj3, ab3, conv_buf, s0, p["conv_w"], p["a_log"], p["dt_bias"], p["delta_norm_w"])


def _mix_kernel(uv_ref, oa_ref, ga_ref, gb_ref, x_ref, lnw_ref, lnb_ref, ws_ref, bs_ref,
                wa_ref, wb_ref, wo_ref, nw_ref, y_ref, v_ref, ob_ref, *, keep_all_v):
    c = MLP_CHUNK
    tm = uv_ref.shape[0]
    n_ch = tm // c
    tn = D_MODEL // n_ch
    ri = lax.broadcasted_iota(jnp.int32, (c, c), 0)
    ci = lax.broadcasted_iota(jnp.int32, (c, c), 1)
    gd = D_B // N_GROUPS
    ws = [jnp.where(ri >= ci, ws_ref[g], 0.0).astype(BF16) for g in range(N_GROUPS)]
    pa = []
    for ch in range(n_ch):
        rows = slice(ch * c, (ch + 1) * c)
        x = uv_ref[rows, :]
        gel = 0.5 * x * (1.0 + lax.erf(x * (2.0 ** -0.5)))
        u = gel[:, :D_B]
        v = gel[:, D_B:]
        vc = v - jnp.mean(v, axis=-1, keepdims=True)
        var = jnp.mean(jnp.square(vc), axis=-1, keepdims=True)
        v = vc * lax.rsqrt(var + 1e-5) * lnw_ref[...] + lnb_ref[...]
        if keep_all_v:
            v_ref[rows, :] = v
        elif ch == n_ch - 1:
            v_ref[0] = v
        for g in range(N_GROUPS):
            cols = slice(g * gd, (g + 1) * gd)
            mixed = _mm(ws[g], v[:, cols]) + bs_ref[:, g:g + 1]
            ob_ref[rows, cols] = (u[:, cols] * mixed).astype(BF16)
        pa.append(_mm(oa_ref[...], wa_ref[:, ch * tn:(ch + 1) * tn]))
    pa = jnp.concatenate(pa, axis=1)
    pb = _mm(ob_ref[...], wb_ref[...])
    merged = jax.nn.sigmoid(ga_ref[...]) * pa + jax.nn.sigmoid(gb_ref[...]) * pb
    y = _mm(merged, wo_ref[...])
    y_ref[...] = x_ref[...] + _rms(y, nw_ref[...])


def _mix(oa, proj, x2d, p, l, seq_len):
    m = x2d.shape[0]
    tm = min(TM, m)
    c = MLP_CHUNK
    row = lambda i: (i, 0)
    g0 = (D_CONV + D_QK + 2 * D_B) // D_MODEL
    short = seq_len < c
    if short:
        assert c % seq_len == 0 and tm % c == 0
        ws_key, bs_key = "ws_short", "bs_short"
        v_spec = pl.BlockSpec((tm, D_B), row)
        v_shape = jax.ShapeDtypeStruct((m, D_B), F32)
    else:
        assert seq_len % tm == 0
        ws_key, bs_key = "w_spatial", "bs_t"
        tiles_per_seq = seq_len // tm
        v_spec = pl.BlockSpec((1, c, D_B), lambda i: (i // tiles_per_seq, 0, 0))
        v_shape = jax.ShapeDtypeStruct((m // seq_len, c, D_B), F32)
    resident = dict(pipeline_mode=pl.Buffered(1))
    wspec = pl.BlockSpec((None, D_MODEL, D_MODEL), lambda i: (l, 0, 0), **resident)
    return pl.pallas_call(
        functools.partial(_mix_kernel, keep_all_v=short),
        grid=(m // tm,),
        in_specs=[
            pl.BlockSpec((tm, 2 * D_B), lambda i: (i, (D_CONV + D_QK) // (2 * D_B))),
            pl.BlockSpec((tm, D_QK), row),
            pl.BlockSpec((tm, D_MODEL), lambda i: (i, g0)),
            pl.BlockSpec((tm, D_MODEL), lambda i: (i, g0 + 1)),
            pl.BlockSpec((tm, D_MODEL), row),
            _layer_spec((1, D_B), l, 1),
            _layer_spec((1, D_B), l, 1),
            _layer_spec((N_GROUPS, c, c), l, 1),
            _layer_spec((c, LANES), l, 1),
            wspec, wspec, wspec,
            _layer_spec((1, D_MODEL), l, 1),
        ],
        out_specs=[pl.BlockSpec((tm, D_MODEL), row), v_spec],
        out_shape=[jax.ShapeDtypeStruct((m, D_MODEL), F32), v_shape],
        scratch_shapes=[pltpu.VMEM((tm, D_B), BF16)],
        compiler_params=_params(("arbitrary",)),
        name="mix",
    )(proj, oa, proj, proj, x2d, p["sgu_ln_w"], p["sgu_ln_b"], p[ws_key], p[bs_key],
      p["w_proj_a"], p["w_proj_b"], p["w_out"], p["norm_post_mix"])


def _ffn_kernel(x_ref, npre_ref, wi_ref, wd_ref, npost_ref, y_ref, act_ref):
    x = x_ref[...]
    h = _rms(x, npre_ref[...]).astype(BF16)
    tf = TF_FFN
    for f in range(D_FF // tf):
        gate = _mm(h, wi_ref[:, f * tf:(f + 1) * tf])
        up = _mm(h, wi_ref[:, D_FF + f * tf:D_FF + (f + 1) * tf])
        act_ref[:, f * tf:(f + 1) * tf] = (gate * jax.nn.sigmoid(gate) * up).astype(BF16)
    y = _mm(act_ref[...], wd_ref[...])
    y_ref[...] = x + _rms(y, npost_ref[...])


def _ffn(x2d, p, l):
    m = x2d.shape[0]
    tm = min(TM, m)
    resident = dict(pipeline_mode=pl.Buffered(1))
    return pl.pallas_call(
        _ffn_kernel,
        grid=(m // tm,),
        in_specs=[
            pl.BlockSpec((tm, D_MODEL), lambda i: (i, 0)),
            _layer_spec((1, D_MODEL), l, 1),
            pl.BlockSpec((None, D_MODEL, 2 * D_FF), lambda i: (l, 0, 0), **resident),
            pl.BlockSpec((None, D_FF, D_MODEL), lambda i: (l, 0, 0), **resident),
            _layer_spec((1, D_MODEL), l, 1),
        ],
        out_specs=pl.BlockSpec((tm, D_MODEL), lambda i: (i, 0)),
        out_shape=jax.ShapeDtypeStruct((m, D_MODEL), F32),
        scratch_shapes=[pltpu.VMEM((tm, D_FF), BF16)],
        compiler_params=_params(("arbitrary",)),
        name="ffn",
    )(x2d, p["norm_pre_ffn"], p["w_ffn_in"], p["w_ffn_out"], p["norm_post_ffn"])


def _trunk_layer(x, p, l, delta_fn):
    n, t, _ = x.shape
    m = n * t
    x2d = x.reshape(m, D_MODEL)
    proj, ab = _inproj(x2d, p, l)
    o_a, new_buf, s_new = delta_fn(proj.reshape(n, t, D_MAIN), ab.reshape(n, t, LANES))
    x1, v_rows = _mix(o_a.reshape(m, D_QK), proj, x2d, p, l, t)
    x2 = _ffn(x1, p, l)
    return x2.reshape(n, t, D_MODEL), s_new, new_buf, v_rows.reshape(n, -1, D_B)


def _prepare_params(t_short, norm_pre_mix, w_in, conv_w, a_log, dt_bias, delta_norm_w, sgu_ln_w,
                    sgu_ln_b, w_spatial, b_spatial, w_proj_a, w_proj_b, w_out, norm_post_mix,
                    norm_pre_ffn, w_ffn_in, w_ffn_out, norm_post_ffn):
    depth = w_in.shape[0]
    w_lo = w_in[:, :, :AB_OFF].astype(BF16)
    w_hi = w_in[:, :, AB_OFF + 2 * N_HEADS:].astype(BF16)
    w_ab = jnp.pad(w_in[:, :, AB_OFF:AB_OFF + 2 * N_HEADS], ((0, 0), (0, 0), (0, LANES - 2 * N_HEADS)))
    row = lambda v: v.reshape(depth, 1, -1)
    lanes = lambda v: jnp.pad(v, ((0, 0), (0, LANES - v.shape[1]))).reshape(depth, 1, LANES)
    bs_t = jnp.pad(jnp.swapaxes(b_spatial, 1, 2), ((0, 0), (0, 0), (0, LANES - N_GROUPS)))
    rep = MLP_CHUNK // t_short
    eye = jnp.eye(rep, dtype=F32)
    ws_short = jnp.einsum("ab,lgij->lgaibj", eye, w_spatial[:, :, :t_short, :t_short]).reshape(
        depth, N_GROUPS, MLP_CHUNK, MLP_CHUNK)
    return dict(
        norm_pre_mix=row(norm_pre_mix), w_lo=w_lo, w_hi=w_hi, w_ab=w_ab.astype(BF16),
        conv_w=conv_w, a_log=lanes(a_log), dt_bias=lanes(dt_bias), delta_norm_w=row(delta_norm_w),
        sgu_ln_w=row(sgu_ln_w), sgu_ln_b=row(sgu_ln_b), w_spatial=w_spatial, bs_t=bs_t,
        ws_short=ws_short, bs_short=jnp.tile(bs_t[:, :t_short], (1, rep, 1)),
        w_proj_a=w_proj_a.astype(BF16), w_proj_b=w_proj_b.astype(BF16), w_out=w_out.astype(BF16),
        norm_post_mix=row(norm_post_mix), norm_pre_ffn=row(norm_pre_ffn),
        w_ffn_in=w_ffn_in.astype(BF16), w_ffn_out=w_ffn_out.astype(BF16),
        norm_post_ffn=row(norm_post_ffn))


def kernel(x_prompt, x_sample, state_delta, state_conv, norm_pre_mix, w_in, conv_w, a_log, dt_bias,
           delta_norm_w, sgu_ln_w, sgu_ln_b, w_spatial, b_spatial, w_proj_a, w_proj_b, w_out,
           norm_post_mix, norm_pre_ffn, w_ffn_in, w_ffn_out, norm_post_ffn):
    depth = w_in.shape[0]
    nb, seq, _ = x_prompt.shape
    ndec, dec_seq, _ = x_sample.shape
    assert seq % DELTA_CHUNK == 0 and seq % MLP_CHUNK == 0
    assert dec_seq % SUBLANES == 0 and dec_seq < DELTA_CHUNK and ndec % DEC_SEQS_PER_STEP == 0
    p = _prepare_params(dec_seq, norm_pre_mix, w_in, conv_w, a_log, dt_bias, delta_norm_w, sgu_ln_w,
                        sgu_ln_b, w_spatial, b_spatial, w_proj_a, w_proj_b, w_out, norm_post_mix,
                        norm_pre_ffn, w_ffn_in, w_ffn_out, norm_post_ffn)
    y_p, y_s = x_prompt, x_sample
    conv0 = jnp.zeros((nb, CONV_W - 1, D_CONV), x_prompt.dtype)
    s_zero = jnp.zeros((nb, N_HEADS, HEAD_D, HEAD_D), state_delta.dtype)
    sd_p, sc_p, cv_p, sc_s, cv_s = [], [], [], [], []
    sd_s = None
    for l in range(depth):
        y_p, s_new, buf_new, v_rows = _trunk_layer(
            y_p, p, l, lambda proj3, ab3: _delta(proj3, ab3, conv0, s_zero, p, l))
        sd_p.append(s_new)
        sc_p.append(buf_new)
        cv_p.append(v_rows)
        y_s, sd_s, buf_new, v_rows = _trunk_layer(
            y_s, p, l, lambda proj3, ab3: _delta_dec(proj3, ab3, state_conv, state_delta, p, l, sd_s))
        sc_s.append(buf_new)
        cv_s.append(v_rows)
    return (y_p, y_s, jnp.stack(sd_p), jnp.stack(sc_p), jnp.stack(cv_p),
            sd_s, jnp.stack(sc_s), jnp.stack(cv_s))
```

```python
import functools

import jax
import jax.numpy as jnp
from jax import lax
from jax.experimental import pallas as pl
from jax.experimental.pallas import tpu as pltpu

F32 = jnp.float32
BF16 = jnp.bfloat16

D_MODEL = 1024
N_HEADS = 8
HEAD_D = 128
D_QK = N_HEADS * HEAD_D
D_CONV = 3 * D_QK
CONV_W = 4
DELTA_CHUNK = 64
MLP_CHUNK = 128
N_GROUPS = 8
D_B = 1024
D_FF = 2816
D_MAIN = D_CONV + D_QK + 2 * D_B + 2 * D_MODEL
AB_OFF = D_CONV + D_QK
LANES = 128
SUBLANES = 8

TM = 512
TM_FFN = 1024
TM_INPROJ = 512
TN_INPROJ = 2048
TF_FFN = 256
TT_DELTA = 128
SEQS_DELTA = 4
DEC_SEQS_PER_STEP = 8
VMEM_LIMIT = 48 * 1024 * 1024

_HI = lax.Precision.HIGHEST
_NT = (((1,), (1,)), ((), ()))
_TN = (((0,), (0,)), ((), ()))
_NN = (((1,), (0,)), ((), ()))


def _mm(a, b, dims=_NN):
    return lax.dot_general(a.astype(BF16), b.astype(BF16), dims, preferred_element_type=F32)


def _mm_hi(a, b, dims=_NN):
    return lax.dot_general(a, b, dims, precision=_HI, preferred_element_type=F32)


def _rms(x, w, eps=1e-6):
    return x * lax.rsqrt(jnp.mean(jnp.square(x), axis=-1, keepdims=True) + eps) * w


def _params(sem):
    return pltpu.CompilerParams(dimension_semantics=sem, vmem_limit_bytes=VMEM_LIMIT)


def _layer_spec(shape, l, ngrid):
    zeros = (0,) * len(shape)
    if ngrid == 1:
        return pl.BlockSpec((None,) + shape, lambda i: (l,) + zeros)
    return pl.BlockSpec((None,) + shape, lambda i, j: (l,) + zeros)


def _inproj_kernel(x_ref, nw_ref, wlo_ref, whi_ref, wab_ref, o_ref, ab_ref, h_ref, *, tn):
    j = pl.program_id(1)
    n_lo = AB_OFF // tn

    @pl.when(j == 0)
    def _():
        h = _rms(x_ref[...], nw_ref[...]).astype(BF16)
        h_ref[...] = h
        ab_ref[...] = _mm(h, wab_ref[...])

    @pl.when(j < n_lo)
    def _():
        o_ref[...] = _mm(h_ref[...], wlo_ref[:, pl.ds(pl.multiple_of(j * tn, tn), tn)])

    @pl.when(j >= n_lo)
    def _():
        o_ref[...] = _mm(h_ref[...], whi_ref[:, pl.ds(pl.multiple_of((j - n_lo) * tn, tn), tn)])


def _inproj(x2d, p, l):
    m = x2d.shape[0]
    tm, tn = min(TM_INPROJ, m), TN_INPROJ
    assert AB_OFF % tn == 0 and (D_MAIN - AB_OFF) % tn == 0
    resident = dict(pipeline_mode=pl.Buffered(1))
    return pl.pallas_call(
        functools.partial(_inproj_kernel, tn=tn),
        grid=(m // tm, D_MAIN // tn),
        in_specs=[
            pl.BlockSpec((tm, D_MODEL), lambda i, j: (i, 0)),
            _layer_spec((1, D_MODEL), l, 2),
            pl.BlockSpec((None, D_MODEL, AB_OFF), lambda i, j: (l, 0, 0), **resident),
            pl.BlockSpec((None, D_MODEL, D_MAIN - AB_OFF), lambda i, j: (l, 0, 0), **resident),
            _layer_spec((D_MODEL, LANES), l, 2),
        ],
        out_specs=[
            pl.BlockSpec((tm, tn), lambda i, j: (i, j)),
            pl.BlockSpec((tm, LANES), lambda i, j: (i, 0)),
        ],
        out_shape=[
            jax.ShapeDtypeStruct((m, D_MAIN), F32),
            jax.ShapeDtypeStruct((m, LANES), F32),
        ],
        scratch_shapes=[pltpu.VMEM((tm, D_MODEL), BF16)],
        compiler_params=_params(("arbitrary", "arbitrary")),
        name="inproj",
    )(x2d, p["norm_pre_mix"], p["w_lo"], p["w_hi"], p["w_ab"])


def _conv_silu_norm(y, part):
    y = y * jax.nn.sigmoid(y)
    if part < 2:
        inv = lax.rsqrt(jnp.sum(jnp.square(y), axis=-1, keepdims=True) + 1e-6)
        if part == 0:
            inv = inv * (HEAD_D ** -0.5)
        y = y * inv
    return y


def _causal_conv(xg, w_of_tap):
    sub = lax.broadcasted_iota(jnp.int32, (xg.shape[0] - 1,) + xg.shape[1:], 1)
    y = None
    for i in range(CONV_W):
        k = CONV_W - 1 - i
        if k == 0:
            tap = xg[1:]
        else:
            rot = pltpu.roll(xg, k, axis=1)
            tap = jnp.where(sub >= k, rot[1:], rot[:-1])
        term = w_of_tap(i) * tap
        y = term if y is None else y + term
    return y


def _delta_dec_kernel(qkv_ref, z_ref, ab_ref, cbuf_ref, s0_ref, cw_ref, alog_ref, dtb_ref, nw_ref,
                      *rest, nb, c, aliased):
    if aliased:
        rest = rest[1:]
    o_ref, nbuf_ref, snew_ref, xbuf_ref, qkvc_ref = rest
    pad = SUBLANES
    hist = CONV_W - 1
    xbuf_ref[:, pad - hist:pad, :] = cbuf_ref[...]
    xbuf_ref[:, pad:pad + c, :] = qkv_ref[...]
    for part in range(3):
        for h in range(N_HEADS):
            c0 = part * D_QK + h * HEAD_D
            cols = slice(c0, c0 + HEAD_D)
            y = cw_ref[0:1, cols] * xbuf_ref[:, pad - hist:pad - hist + c, cols]
            for i in range(1, CONV_W):
                y = y + cw_ref[i:i + 1, cols] * xbuf_ref[:, pad - hist + i:pad - hist + i + c, cols]
            qkvc_ref[:, :, cols] = _conv_silu_norm(y, part)
    nbuf_ref[...] = xbuf_ref[:, pad + c - hist:pad + c, :]

    ri = lax.broadcasted_iota(jnp.int32, (c, c), 0)
    ci = lax.broadcasted_iota(jnp.int32, (c, c), 1)
    causal = ri >= ci
    strict = ri > ci
    diag = ri == ci
    tri = jnp.where(causal, 1.0, 0.0).astype(F32)
    eye = jnp.where(diag, 1.0, 0.0).astype(F32)
    n_levels = c.bit_length() - 2
    items = [(i, h) for i in range(nb) for h in range(N_HEADS)]
    idx = {it: n for n, it in enumerate(items)}
    every = range(len(items))

    gc, beta = [], []
    for i in range(nb):
        g = -jnp.exp(alog_ref[...]) * jax.nn.softplus(ab_ref[i] + dtb_ref[...])
        gc.append(_mm_hi(tri, g))
        beta.append(jax.nn.sigmoid(ab_ref[i]))

    def col(part, i, h):
        return qkvc_ref[i, :, part * D_QK + h * HEAD_D:part * D_QK + (h + 1) * HEAD_D]

    gcol = [gc[i][:, h:h + 1] for i, h in items]
    bcol = [beta[i][:, N_HEADS + h:N_HEADS + h + 1] for i, h in items]
    kb = [col(1, i, h) * bcol[idx[i, h]] for i, h in items]
    kq = [_mm(jnp.concatenate([kb[idx[i, h]], col(0, i, h)], axis=0), col(1, i, h), _NT)
          for i, h in items]
    x, qk, tk = [], [], []
    for n in every:
        grow = jnp.sum(jnp.where(diag, gcol[n], 0.0), axis=0, keepdims=True)
        dec = jnp.where(causal, jnp.exp(jnp.where(causal, gcol[n] - grow, 0.0)), 0.0)
        x.append(jnp.where(strict, -(kq[n][:c] * dec), 0.0))
        qk.append(kq[n][c:] * dec)
        tk.append(eye + x[n])
    pk = [_mm(x[n], x[n]) for n in every]
    for lvl in range(1, n_levels + 1):
        if lvl < n_levels:
            r = [_mm(jnp.concatenate([pk[n], tk[n]], axis=0), pk[n]) for n in every]
            pk = [r[n][:c] for n in every]
            tk = [tk[n] + r[n][c:] for n in every]
        else:
            tk = [tk[n] + _mm(tk[n], pk[n]) for n in every]
    e = [jnp.exp(gcol[n]) for n in every]
    sol = [_mm(tk[idx[i, h]], jnp.concatenate([col(2, i, h) * bcol[idx[i, h]],
                                                kb[idx[i, h]] * e[idx[i, h]]], axis=1))
           for i, h in items]
    wq = [_mm(jnp.concatenate([sol[idx[i, h]][:, HEAD_D:], col(0, i, h) * e[idx[i, h]]], axis=0),
              s0_ref[i, h]) for i, h in items]
    vn = [sol[n][:, :HEAD_D] - wq[n][:c] for n in every]
    op = [_mm(qk[n], vn[n]) for n in every]
    for i, h in items:
        n = idx[i, h]
        glast = gc[i][c - 1:c, h:h + 1]
        kdec = col(1, i, h) * jnp.exp(glast - gcol[n])
        snew_ref[i, h] = s0_ref[i, h] * jnp.exp(glast) + _mm(kdec, vn[n], _TN)
    for i, h in items:
        n = idx[i, h]
        o = _rms(wq[n][c:] + op[n], nw_ref[...])
        cols = slice(h * HEAD_D, (h + 1) * HEAD_D)
        zz = z_ref[i, :, cols]
        o_ref[i, :, cols] = o * (zz * jax.nn.sigmoid(zz))


def _delta_dec(proj3, ab3, conv_all, s_all, p, l, s_out_prev):
    n, c, _ = proj3.shape
    depth = s_all.shape[0]
    nb = DEC_SEQS_PER_STEP
    aliased = s_out_prev is not None
    kern = functools.partial(_delta_dec_kernel, nb=nb, c=c, aliased=aliased)
    state_blk = (None, nb, N_HEADS, HEAD_D, HEAD_D)
    in_specs = [
        pl.BlockSpec((nb, c, D_CONV), lambda i: (i, 0, 0)),
        pl.BlockSpec((nb, c, D_QK), lambda i: (i, 0, D_CONV // D_QK)),
        pl.BlockSpec((nb, c, LANES), lambda i: (i, 0, 0)),
        pl.BlockSpec((None, nb, CONV_W - 1, D_CONV), lambda i: (l, i, 0, 0)),
        pl.BlockSpec(state_blk, lambda i: (l, i, 0, 0, 0)),
        _layer_spec((CONV_W, D_CONV), l, 1),
        _layer_spec((1, LANES), l, 1),
        _layer_spec((1, LANES), l, 1),
        _layer_spec((1, HEAD_D), l, 1),
    ]
    args = [proj3, proj3, ab3, conv_all, s_all, p["conv_w"], p["a_log"], p["dt_bias"], p["delta_norm_w"]]
    aliases = {}
    if aliased:
        in_specs.append(pl.BlockSpec(memory_space=pl.ANY))
        args.append(s_out_prev)
        aliases = {len(args) - 1: 2}
    return pl.pallas_call(
        kern,
        grid=(n // nb,),
        in_specs=in_specs,
        out_specs=[
            pl.BlockSpec((nb, c, D_QK), lambda i: (i, 0, 0)),
            pl.BlockSpec((nb, CONV_W - 1, D_CONV), lambda i: (i, 0, 0)),
            pl.BlockSpec(state_blk, lambda i: (l, i, 0, 0, 0)),
        ],
        out_shape=[
            jax.ShapeDtypeStruct((n, c, D_QK), F32),
            jax.ShapeDtypeStruct((n, CONV_W - 1, D_CONV), F32),
            jax.ShapeDtypeStruct((depth, n, N_HEADS, HEAD_D, HEAD_D), F32),
        ],
        scratch_shapes=[
            pltpu.VMEM((nb, c + SUBLANES, D_CONV), F32),
            pltpu.VMEM((nb, c, D_CONV), F32),
        ],
        input_output_aliases=aliases,
        compiler_params=_params(("arbitrary",)),
        name="delta_dec",
    )(*args)


def _delta_chunk(refs, ic, c):
    qkvc_ref, gc_ref, beta_ref, s_ref, z_ref, nw_ref, o_ref = refs
    assert 2 * c == LANES
    ns = gc_ref.shape[0]
    ri = lax.broadcasted_iota(jnp.int32, (c, LANES), 0)
    lane = lax.broadcasted_iota(jnp.int32, (c, LANES), 1)
    half = lane >= c
    cj = jnp.where(half, lane - c, lane)
    causal = ri >= cj
    strict = ri > cj
    diag = ri == cj
    eye = jnp.where(diag, 1.0, 0.0).astype(F32)
    zeros = jnp.zeros((c, HEAD_D), F32)
    zeros2 = jnp.zeros((c, 2 * HEAD_D), F32)
    n_levels = c.bit_length() - 2

    def blockdiag(m):
        return jnp.concatenate([jnp.where(half, 0.0, m), jnp.where(half, m, 0.0)], axis=0)

    def side_by_side(a, b):
        return jnp.concatenate([jnp.concatenate([a, zeros], axis=1),
                                jnp.concatenate([zeros, b], axis=1)], axis=0)

    rows = pl.ds(pl.multiple_of(ic * c, c), c)
    heads = [(s, h) for s in range(ns) for h in range(N_HEADS)]
    pairs = [(s, pr) for s in range(ns) for pr in range(N_HEADS // 2)]
    gc = [gc_ref[s, rows, :] for s in range(ns)]
    beta = [beta_ref[s, rows, :] for s in range(ns)]

    def col(part, s, h):
        return qkvc_ref[s, rows, part * D_QK + h * HEAD_D:part * D_QK + (h + 1) * HEAD_D]

    g = {(s, h): jnp.broadcast_to(gc[s][:, h:h + 1], (c, HEAD_D)) for s, h in heads}
    b = {(s, h): jnp.broadcast_to(beta[s][:, N_HEADS + h:N_HEADS + h + 1], (c, HEAD_D))
         for s, h in heads}
    kb = {(s, h): col(1, s, h) * b[s, h] for s, h in heads}
    kq = {}
    for s, pr in pairs:
        h1, h2 = 2 * pr, 2 * pr + 1
        lhs = jnp.concatenate([jnp.concatenate([kb[s, h1], kb[s, h2]], axis=1),
                               jnp.concatenate([col(0, s, h1), col(0, s, h2)], axis=1)], axis=0)
        kq[s, pr] = _mm(lhs, side_by_side(col(1, s, h1), col(1, s, h2)), _NT)
    x, qkd, tk = {}, {}, {}
    for s, pr in pairs:
        gcp = jnp.where(half, g[s, 2 * pr + 1], g[s, 2 * pr])
        rowp = jnp.sum(jnp.where(diag, gcp, 0.0), axis=0, keepdims=True)
        dec = jnp.where(causal, jnp.exp(jnp.where(causal, gcp - rowp, 0.0)), 0.0)
        x[s, pr] = jnp.where(strict, -(kq[s, pr][:c] * dec), 0.0)
        qkd[s, pr] = kq[s, pr][c:] * dec
        tk[s, pr] = eye + x[s, pr]
    pk = {sp: _mm(x[sp], blockdiag(x[sp])) for sp in pairs}
    for lvl in range(1, n_levels + 1):
        if lvl < n_levels:
            r = {sp: _mm(jnp.concatenate([pk[sp], tk[sp]], axis=0), blockdiag(pk[sp])) for sp in pairs}
            pk = {sp: r[sp][:c] for sp in pairs}
            tk = {sp: tk[sp] + r[sp][c:] for sp in pairs}
        else:
            tk = {sp: tk[sp] + _mm(tk[sp], blockdiag(pk[sp])) for sp in pairs}
    e = {sh: jnp.exp(g[sh]) for sh in heads}
    sol = {}
    for s, h in heads:
        rhs = jnp.concatenate([col(2, s, h) * b[s, h], kb[s, h] * e[s, h]], axis=1)
        rhs = jnp.concatenate([rhs, zeros2] if h % 2 == 0 else [zeros2, rhs], axis=0)
        sol[s, h] = _mm(tk[s, h // 2], rhs)
    wq = {(s, h): _mm(jnp.concatenate([sol[s, h][:, HEAD_D:], col(0, s, h) * e[s, h]], axis=0),
                      s_ref[s, h]) for s, h in heads}
    vn = {sh: sol[sh][:, :HEAD_D] - wq[sh][:c] for sh in heads}
    op = {(s, pr): _mm(qkd[s, pr], side_by_side(vn[s, 2 * pr], vn[s, 2 * pr + 1]))
          for s, pr in pairs}
    for s, h in heads:
        glast = gc[s][c - 1:c, h:h + 1]
        kdec = col(1, s, h) * jnp.exp(glast - g[s, h])
        s_ref[s, h] = s_ref[s, h] * jnp.exp(glast) + _mm(kdec, vn[s, h], _TN)
    for s, h in heads:
        o = _rms(wq[s, h][c:] + op[s, h // 2][:, (h % 2) * HEAD_D:(h % 2 + 1) * HEAD_D], nw_ref[...])
        cols = slice(h * HEAD_D, (h + 1) * HEAD_D)
        zz = z_ref[s, rows, cols]
        o_ref[s, rows, cols] = (o * (zz * jax.nn.sigmoid(zz))).astype(o_ref.dtype)


def _delta_kernel(qkv_ref, z_ref, ab_ref, cbuf_ref, s0_ref, cw_ref, alog_ref, dtb_ref, nw_ref,
                  o_ref, nbuf_ref, snew_ref,
                  s_ref, hist_ref, qkvc_ref, gc_ref, beta_ref, *, tt, c):
    t = pl.program_id(1)
    nt = pl.num_programs(1)
    ns = qkv_ref.shape[0]
    hist = CONV_W - 1

    @pl.when(t == 0)
    def _():
        s_ref[...] = s0_ref[...]
        hist_ref[...] = jnp.zeros(hist_ref.shape, F32)
        hist_ref[:, SUBLANES - hist:, :] = cbuf_ref[...]

    for s in range(ns):
        for part in range(3):
            for h in range(N_HEADS):
                c0 = part * D_QK + h * HEAD_D
                cols = slice(c0, c0 + HEAD_D)
                raw = qkv_ref[s, :, cols]
                xg = jnp.concatenate([hist_ref[s, :, cols], raw], axis=0)
                xg = xg.reshape(tt // SUBLANES + 1, SUBLANES, HEAD_D)
                y = _causal_conv(xg, lambda i: cw_ref[i:i + 1, cols])
                qkvc_ref[s, :, cols] = _conv_silu_norm(y, part).reshape(tt, HEAD_D)
                hist_ref[s, :, cols] = raw[tt - SUBLANES:, :]

    @pl.when(t == nt - 1)
    def _():
        nbuf_ref[...] = hist_ref[:, SUBLANES - hist:, :]

    rt = lax.broadcasted_iota(jnp.int32, (c, c), 0)
    ct = lax.broadcasted_iota(jnp.int32, (c, c), 1)
    tri = jnp.where(rt >= ct, 1.0, 0.0).astype(F32)
    for s in range(ns):
        ab = ab_ref[s]
        g = -jnp.exp(alog_ref[...]) * jax.nn.softplus(ab + dtb_ref[...])
        beta_ref[s] = jax.nn.sigmoid(ab)
        for ic in range(tt // c):
            gc_ref[s, ic * c:(ic + 1) * c, :] = _mm_hi(tri, g[ic * c:(ic + 1) * c, :])

    refs = (qkvc_ref, gc_ref, beta_ref, s_ref, z_ref, nw_ref, o_ref)

    def body(ic, carry):
        _delta_chunk(refs, ic, c)
        return carry

    lax.fori_loop(0, tt // c, body, 0)

    @pl.when(t == nt - 1)
    def _():
        snew_ref[...] = s_ref[...]


def _delta(proj3, ab3, conv_buf, s0, p, l):
    n, t, _ = proj3.shape
    tt, c = min(TT_DELTA, t), DELTA_CHUNK
    ns = min(SEQS_DELTA, n)
    kern = functools.partial(_delta_kernel, tt=tt, c=c)
    tile = lambda i, j: (i, j, 0)
    return pl.pallas_call(
        kern,
        grid=(n // ns, t // tt),
        in_specs=[
            pl.BlockSpec((ns, tt, D_CONV), tile),
            pl.BlockSpec((ns, tt, D_QK), lambda i, j: (i, j, D_CONV // D_QK)),
            pl.BlockSpec((ns, tt, LANES), tile),
            pl.BlockSpec((ns, CONV_W - 1, D_CONV), lambda i, j: (i, 0, 0)),
            pl.BlockSpec((ns, N_HEADS, HEAD_D, HEAD_D), lambda i, j: (i, 0, 0, 0)),
            _layer_spec((CONV_W, D_CONV), l, 2),
            _layer_spec((1, LANES), l, 2),
            _layer_spec((1, LANES), l, 2),
            _layer_spec((1, HEAD_D), l, 2),
        ],
        out_specs=[
            pl.BlockSpec((ns, tt, D_QK), tile),
            pl.BlockSpec((ns, CONV_W - 1, D_CONV), lambda i, j: (i, 0, 0)),
            pl.BlockSpec((ns, N_HEADS, HEAD_D, HEAD_D), lambda i, j: (i, 0, 0, 0)),
        ],
        out_shape=[
            jax.ShapeDtypeStruct((n, t, D_QK), BF16),
            jax.ShapeDtypeStruct((n, CONV_W - 1, D_CONV), F32),
            jax.ShapeDtypeStruct((n, N_HEADS, HEAD_D, HEAD_D), F32),
        ],
        scratch_shapes=[
            pltpu.VMEM((ns, N_HEADS, HEAD_D, HEAD_D), F32),
            pltpu.VMEM((ns, SUBLANES, D_CONV), F32),
            pltpu.VMEM((ns, tt, D_CONV), F32),
            pltpu.VMEM((ns, tt, LANES), F32),
            pltpu.VMEM((ns, tt, LANES), F32),
        ],
        compiler_params=_params(("arbitrary", "arbitrary")),
        name="delta",
    )(proj3, proj3, ab3, conv_buf, s0, p["conv_w"], p["a_log"], p["dt_bias"], p["delta_norm_w"])


def _mix_kernel(uv_ref, oa_ref, ga_ref, gb_ref, x_ref, lnw_ref, lnb_ref, ws_ref, bs_ref,
                wa_ref, wb_ref, wo_ref, nw_ref, y_ref, v_ref, ob_ref, *, keep_all_v):
    c = MLP_CHUNK
    tm = uv_ref.shape[0]
    n_ch = tm // c
    tn = D_MODEL // n_ch
    ri = lax.broadcasted_iota(jnp.int32, (c, c), 0)
    ci = lax.broadcasted_iota(jnp.int32, (c, c), 1)
    gd = D_B // N_GROUPS
    ws = [jnp.where(ri >= ci, ws_ref[g], 0.0).astype(BF16) for g in range(N_GROUPS)]
    pa = []
    for ch in range(n_ch):
        pa.append(_mm(oa_ref[...], wa_ref[:, ch * tn:(ch + 1) * tn]))
        rows = slice(ch * c, (ch + 1) * c)
        x = uv_ref[rows, :]
        gel = 0.5 * x * (1.0 + lax.erf(x * (2.0 ** -0.5)))
        u = gel[:, :D_B]
        v = gel[:, D_B:]
        vc = v - jnp.mean(v, axis=-1, keepdims=True)
        var = jnp.mean(jnp.square(vc), axis=-1, keepdims=True)
        v = vc * lax.rsqrt(var + 1e-5) * lnw_ref[...] + lnb_ref[...]
        if keep_all_v:
            v_ref[rows, :] = v
        elif ch == n_ch - 1:
            v_ref[0] = v
        for g in range(N_GROUPS):
            cols = slice(g * gd, (g + 1) * gd)
            mixed = _mm(ws[g], v[:, cols]) + bs_ref[:, g:g + 1]
            ob_ref[rows, cols] = (u[:, cols] * mixed).astype(BF16)
    pa = jnp.concatenate(pa, axis=1)
    pb = _mm(ob_ref[...], wb_ref[...])
    merged = jax.nn.sigmoid(ga_ref[...]) * pa + jax.nn.sigmoid(gb_ref[...]) * pb
    y = _mm(merged, wo_ref[...])
    y_ref[...] = x_ref[...] + _rms(y, nw_ref[...])


def _mix(oa, proj, x2d, p, l, seq_len):
    m = x2d.shape[0]
    tm = min(TM, m)
    c = MLP_CHUNK
    row = lambda i: (i, 0)
    g0 = (D_CONV + D_QK + 2 * D_B) // D_MODEL
    short = seq_len < c
    if short:
        assert c % seq_len == 0 and tm % c == 0
        ws_key, bs_key = "ws_short", "bs_short"
        v_spec = pl.BlockSpec((tm, D_B), row)
        v_shape = jax.ShapeDtypeStruct((m, D_B), F32)
    else:
        assert seq_len % tm == 0
        ws_key, bs_key = "w_spatial", "bs_t"
        tiles_per_seq = seq_len // tm
        v_spec = pl.BlockSpec((1, c, D_B), lambda i: (i // tiles_per_seq, 0, 0))
        v_shape = jax.ShapeDtypeStruct((m // seq_len, c, D_B), F32)
    resident = dict(pipeline_mode=pl.Buffered(1))
    wspec = pl.BlockSpec((None, D_MODEL, D_MODEL), lambda i: (l, 0, 0), **resident)
    return pl.pallas_call(
        functools.partial(_mix_kernel, keep_all_v=short),
        grid=(m // tm,),
        in_specs=[
            pl.BlockSpec((tm, 2 * D_B), lambda i: (i, (D_CONV + D_QK) // (2 * D_B))),
            pl.BlockSpec((tm, D_QK), row),
            pl.BlockSpec((tm, D_MODEL), lambda i: (i, g0)),
            pl.BlockSpec((tm, D_MODEL), lambda i: (i, g0 + 1)),
            pl.BlockSpec((tm, D_MODEL), row),
            _layer_spec((1, D_B), l, 1),
            _layer_spec((1, D_B), l, 1),
            _layer_spec((N_GROUPS, c, c), l, 1),
            _layer_spec((c, LANES), l, 1),
            wspec, wspec, wspec,
            _layer_spec((1, D_MODEL), l, 1),
        ],
        out_specs=[pl.BlockSpec((tm, D_MODEL), row), v_spec],
        out_shape=[jax.ShapeDtypeStruct((m, D_MODEL), F32), v_shape],
        scratch_shapes=[pltpu.VMEM((tm, D_B), BF16)],
        compiler_params=_params(("arbitrary",)),
        name="mix",
    )(proj, oa, proj, proj, x2d, p["sgu_ln_w"], p["sgu_ln_b"], p[ws_key], p[bs_key],
      p["w_proj_a"], p["w_proj_b"], p["w_out"], p["norm_post_mix"])


def _ffn_kernel(x_ref, npre_ref, wi_ref, wd_ref, npost_ref, y_ref, act_ref):
    x = x_ref[...]
    h = _rms(x, npre_ref[...]).astype(BF16)
    tf = TF_FFN
    for f in range(D_FF // tf):
        gate = _mm(h, wi_ref[:, f * tf:(f + 1) * tf])
        up = _mm(h, wi_ref[:, D_FF + f * tf:D_FF + (f + 1) * tf])
        act_ref[:, f * tf:(f + 1) * tf] = (gate * jax.nn.sigmoid(gate) * up).astype(BF16)
    y = _mm(act_ref[...], wd_ref[...])
    y_ref[...] = x + _rms(y, npost_ref[...])


def _ffn(x2d, p, l):
    m = x2d.shape[0]
    tm = min(TM_FFN, m)
    resident = dict(pipeline_mode=pl.Buffered(1))
    return pl.pallas_call(
        _ffn_kernel,
        grid=(m // tm,),
        in_specs=[
            pl.BlockSpec((tm, D_MODEL), lambda i: (i, 0)),
            _layer_spec((1, D_MODEL), l, 1),
            pl.BlockSpec((None, D_MODEL, 2 * D_FF), lambda i: (l, 0, 0), **resident),
            pl.BlockSpec((None, D_FF, D_MODEL), lambda i: (l, 0, 0), **resident),
            _layer_spec((1, D_MODEL), l, 1),
        ],
        out_specs=pl.BlockSpec((tm, D_MODEL), lambda i: (i, 0)),
        out_shape=jax.ShapeDtypeStruct((m, D_MODEL), F32),
        scratch_shapes=[pltpu.VMEM((tm, D_FF), BF16)],
        compiler_params=_params(("arbitrary",)),
        name="ffn",
    )(x2d, p["norm_pre_ffn"], p["w_ffn_in"], p["w_ffn_out"], p["norm_post_ffn"])


def _trunk_layer(x, p, l, delta_fn):
    n, t, _ = x.shape
    m = n * t
    x2d = x.reshape(m, D_MODEL)
    proj, ab = _inproj(x2d, p, l)
    o_a, new_buf, s_new = delta_fn(proj.reshape(n, t, D_MAIN), ab.reshape(n, t, LANES))
    x1, v_rows = _mix(o_a.reshape(m, D_QK), proj, x2d, p, l, t)
    x2 = _ffn(x1, p, l)
    return x2.reshape(n, t, D_MODEL), s_new, new_buf, v_rows.reshape(n, -1, D_B)


def _prepare_params(t_short, norm_pre_mix, w_in, conv_w, a_log, dt_bias, delta_norm_w, sgu_ln_w,
                    sgu_ln_b, w_spatial, b_spatial, w_proj_a, w_proj_b, w_out, norm_post_mix,
                    norm_pre_ffn, w_ffn_in, w_ffn_out, norm_post_ffn):
    depth = w_in.shape[0]
    w_lo = w_in[:, :, :AB_OFF].astype(BF16)
    w_hi = w_in[:, :, AB_OFF + 2 * N_HEADS:].astype(BF16)
    w_ab = jnp.pad(w_in[:, :, AB_OFF:AB_OFF + 2 * N_HEADS], ((0, 0), (0, 0), (0, LANES - 2 * N_HEADS)))
    row = lambda v: v.reshape(depth, 1, -1)
    lanes = lambda v: jnp.pad(v, ((0, 0), (0, LANES - v.shape[1]))).reshape(depth, 1, LANES)
    bs_t = jnp.pad(jnp.swapaxes(b_spatial, 1, 2), ((0, 0), (0, 0), (0, LANES - N_GROUPS)))
    rep = MLP_CHUNK // t_short
    idx = jnp.arange(MLP_CHUNK)
    same_block = (idx[:, None] // t_short) == (idx[None, :] // t_short)
    onehot = (idx[:, None] % t_short == jnp.arange(t_short)[None, :]).astype(F32)
    tiled = jnp.einsum("ri,lgij,cj->lgrc", onehot, w_spatial[:, :, :t_short, :t_short], onehot,
                       precision=_HI)
    ws_short = jnp.where(same_block, tiled, 0.0)
    return dict(
        norm_pre_mix=row(norm_pre_mix), w_lo=w_lo, w_hi=w_hi, w_ab=w_ab.astype(BF16),
        conv_w=conv_w, a_log=lanes(a_log), dt_bias=lanes(dt_bias), delta_norm_w=row(delta_norm_w),
        sgu_ln_w=row(sgu_ln_w), sgu_ln_b=row(sgu_ln_b), w_spatial=w_spatial, bs_t=bs_t,
        ws_short=ws_short, bs_short=jnp.tile(bs_t[:, :t_short], (1, rep, 1)),
        w_proj_a=w_proj_a.astype(BF16), w_proj_b=w_proj_b.astype(BF16), w_out=w_out.astype(BF16),
        norm_post_mix=row(norm_post_mix), norm_pre_ffn=row(norm_pre_ffn),
        w_ffn_in=w_ffn_in.astype(BF16), w_ffn_out=w_ffn_out.astype(BF16),
        norm_post_ffn=row(norm_post_ffn))


def kernel(x_prompt, x_sample, state_delta, state_conv, norm_pre_mix, w_in, conv_w, a_log, dt_bias,
           delta_norm_w, sgu_ln_w, sgu_ln_b, w_spatial, b_spatial, w_proj_a, w_proj_b, w_out,
           norm_post_mix, norm_pre_ffn, w_ffn_in, w_ffn_out, norm_post_ffn):
    depth = w_in.shape[0]
    nb, seq, _ = x_prompt.shape
    ndec, dec_seq, _ = x_sample.shape
    assert seq % DELTA_CHUNK == 0 and seq % MLP_CHUNK == 0
    assert dec_seq % SUBLANES == 0 and dec_seq < DELTA_CHUNK and ndec % DEC_SEQS_PER_STEP == 0
    p = _prepare_params(dec_seq, norm_pre_mix, w_in, conv_w, a_log, dt_bias, delta_norm_w, sgu_ln_w,
                        sgu_ln_b, w_spatial, b_spatial, w_proj_a, w_proj_b, w_out, norm_post_mix,
                        norm_pre_ffn, w_ffn_in, w_ffn_out, norm_post_ffn)
    y_p, y_s = x_prompt, x_sample
    conv0 = jnp.zeros((nb, CONV_W - 1, D_CONV), x_prompt.dtype)
    s_zero = jnp.zeros((nb, N_HEADS, HEAD_D, HEAD_D), state_delta.dtype)
    sd_p, sc_p, cv_p, sc_s, cv_s = [], [], [], [], []
    sd_s = None
    for l in range(depth):
        y_p, s_new, buf_new, v_rows = _trunk_layer(
            y_p, p, l, lambda proj3, ab3: _delta(proj3, ab3, conv0, s_zero, p, l))
        sd_p.append(s_new)
        sc_p.append(buf_new)
        cv_p.append(v_rows)
        y_s, sd_s, buf_new, v_rows = _trunk_layer(
            y_s, p, l, lambda proj3, ab3: _delta_dec(proj3, ab3, state_conv, state_delta, p, l, sd_s))
        sc_s.append(buf_new)
        cv_s.append(v_rows)
    return (y_p, y_s, jnp.stack(sd_p), jnp.stack(sc_p), jnp.stack(cv_p),
            sd_s, jnp.stack(sc_s), jnp.stack(cv_s))
```

```python
import functools

import jax
import jax.numpy as jnp
from jax import lax
from jax.experimental import pallas as pl
from jax.experimental.pallas import tpu as pltpu

F32 = jnp.float32
BF16 = jnp.bfloat16

D_MODEL = 1024
N_HEADS = 8
HEAD_D = 128
D_QK = N_HEADS * HEAD_D
D_CONV = 3 * D_QK
CONV_W = 4
DELTA_CHUNK = 64
MLP_CHUNK = 128
N_GROUPS = 8
D_B = 1024
D_FF = 2816
D_MAIN = D_CONV + D_QK + 2 * D_B + 2 * D_MODEL
AB_OFF = D_CONV + D_QK
LANES = 128
SUBLANES = 8

TM = 512
TM_FFN = 1024
TF_FFN = 256
TT_DELTA = 128
SEQS_DELTA = 4
DEC_SEQS_PER_STEP = 8
VMEM_LIMIT = 48 * 1024 * 1024

_HI = lax.Precision.HIGHEST
_NT = (((1,), (1,)), ((), ()))
_TN = (((0,), (0,)), ((), ()))
_NN = (((1,), (0,)), ((), ()))


def _mm(a, b, dims=_NN):
    return lax.dot_general(a.astype(BF16), b.astype(BF16), dims, preferred_element_type=F32)


def _mm_hi(a, b, dims=_NN):
    return lax.dot_general(a, b, dims, precision=_HI, preferred_element_type=F32)


def _rms(x, w, eps=1e-6):
    return x * lax.rsqrt(jnp.mean(jnp.square(x), axis=-1, keepdims=True) + eps) * w


def _params(sem):
    return pltpu.CompilerParams(dimension_semantics=sem, vmem_limit_bytes=VMEM_LIMIT)


def _layer_spec(shape, l, ngrid):
    zeros = (0,) * len(shape)
    if ngrid == 1:
        return pl.BlockSpec((None,) + shape, lambda i: (l,) + zeros)
    return pl.BlockSpec((None,) + shape, lambda i, j: (l,) + zeros)


def _inproj_kernel(x_ref, nw_ref, wlo_ref, wab_ref, o_ref, ab_ref):
    h = _rms(x_ref[...], nw_ref[...]).astype(BF16)
    ab_ref[...] = _mm(h, wab_ref[...])
    o_ref[...] = _mm(h, wlo_ref[...])


def _inproj(x2d, p, l):
    m = x2d.shape[0]
    tm = min(TM, m)
    return pl.pallas_call(
        _inproj_kernel,
        grid=(m // tm,),
        in_specs=[
            pl.BlockSpec((tm, D_MODEL), lambda i: (i, 0)),
            _layer_spec((1, D_MODEL), l, 1),
            pl.BlockSpec((None, D_MODEL, AB_OFF), lambda i: (l, 0, 0), pipeline_mode=pl.Buffered(1)),
            _layer_spec((D_MODEL, LANES), l, 1),
        ],
        out_specs=[
            pl.BlockSpec((tm, AB_OFF), lambda i: (i, 0)),
            pl.BlockSpec((tm, LANES), lambda i: (i, 0)),
        ],
        out_shape=[
            jax.ShapeDtypeStruct((m, AB_OFF), F32),
            jax.ShapeDtypeStruct((m, LANES), F32),
        ],
        compiler_params=_params(("arbitrary",)),
        name="inproj",
    )(x2d, p["norm_pre_mix"], p["w_lo"], p["w_ab"])


def _conv_silu_norm(y, part):
    y = y * jax.nn.sigmoid(y)
    if part < 2:
        inv = lax.rsqrt(jnp.sum(jnp.square(y), axis=-1, keepdims=True) + 1e-6)
        if part == 0:
            inv = inv * (HEAD_D ** -0.5)
        y = y * inv
    return y


def _causal_conv(xg, w_of_tap):
    sub = lax.broadcasted_iota(jnp.int32, (xg.shape[0] - 1,) + xg.shape[1:], 1)
    y = None
    for i in range(CONV_W):
        k = CONV_W - 1 - i
        if k == 0:
            tap = xg[1:]
        else:
            rot = pltpu.roll(xg, k, axis=1)
            tap = jnp.where(sub >= k, rot[1:], rot[:-1])
        term = w_of_tap(i) * tap
        y = term if y is None else y + term
    return y


def _delta_dec_kernel(qkv_ref, z_ref, ab_ref, cbuf_ref, s0_ref, cw_ref, alog_ref, dtb_ref, nw_ref,
                      *rest, nb, c, aliased):
    if aliased:
        rest = rest[1:]
    o_ref, nbuf_ref, snew_ref, xbuf_ref, qkvc_ref = rest
    pad = SUBLANES
    hist = CONV_W - 1
    xbuf_ref[:, pad - hist:pad, :] = cbuf_ref[...]
    xbuf_ref[:, pad:pad + c, :] = qkv_ref[...]
    for part in range(3):
        for h in range(N_HEADS):
            c0 = part * D_QK + h * HEAD_D
            cols = slice(c0, c0 + HEAD_D)
            y = cw_ref[0:1, cols] * xbuf_ref[:, pad - hist:pad - hist + c, cols]
            for i in range(1, CONV_W):
                y = y + cw_ref[i:i + 1, cols] * xbuf_ref[:, pad - hist + i:pad - hist + i + c, cols]
            qkvc_ref[:, :, cols] = _conv_silu_norm(y, part)
    nbuf_ref[...] = xbuf_ref[:, pad + c - hist:pad + c, :]

    ri = lax.broadcasted_iota(jnp.int32, (c, c), 0)
    ci = lax.broadcasted_iota(jnp.int32, (c, c), 1)
    causal = ri >= ci
    strict = ri > ci
    diag = ri == ci
    tri = jnp.where(causal, 1.0, 0.0).astype(F32)
    eye = jnp.where(diag, 1.0, 0.0).astype(F32)
    n_levels = c.bit_length() - 2
    items = [(i, h) for i in range(nb) for h in range(N_HEADS)]
    idx = {it: n for n, it in enumerate(items)}
    every = range(len(items))

    gc, beta = [], []
    for i in range(nb):
        g = -jnp.exp(alog_ref[...]) * jax.nn.softplus(ab_ref[i] + dtb_ref[...])
        gc.append(_mm_hi(tri, g))
        beta.append(jax.nn.sigmoid(ab_ref[i]))

    def col(part, i, h):
        return qkvc_ref[i, :, part * D_QK + h * HEAD_D:part * D_QK + (h + 1) * HEAD_D]

    gcol = [gc[i][:, h:h + 1] for i, h in items]
    bcol = [beta[i][:, N_HEADS + h:N_HEADS + h + 1] for i, h in items]
    kb = [col(1, i, h) * bcol[idx[i, h]] for i, h in items]
    kq = [_mm(jnp.concatenate([kb[idx[i, h]], col(0, i, h)], axis=0), col(1, i, h), _NT)
          for i, h in items]
    x, qk, tk = [], [], []
    for n in every:
        grow = jnp.sum(jnp.where(diag, gcol[n], 0.0), axis=0, keepdims=True)
        dec = jnp.where(causal, jnp.exp(jnp.where(causal, gcol[n] - grow, 0.0)), 0.0)
        x.append(jnp.where(strict, -(kq[n][:c] * dec), 0.0))
        qk.append(kq[n][c:] * dec)
        tk.append(eye + x[n])
    pk = [_mm(x[n], x[n]) for n in every]
    for lvl in range(1, n_levels + 1):
        if lvl < n_levels:
            r = [_mm(jnp.concatenate([pk[n], tk[n]], axis=0), pk[n]) for n in every]
            pk = [r[n][:c] for n in every]
            tk = [tk[n] + r[n][c:] for n in every]
        else:
            tk = [tk[n] + _mm(tk[n], pk[n]) for n in every]
    e = [jnp.exp(gcol[n]) for n in every]
    sol = [_mm(tk[idx[i, h]], jnp.concatenate([col(2, i, h) * bcol[idx[i, h]],
                                                kb[idx[i, h]] * e[idx[i, h]]], axis=1))
           for i, h in items]
    wq = [_mm(jnp.concatenate([sol[idx[i, h]][:, HEAD_D:], col(0, i, h) * e[idx[i, h]]], axis=0),
              s0_ref[i, h]) for i, h in items]
    vn = [sol[n][:, :HEAD_D] - wq[n][:c] for n in every]
    op = [_mm(qk[n], vn[n]) for n in every]
    for i, h in items:
        n = idx[i, h]
        glast = gc[i][c - 1:c, h:h + 1]
        kdec = col(1, i, h) * jnp.exp(glast - gcol[n])
        snew_ref[i, h] = s0_ref[i, h] * jnp.exp(glast) + _mm(kdec, vn[n], _TN)
    for i, h in items:
        n = idx[i, h]
        o = _rms(wq[n][c:] + op[n], nw_ref[...])
        cols = slice(h * HEAD_D, (h + 1) * HEAD_D)
        zz = z_ref[i, :, cols]
        o_ref[i, :, cols] = o * (zz * jax.nn.sigmoid(zz))


def _delta_dec(proj3, ab3, conv_all, s_all, p, l, s_out_prev):
    n, c, _ = proj3.shape
    depth = s_all.shape[0]
    nb = DEC_SEQS_PER_STEP
    aliased = s_out_prev is not None
    kern = functools.partial(_delta_dec_kernel, nb=nb, c=c, aliased=aliased)
    state_blk = (None, nb, N_HEADS, HEAD_D, HEAD_D)
    in_specs = [
        pl.BlockSpec((nb, c, D_CONV), lambda i: (i, 0, 0)),
        pl.BlockSpec((nb, c, D_QK), lambda i: (i, 0, D_CONV // D_QK)),
        pl.BlockSpec((nb, c, LANES), lambda i: (i, 0, 0)),
        pl.BlockSpec((None, nb, CONV_W - 1, D_CONV), lambda i: (l, i, 0, 0)),
        pl.BlockSpec(state_blk, lambda i: (l, i, 0, 0, 0)),
        _layer_spec((CONV_W, D_CONV), l, 1),
        _layer_spec((1, LANES), l, 1),
        _layer_spec((1, LANES), l, 1),
        _layer_spec((1, HEAD_D), l, 1),
    ]
    args = [proj3, proj3, ab3, conv_all, s_all, p["conv_w"], p["a_log"], p["dt_bias"], p["delta_norm_w"]]
    aliases = {}
    if aliased:
        in_specs.append(pl.BlockSpec(memory_space=pl.ANY))
        args.append(s_out_prev)
        aliases = {len(args) - 1: 2}
    return pl.pallas_call(
        kern,
        grid=(n // nb,),
        in_specs=in_specs,
        out_specs=[
            pl.BlockSpec((nb, c, D_QK), lambda i: (i, 0, 0)),
            pl.BlockSpec((nb, CONV_W - 1, D_CONV), lambda i: (i, 0, 0)),
            pl.BlockSpec(state_blk, lambda i: (l, i, 0, 0, 0)),
        ],
        out_shape=[
            jax.ShapeDtypeStruct((n, c, D_QK), F32),
            jax.ShapeDtypeStruct((n, CONV_W - 1, D_CONV), F32),
            jax.ShapeDtypeStruct((depth, n, N_HEADS, HEAD_D, HEAD_D), F32),
        ],
        scratch_shapes=[
            pltpu.VMEM((nb, c + SUBLANES, D_CONV), F32),
            pltpu.VMEM((nb, c, D_CONV), F32),
        ],
        input_output_aliases=aliases,
        compiler_params=_params(("arbitrary",)),
        name="delta_dec",
    )(*args)


def _delta_chunk(refs, ic, c):
    qkvc_ref, gc_ref, beta_ref, s_ref, z_ref, nw_ref, o_ref = refs
    assert 2 * c == LANES
    ns = gc_ref.shape[0]
    ri = lax.broadcasted_iota(jnp.int32, (c, LANES), 0)
    lane = lax.broadcasted_iota(jnp.int32, (c, LANES), 1)
    half = lane >= c
    cj = jnp.where(half, lane - c, lane)
    causal = ri >= cj
    strict = ri > cj
    diag = ri == cj
    eye = jnp.where(diag, 1.0, 0.0).astype(F32)
    zeros = jnp.zeros((c, HEAD_D), F32)
    zeros2 = jnp.zeros((c, 2 * HEAD_D), F32)
    n_levels = c.bit_length() - 2

    def blockdiag(m):
        return jnp.concatenate([jnp.where(half, 0.0, m), jnp.where(half, m, 0.0)], axis=0)

    def side_by_side(a, b):
        return jnp.concatenate([jnp.concatenate([a, zeros], axis=1),
                                jnp.concatenate([zeros, b], axis=1)], axis=0)

    rows = pl.ds(pl.multiple_of(ic * c, c), c)
    heads = [(s, h) for s in range(ns) for h in range(N_HEADS)]
    pairs = [(s, pr) for s in range(ns) for pr in range(N_HEADS // 2)]
    gc = [gc_ref[s, rows, :] for s in range(ns)]
    beta = [beta_ref[s, rows, :] for s in range(ns)]

    def col(part, s, h):
        return qkvc_ref[s, rows, part * D_QK + h * HEAD_D:part * D_QK + (h + 1) * HEAD_D]

    g = {(s, h): jnp.broadcast_to(gc[s][:, h:h + 1], (c, HEAD_D)) for s, h in heads}
    b = {(s, h): jnp.broadcast_to(beta[s][:, N_HEADS + h:N_HEADS + h + 1], (c, HEAD_D))
         for s, h in heads}
    kb = {(s, h): col(1, s, h) * b[s, h] for s, h in heads}
    kq = {}
    for s, pr in pairs:
        h1, h2 = 2 * pr, 2 * pr + 1
        lhs = jnp.concatenate([jnp.concatenate([kb[s, h1], kb[s, h2]], axis=1),
                               jnp.concatenate([col(0, s, h1), col(0, s, h2)], axis=1)], axis=0)
        kq[s, pr] = _mm(lhs, side_by_side(col(1, s, h1), col(1, s, h2)), _NT)
    x, qkd, tk = {}, {}, {}
    for s, pr in pairs:
        gcp = jnp.where(half, g[s, 2 * pr + 1], g[s, 2 * pr])
        rowp = jnp.sum(jnp.where(diag, gcp, 0.0), axis=0, keepdims=True)
        dec = jnp.where(causal, jnp.exp(jnp.where(causal, gcp - rowp, 0.0)), 0.0)
        x[s, pr] = jnp.where(strict, -(kq[s, pr][:c] * dec), 0.0)
        qkd[s, pr] = kq[s, pr][c:] * dec
        tk[s, pr] = eye + x[s, pr]
    pk = {sp: _mm(x[sp], blockdiag(x[sp])) for sp in pairs}
    for lvl in range(1, n_levels + 1):
        if lvl < n_levels:
            r = {sp: _mm(jnp.concatenate([pk[sp], tk[sp]], axis=0), blockdiag(pk[sp])) for sp in pairs}
            pk = {sp: r[sp][:c] for sp in pairs}
            tk = {sp: tk[sp] + r[sp][c:] for sp in pairs}
        else:
            tk = {sp: tk[sp] + _mm(tk[sp], blockdiag(pk[sp])) for sp in pairs}
    e = {sh: jnp.exp(g[sh]) for sh in heads}
    sol = {}
    for s, h in heads:
        rhs = jnp.concatenate([col(2, s, h) * b[s, h], kb[s, h] * e[s, h]], axis=1)
        rhs = jnp.concatenate([rhs, zeros2] if h % 2 == 0 else [zeros2, rhs], axis=0)
        sol[s, h] = _mm(tk[s, h // 2], rhs)
    wq = {(s, h): _mm(jnp.concatenate([sol[s, h][:, HEAD_D:], col(0, s, h) * e[s, h]], axis=0),
                      s_ref[s, h]) for s, h in heads}
    vn = {sh: sol[sh][:, :HEAD_D] - wq[sh][:c] for sh in heads}
    op = {(s, pr): _mm(qkd[s, pr], side_by_side(vn[s, 2 * pr], vn[s, 2 * pr + 1]))
          for s, pr in pairs}
    for s, h in heads:
        glast = gc[s][c - 1:c, h:h + 1]
        kdec = col(1, s, h) * jnp.exp(glast - g[s, h])
        s_ref[s, h] = s_ref[s, h] * jnp.exp(glast) + _mm(kdec, vn[s, h], _TN)
    for s, h in heads:
        o = _rms(wq[s, h][c:] + op[s, h // 2][:, (h % 2) * HEAD_D:(h % 2 + 1) * HEAD_D], nw_ref[...])
        cols = slice(h * HEAD_D, (h + 1) * HEAD_D)
        zz = z_ref[s, rows, cols]
        o_ref[s, rows, cols] = (o * (zz * jax.nn.sigmoid(zz))).astype(o_ref.dtype)


def _delta_kernel(qkv_ref, z_ref, ab_ref, cbuf_ref, s0_ref, cw_ref, alog_ref, dtb_ref, nw_ref,
                  o_ref, nbuf_ref, snew_ref,
                  s_ref, hist_ref, qkvc_ref, gc_ref, beta_ref, *, tt, c):
    t = pl.program_id(1)
    nt = pl.num_programs(1)
    ns = qkv_ref.shape[0]
    hist = CONV_W - 1

    @pl.when(t == 0)
    def _():
        s_ref[...] = s0_ref[...]
        hist_ref[...] = jnp.zeros(hist_ref.shape, F32)
        hist_ref[:, SUBLANES - hist:, :] = cbuf_ref[...]

    for s in range(ns):
        for part in range(3):
            for h in range(N_HEADS):
                c0 = part * D_QK + h * HEAD_D
                cols = slice(c0, c0 + HEAD_D)
                raw = qkv_ref[s, :, cols]
                xg = jnp.concatenate([hist_ref[s, :, cols], raw], axis=0)
                xg = xg.reshape(tt // SUBLANES + 1, SUBLANES, HEAD_D)
                y = _causal_conv(xg, lambda i: cw_ref[i:i + 1, cols])
                qkvc_ref[s, :, cols] = _conv_silu_norm(y, part).reshape(tt, HEAD_D)
                hist_ref[s, :, cols] = raw[tt - SUBLANES:, :]

    @pl.when(t == nt - 1)
    def _():
        nbuf_ref[...] = hist_ref[:, SUBLANES - hist:, :]

    rt = lax.broadcasted_iota(jnp.int32, (c, c), 0)
    ct = lax.broadcasted_iota(jnp.int32, (c, c), 1)
    tri = jnp.where(rt >= ct, 1.0, 0.0).astype(F32)
    for s in range(ns):
        ab = ab_ref[s]
        g = -jnp.exp(alog_ref[...]) * jax.nn.softplus(ab + dtb_ref[...])
        beta_ref[s] = jax.nn.sigmoid(ab)
        for ic in range(tt // c):
            gc_ref[s, ic * c:(ic + 1) * c, :] = _mm_hi(tri, g[ic * c:(ic + 1) * c, :])

    refs = (qkvc_ref, gc_ref, beta_ref, s_ref, z_ref, nw_ref, o_ref)

    def body(ic, carry):
        _delta_chunk(refs, ic, c)
        return carry

    lax.fori_loop(0, tt // c, body, 0)

    @pl.when(t == nt - 1)
    def _():
        snew_ref[...] = s_ref[...]


def _delta(proj3, ab3, conv_buf, s0, p, l):
    n, t, _ = proj3.shape
    tt, c = min(TT_DELTA, t), DELTA_CHUNK
    ns = min(SEQS_DELTA, n)
    kern = functools.partial(_delta_kernel, tt=tt, c=c)
    tile = lambda i, j: (i, j, 0)
    return pl.pallas_call(
        kern,
        grid=(n // ns, t // tt),
        in_specs=[
            pl.BlockSpec((ns, tt, D_CONV), tile),
            pl.BlockSpec((ns, tt, D_QK), lambda i, j: (i, j, D_CONV // D_QK)),
            pl.BlockSpec((ns, tt, LANES), tile),
            pl.BlockSpec((ns, CONV_W - 1, D_CONV), lambda i, j: (i, 0, 0)),
            pl.BlockSpec((ns, N_HEADS, HEAD_D, HEAD_D), lambda i, j: (i, 0, 0, 0)),
            _layer_spec((CONV_W, D_CONV), l, 2),
            _layer_spec((1, LANES), l, 2),
            _layer_spec((1, LANES), l, 2),
            _layer_spec((1, HEAD_D), l, 2),
        ],
        out_specs=[
            pl.BlockSpec((ns, tt, D_QK), tile),
            pl.BlockSpec((ns, CONV_W - 1, D_CONV), lambda i, j: (i, 0, 0)),
            pl.BlockSpec((ns, N_HEADS, HEAD_D, HEAD_D), lambda i, j: (i, 0, 0, 0)),
        ],
        out_shape=[
            jax.ShapeDtypeStruct((n, t, D_QK), BF16),
            jax.ShapeDtypeStruct((n, CONV_W - 1, D_CONV), F32),
            jax.ShapeDtypeStruct((n, N_HEADS, HEAD_D, HEAD_D), F32),
        ],
        scratch_shapes=[
            pltpu.VMEM((ns, N_HEADS, HEAD_D, HEAD_D), F32),
            pltpu.VMEM((ns, SUBLANES, D_CONV), F32),
            pltpu.VMEM((ns, tt, D_CONV), F32),
            pltpu.VMEM((ns, tt, LANES), F32),
            pltpu.VMEM((ns, tt, LANES), F32),
        ],
        compiler_params=_params(("arbitrary", "arbitrary")),
        name="delta",
    )(proj3, proj3, ab3, conv_buf, s0, p["conv_w"], p["a_log"], p["dt_bias"], p["delta_norm_w"])


def _mix_kernel(x_ref, oa_ref, npre_ref, whi_ref, lnw_ref, lnb_ref, ws_ref, bs_ref,
                wa_ref, wb_ref, wo_ref, nw_ref, y_ref, v_ref, uv_ref, gate_ref, ob_ref, *, keep_all_v):
    c = MLP_CHUNK
    tm = x_ref.shape[0]
    n_ch = tm // c
    tn = D_MODEL // n_ch
    gw = 2 * D_MODEL // n_ch
    ri = lax.broadcasted_iota(jnp.int32, (c, c), 0)
    ci = lax.broadcasted_iota(jnp.int32, (c, c), 1)
    gd = D_B // N_GROUPS
    ws = [jnp.where(ri >= ci, ws_ref[g], 0.0).astype(BF16) for g in range(N_GROUPS)]
    h = _rms(x_ref[...], npre_ref[...]).astype(BF16)
    uv_ref[...] = _mm(h, whi_ref[:, :2 * D_B])
    pa = []
    for ch in range(n_ch):
        pa.append(_mm(oa_ref[...], wa_ref[:, ch * tn:(ch + 1) * tn]))
        g0 = 2 * D_B + ch * gw
        gate_ref[:, ch * gw:(ch + 1) * gw] = jax.nn.sigmoid(_mm(h, whi_ref[:, g0:g0 + gw]))
        rows = slice(ch * c, (ch + 1) * c)
        x = uv_ref[rows, :]
        gel = 0.5 * x * (1.0 + lax.erf(x * (2.0 ** -0.5)))
        u = gel[:, :D_B]
        v = gel[:, D_B:]
        vc = v - jnp.mean(v, axis=-1, keepdims=True)
        var = jnp.mean(jnp.square(vc), axis=-1, keepdims=True)
        v = vc * lax.rsqrt(var + 1e-5) * lnw_ref[...] + lnb_ref[...]
        if keep_all_v:
            v_ref[rows, :] = v
        elif ch == n_ch - 1:
            v_ref[0] = v
        for g in range(N_GROUPS):
            cols = slice(g * gd, (g + 1) * gd)
            mixed = _mm(ws[g], v[:, cols]) + bs_ref[:, g:g + 1]
            ob_ref[rows, cols] = (u[:, cols] * mixed).astype(BF16)
    pa = jnp.concatenate(pa, axis=1)
    pb = _mm(ob_ref[...], wb_ref[...])
    merged = gate_ref[:, :D_MODEL] * pa + gate_ref[:, D_MODEL:] * pb
    y = _mm(merged, wo_ref[...])
    y_ref[...] = x_ref[...] + _rms(y, nw_ref[...])


def _mix(oa, x2d, p, l, seq_len):
    m = x2d.shape[0]
    tm = min(TM, m)
    c = MLP_CHUNK
    row = lambda i: (i, 0)
    short = seq_len < c
    if short:
        assert c % seq_len == 0 and tm % c == 0
        ws_key, bs_key = "ws_short", "bs_short"
        v_spec = pl.BlockSpec((tm, D_B), row)
        v_shape = jax.ShapeDtypeStruct((m, D_B), F32)
    else:
        assert seq_len % tm == 0
        ws_key, bs_key = "w_spatial", "bs_t"
        tiles_per_seq = seq_len // tm
        v_spec = pl.BlockSpec((1, c, D_B), lambda i: (i // tiles_per_seq, 0, 0))
        v_shape = jax.ShapeDtypeStruct((m // seq_len, c, D_B), F32)
    resident = dict(pipeline_mode=pl.Buffered(1))
    wspec = pl.BlockSpec((None, D_MODEL, D_MODEL), lambda i: (l, 0, 0), **resident)
    return pl.pallas_call(
        functools.partial(_mix_kernel, keep_all_v=short),
        grid=(m // tm,),
        in_specs=[
            pl.BlockSpec((tm, D_MODEL), row),
            pl.BlockSpec((tm, D_QK), row),
            _layer_spec((1, D_MODEL), l, 1),
            pl.BlockSpec((None, D_MODEL, D_MAIN - AB_OFF), lambda i: (l, 0, 0), **resident),
            _layer_spec((1, D_B), l, 1),
            _layer_spec((1, D_B), l, 1),
            _layer_spec((N_GROUPS, c, c), l, 1),
            _layer_spec((c, LANES), l, 1),
            wspec, wspec, wspec,
            _layer_spec((1, D_MODEL), l, 1),
        ],
        out_specs=[pl.BlockSpec((tm, D_MODEL), row), v_spec],
        out_shape=[jax.ShapeDtypeStruct((m, D_MODEL), F32), v_shape],
        scratch_shapes=[pltpu.VMEM((tm, 2 * D_B), F32), pltpu.VMEM((tm, 2 * D_MODEL), F32),
                        pltpu.VMEM((tm, D_B), BF16)],
        compiler_params=_params(("arbitrary",)),
        name="mix",
    )(x2d, oa, p["norm_pre_mix"], p["w_hi"], p["sgu_ln_w"], p["sgu_ln_b"], p[ws_key], p[bs_key],
      p["w_proj_a"], p["w_proj_b"], p["w_out"], p["norm_post_mix"])


def _ffn_kernel(x_ref, npre_ref, wi_ref, wd_ref, npost_ref, y_ref, act_ref):
    x = x_ref[...]
    h = _rms(x, npre_ref[...]).astype(BF16)
    tf = TF_FFN
    for f in range(D_FF // tf):
        gate = _mm(h, wi_ref[:, f * tf:(f + 1) * tf])
        up = _mm(h, wi_ref[:, D_FF + f * tf:D_FF + (f + 1) * tf])
        act_ref[:, f * tf:(f + 1) * tf] = (gate * jax.nn.sigmoid(gate) * up).astype(BF16)
    y = _mm(act_ref[...], wd_ref[...])
    y_ref[...] = x + _rms(y, npost_ref[...])


def _ffn(x2d, p, l):
    m = x2d.shape[0]
    tm = min(TM_FFN, m)
    resident = dict(pipeline_mode=pl.Buffered(1))
    return pl.pallas_call(
        _ffn_kernel,
        grid=(m // tm,),
        in_specs=[
            pl.BlockSpec((tm, D_MODEL), lambda i: (i, 0)),
            _layer_spec((1, D_MODEL), l, 1),
            pl.BlockSpec((None, D_MODEL, 2 * D_FF), lambda i: (l, 0, 0), **resident),
            pl.BlockSpec((None, D_FF, D_MODEL), lambda i: (l, 0, 0), **resident),
            _layer_spec((1, D_MODEL), l, 1),
        ],
        out_specs=pl.BlockSpec((tm, D_MODEL), lambda i: (i, 0)),
        out_shape=jax.ShapeDtypeStruct((m, D_MODEL), F32),
        scratch_shapes=[pltpu.VMEM((tm, D_FF), BF16)],
        compiler_params=_params(("arbitrary",)),
        name="ffn",
    )(x2d, p["norm_pre_ffn"], p["w_ffn_in"], p["w_ffn_out"], p["norm_post_ffn"])


def _trunk_layer(x, p, l, delta_fn):
    n, t, _ = x.shape
    m = n * t
    x2d = x.reshape(m, D_MODEL)
    proj, ab = _inproj(x2d, p, l)
    o_a, new_buf, s_new = delta_fn(proj.reshape(n, t, AB_OFF), ab.reshape(n, t, LANES))
    x1, v_rows = _mix(o_a.reshape(m, D_QK), x2d, p, l, t)
    x2 = _ffn(x1, p, l)
    return x2.reshape(n, t, D_MODEL), s_new, new_buf, v_rows.reshape(n, -1, D_B)


def _prepare_params(t_short, norm_pre_mix, w_in, conv_w, a_log, dt_bias, delta_norm_w, sgu_ln_w,
                    sgu_ln_b, w_spatial, b_spatial, w_proj_a, w_proj_b, w_out, norm_post_mix,
                    norm_pre_ffn, w_ffn_in, w_ffn_out, norm_post_ffn):
    depth = w_in.shape[0]
    w_lo = w_in[:, :, :AB_OFF].astype(BF16)
    w_hi = w_in[:, :, AB_OFF + 2 * N_HEADS:].astype(BF16)
    w_ab = jnp.pad(w_in[:, :, AB_OFF:AB_OFF + 2 * N_HEADS], ((0, 0), (0, 0), (0, LANES - 2 * N_HEADS)))
    row = lambda v: v.reshape(depth, 1, -1)
    lanes = lambda v: jnp.pad(v, ((0, 0), (0, LANES - v.shape[1]))).reshape(depth, 1, LANES)
    bs_t = jnp.pad(jnp.swapaxes(b_spatial, 1, 2), ((0, 0), (0, 0), (0, LANES - N_GROUPS)))
    rep = MLP_CHUNK // t_short
    idx = jnp.arange(MLP_CHUNK)
    same_block = (idx[:, None] // t_short) == (idx[None, :] // t_short)
    onehot = (idx[:, None] % t_short == jnp.arange(t_short)[None, :]).astype(F32)
    tiled = jnp.einsum("ri,lgij,cj->lgrc", onehot, w_spatial[:, :, :t_short, :t_short], onehot,
                       precision=_HI)
    ws_short = jnp.where(same_block, tiled, 0.0)
    return dict(
        norm_pre_mix=row(norm_pre_mix), w_lo=w_lo, w_hi=w_hi, w_ab=w_ab.astype(BF16),
        conv_w=conv_w, a_log=lanes(a_log), dt_bias=lanes(dt_bias), delta_norm_w=row(delta_norm_w),
        sgu_ln_w=row(sgu_ln_w), sgu_ln_b=row(sgu_ln_b), w_spatial=w_spatial, bs_t=bs_t,
        ws_short=ws_short, bs_short=jnp.tile(bs_t[:, :t_short], (1, rep, 1)),
        w_proj_a=w_proj_a.astype(BF16), w_proj_b=w_proj_b.astype(BF16), w_out=w_out.astype(BF16),
        norm_post_mix=row(norm_post_mix), norm_pre_ffn=row(norm_pre_ffn),
        w_ffn_in=w_ffn_in.astype(BF16), w_ffn_out=w_ffn_out.astype(BF16),
        norm_post_ffn=row(norm_post_ffn))


def kernel(x_prompt, x_sample, state_delta, state_conv, norm_pre_mix, w_in, conv_w, a_log, dt_bias,
           delta_norm_w, sgu_ln_w, sgu_ln_b, w_spatial, b_spatial, w_proj_a, w_proj_b, w_out,
           norm_post_mix, norm_pre_ffn, w_ffn_in, w_ffn_out, norm_post_ffn):
    depth = w_in.shape[0]
    nb, seq, _ = x_prompt.shape
    ndec, dec_seq, _ = x_sample.shape
    assert seq % DELTA_CHUNK == 0 and seq % MLP_CHUNK == 0
    assert dec_seq % SUBLANES == 0 and dec_seq < DELTA_CHUNK and ndec % DEC_SEQS_PER_STEP == 0
    p = _prepare_params(dec_seq, norm_pre_mix, w_in, conv_w, a_log, dt_bias, delta_norm_w, sgu_ln_w,
                        sgu_ln_b, w_spatial, b_spatial, w_proj_a, w_proj_b, w_out, norm_post_mix,
                        norm_pre_ffn, w_ffn_in, w_ffn_out, norm_post_ffn)
    y_p, y_s = x_prompt, x_sample
    conv0 = jnp.zeros((nb, CONV_W - 1, D_CONV), x_prompt.dtype)
    s_zero = jnp.zeros((nb, N_HEADS, HEAD_D, HEAD_D), state_delta.dtype)
    sd_p, sc_p, cv_p, sc_s, cv_s = [], [], [], [], []
    sd_s = None
    for l in range(depth):
        y_p, s_new, buf_new, v_rows = _trunk_layer(
            y_p, p, l, lambda proj3, ab3: _delta(proj3, ab3, conv0, s_zero, p, l))
        sd_p.append(s_new)
        sc_p.append(buf_new)
        cv_p.append(v_rows)
        y_s, sd_s, buf_new, v_rows = _trunk_layer(
            y_s, p, l, lambda proj3, ab3: _delta_dec(proj3, ab3, state_conv, state_delta, p, l, sd_s))
        sc_s.append(buf_new)
        cv_s.append(v_rows)
    return (y_p, y_s, jnp.stack(sd_p), jnp.stack(sc_p), jnp.stack(cv_p),
            sd_s, jnp.stack(sc_s), jnp.stack(cv_s))
```

```python
import functools

import jax
import jax.numpy as jnp
from jax import lax
from jax.experimental import pallas as pl
from jax.experimental.pallas import tpu as pltpu

F32 = jnp.float32
BF16 = jnp.bfloat16

D_MODEL = 1024
N_HEADS = 8
HEAD_D = 128
D_QK = N_HEADS * HEAD_D
D_CONV = 3 * D_QK
CONV_W = 4
DELTA_CHUNK = 64
MLP_CHUNK = 128
N_GROUPS = 8
D_B = 1024
D_FF = 2816
D_MAIN = D_CONV + D_QK + 2 * D_B + 2 * D_MODEL
AB_OFF = D_CONV + D_QK
LANES = 128
SUBLANES = 8

TM = 512
TM_FFN = 1024
INPROJ_SUB = 512
TF_FFN = 256
TT_DELTA = 128
SEQS_DELTA = 4
DEC_SEQS_PER_STEP = 8
VMEM_LIMIT = 48 * 1024 * 1024

_HI = lax.Precision.HIGHEST
_NT = (((1,), (1,)), ((), ()))
_TN = (((0,), (0,)), ((), ()))
_NN = (((1,), (0,)), ((), ()))


def _mm(a, b, dims=_NN):
    return lax.dot_general(a.astype(BF16), b.astype(BF16), dims, preferred_element_type=F32)


def _mm_hi(a, b, dims=_NN):
    return lax.dot_general(a, b, dims, precision=_HI, preferred_element_type=F32)


def _rms(x, w, eps=1e-6):
    return x * lax.rsqrt(jnp.mean(jnp.square(x), axis=-1, keepdims=True) + eps) * w


def _params(sem):
    return pltpu.CompilerParams(dimension_semantics=sem, vmem_limit_bytes=VMEM_LIMIT)


def _layer_spec(shape, l, ngrid):
    zeros = (0,) * len(shape)
    if ngrid == 1:
        return pl.BlockSpec((None,) + shape, lambda i: (l,) + zeros)
    return pl.BlockSpec((None,) + shape, lambda i, j: (l,) + zeros)


def _inproj_kernel(x_ref, nw_ref, wlo_ref, wab_ref, o_ref, ab_ref):
    h = _rms(x_ref[...], nw_ref[...]).astype(BF16)
    ab_ref[...] = _mm(h, wab_ref[...])
    o_ref[...] = _mm(h, wlo_ref[...])


def _inproj(x2d, p, l):
    m = x2d.shape[0]
    tm = min(TM, m)
    return pl.pallas_call(
        _inproj_kernel,
        grid=(m // tm,),
        in_specs=[
            pl.BlockSpec((tm, D_MODEL), lambda i: (i, 0)),
            _layer_spec((1, D_MODEL), l, 1),
            pl.BlockSpec((None, D_MODEL, AB_OFF), lambda i: (l, 0, 0), pipeline_mode=pl.Buffered(1)),
            _layer_spec((D_MODEL, LANES), l, 1),
        ],
        out_specs=[
            pl.BlockSpec((tm, AB_OFF), lambda i: (i, 0)),
            pl.BlockSpec((tm, LANES), lambda i: (i, 0)),
        ],
        out_shape=[
            jax.ShapeDtypeStruct((m, AB_OFF), F32),
            jax.ShapeDtypeStruct((m, LANES), F32),
        ],
        compiler_params=_params(("arbitrary",)),
        name="inproj",
    )(x2d, p["norm_pre_mix"], p["w_lo"], p["w_ab"])


def _inproj_conv_kernel(x_ref, nw_ref, wlo_ref, wab_ref, cw_ref, cbuf_ref,
                        o_ref, ab_ref, nbuf_ref, h_ref, carry_ref, raw_ref, *, tiles_per_seq):
    i = pl.program_id(0)
    tm = o_ref.shape[0]
    hist = CONV_W - 1
    sub_w = raw_ref.shape[2]
    n_sub = AB_OFF // sub_w

    @pl.when(i % tiles_per_seq == 0)
    def _():
        carry_ref[...] = jnp.zeros(carry_ref.shape, F32)
        carry_ref[SUBLANES - hist:, :] = cbuf_ref[0]

    h_ref[...] = _rms(x_ref[...], nw_ref[...]).astype(BF16)
    ab_ref[...] = _mm(h_ref[...], wab_ref[...])

    def mm(s):
        raw_ref[s % 2] = _mm(h_ref[...], wlo_ref[:, s * sub_w:(s + 1) * sub_w])

    def epilogue(s):
        for k in range(sub_w // HEAD_D):
            c0 = s * sub_w + k * HEAD_D
            cols = slice(c0, c0 + HEAD_D)
            raw = raw_ref[s % 2, :, k * HEAD_D:(k + 1) * HEAD_D]
            if c0 < D_CONV:
                xg = jnp.concatenate([carry_ref[:, cols], raw], axis=0)
                xg = xg.reshape(tm // SUBLANES + 1, SUBLANES, HEAD_D)
                y = _causal_conv(xg, lambda t: cw_ref[t:t + 1, cols])
                o_ref[:, cols] = _conv_silu_norm(y, c0 // D_QK).reshape(tm, HEAD_D)
                carry_ref[:, cols] = raw[tm - SUBLANES:, :]
                nbuf_ref[0, :, cols] = raw[tm - hist:, :]
            else:
                o_ref[:, cols] = raw * jax.nn.sigmoid(raw)

    mm(0)
    for s in range(n_sub):
        if s + 1 < n_sub:
            mm(s + 1)
        epilogue(s)


def _inproj_conv(x2d, conv_buf, p, l, seq_len):
    m = x2d.shape[0]
    tm = min(TM, m)
    assert seq_len % tm == 0
    tiles_per_seq = seq_len // tm
    seq_blk = lambda i: (i // tiles_per_seq, 0, 0)
    return pl.pallas_call(
        functools.partial(_inproj_conv_kernel, tiles_per_seq=tiles_per_seq),
        grid=(m // tm,),
        in_specs=[
            pl.BlockSpec((tm, D_MODEL), lambda i: (i, 0)),
            _layer_spec((1, D_MODEL), l, 1),
            pl.BlockSpec((None, D_MODEL, AB_OFF), lambda i: (l, 0, 0), pipeline_mode=pl.Buffered(1)),
            _layer_spec((D_MODEL, LANES), l, 1),
            _layer_spec((CONV_W, D_CONV), l, 1),
            pl.BlockSpec((1, CONV_W - 1, D_CONV), seq_blk),
        ],
        out_specs=[
            pl.BlockSpec((tm, AB_OFF), lambda i: (i, 0)),
            pl.BlockSpec((tm, LANES), lambda i: (i, 0)),
            pl.BlockSpec((1, CONV_W - 1, D_CONV), seq_blk),
        ],
        out_shape=[
            jax.ShapeDtypeStruct((m, AB_OFF), F32),
            jax.ShapeDtypeStruct((m, LANES), F32),
            jax.ShapeDtypeStruct((m // seq_len, CONV_W - 1, D_CONV), F32),
        ],
        scratch_shapes=[pltpu.VMEM((tm, D_MODEL), BF16), pltpu.VMEM((SUBLANES, D_CONV), F32),
                        pltpu.VMEM((2, tm, INPROJ_SUB), F32)],
        compiler_params=_params(("arbitrary",)),
        name="inproj_conv",
    )(x2d, p["norm_pre_mix"], p["w_lo"], p["w_ab"], p["conv_w"], conv_buf)


def _conv_silu_norm(y, part):
    y = y * jax.nn.sigmoid(y)
    if part < 2:
        inv = lax.rsqrt(jnp.sum(jnp.square(y), axis=-1, keepdims=True) + 1e-6)
        if part == 0:
            inv = inv * (HEAD_D ** -0.5)
        y = y * inv
    return y


def _causal_conv(xg, w_of_tap):
    sub = lax.broadcasted_iota(jnp.int32, (xg.shape[0] - 1,) + xg.shape[1:], 1)
    y = None
    for i in range(CONV_W):
        k = CONV_W - 1 - i
        if k == 0:
            tap = xg[1:]
        else:
            rot = pltpu.roll(xg, k, axis=1)
            tap = jnp.where(sub >= k, rot[1:], rot[:-1])
        term = w_of_tap(i) * tap
        y = term if y is None else y + term
    return y


def _delta_dec_kernel(qkv_ref, z_ref, ab_ref, cbuf_ref, s0_ref, cw_ref, alog_ref, dtb_ref, nw_ref,
                      *rest, nb, c, aliased):
    if aliased:
        rest = rest[1:]
    o_ref, nbuf_ref, snew_ref, xbuf_ref, qkvc_ref = rest
    pad = SUBLANES
    hist = CONV_W - 1
    xbuf_ref[:, pad - hist:pad, :] = cbuf_ref[...]
    xbuf_ref[:, pad:pad + c, :] = qkv_ref[...]
    for part in range(3):
        for h in range(N_HEADS):
            c0 = part * D_QK + h * HEAD_D
            cols = slice(c0, c0 + HEAD_D)
            y = cw_ref[0:1, cols] * xbuf_ref[:, pad - hist:pad - hist + c, cols]
            for i in range(1, CONV_W):
                y = y + cw_ref[i:i + 1, cols] * xbuf_ref[:, pad - hist + i:pad - hist + i + c, cols]
            qkvc_ref[:, :, cols] = _conv_silu_norm(y, part)
    nbuf_ref[...] = xbuf_ref[:, pad + c - hist:pad + c, :]

    ri = lax.broadcasted_iota(jnp.int32, (c, c), 0)
    ci = lax.broadcasted_iota(jnp.int32, (c, c), 1)
    causal = ri >= ci
    strict = ri > ci
    diag = ri == ci
    tri = jnp.where(causal, 1.0, 0.0).astype(F32)
    eye = jnp.where(diag, 1.0, 0.0).astype(F32)
    n_levels = c.bit_length() - 2
    items = [(i, h) for i in range(nb) for h in range(N_HEADS)]
    idx = {it: n for n, it in enumerate(items)}
    every = range(len(items))

    gc, beta = [], []
    for i in range(nb):
        g = -jnp.exp(alog_ref[...]) * jax.nn.softplus(ab_ref[i] + dtb_ref[...])
        gc.append(_mm_hi(tri, g))
        beta.append(jax.nn.sigmoid(ab_ref[i]))

    def col(part, i, h):
        return qkvc_ref[i, :, part * D_QK + h * HEAD_D:part * D_QK + (h + 1) * HEAD_D]

    gcol = [gc[i][:, h:h + 1] for i, h in items]
    bcol = [beta[i][:, N_HEADS + h:N_HEADS + h + 1] for i, h in items]
    kb = [col(1, i, h) * bcol[idx[i, h]] for i, h in items]
    kq = [_mm(jnp.concatenate([kb[idx[i, h]], col(0, i, h)], axis=0), col(1, i, h), _NT)
          for i, h in items]
    x, qk, tk = [], [], []
    for n in every:
        grow = jnp.sum(jnp.where(diag, gcol[n], 0.0), axis=0, keepdims=True)
        dec = jnp.where(causal, jnp.exp(jnp.where(causal, gcol[n] - grow, 0.0)), 0.0)
        x.append(jnp.where(strict, -(kq[n][:c] * dec), 0.0))
        qk.append(kq[n][c:] * dec)
        tk.append(eye + x[n])
    pk = [_mm(x[n], x[n]) for n in every]
    for lvl in range(1, n_levels + 1):
        if lvl < n_levels:
            r = [_mm(jnp.concatenate([pk[n], tk[n]], axis=0), pk[n]) for n in every]
            pk = [r[n][:c] for n in every]
            tk = [tk[n] + r[n][c:] for n in every]
        else:
            tk = [tk[n] + _mm(tk[n], pk[n]) for n in every]
    e = [jnp.exp(gcol[n]) for n in every]
    sol = [_mm(tk[idx[i, h]], jnp.concatenate([col(2, i, h) * bcol[idx[i, h]],
                                                kb[idx[i, h]] * e[idx[i, h]]], axis=1))
           for i, h in items]
    wq = [_mm(jnp.concatenate([sol[idx[i, h]][:, HEAD_D:], col(0, i, h) * e[idx[i, h]]], axis=0),
              s0_ref[i, h]) for i, h in items]
    vn = [sol[n][:, :HEAD_D] - wq[n][:c] for n in every]
    op = [_mm(qk[n], vn[n]) for n in every]
    for i, h in items:
        n = idx[i, h]
        glast = gc[i][c - 1:c, h:h + 1]
        kdec = col(1, i, h) * jnp.exp(glast - gcol[n])
        snew_ref[i, h] = s0_ref[i, h] * jnp.exp(glast) + _mm(kdec, vn[n], _TN)
    for i, h in items:
        n = idx[i, h]
        o = _rms(wq[n][c:] + op[n], nw_ref[...])
        cols = slice(h * HEAD_D, (h + 1) * HEAD_D)
        zz = z_ref[i, :, cols]
        o_ref[i, :, cols] = o * (zz * jax.nn.sigmoid(zz))


def _delta_dec(proj3, ab3, conv_all, s_all, p, l, s_out_prev):
    n, c, _ = proj3.shape
    depth = s_all.shape[0]
    nb = DEC_SEQS_PER_STEP
    aliased = s_out_prev is not None
    kern = functools.partial(_delta_dec_kernel, nb=nb, c=c, aliased=aliased)
    state_blk = (None, nb, N_HEADS, HEAD_D, HEAD_D)
    in_specs = [
        pl.BlockSpec((nb, c, D_CONV), lambda i: (i, 0, 0)),
        pl.BlockSpec((nb, c, D_QK), lambda i: (i, 0, D_CONV // D_QK)),
        pl.BlockSpec((nb, c, LANES), lambda i: (i, 0, 0)),
        pl.BlockSpec((None, nb, CONV_W - 1, D_CONV), lambda i: (l, i, 0, 0)),
        pl.BlockSpec(state_blk, lambda i: (l, i, 0, 0, 0)),
        _layer_spec((CONV_W, D_CONV), l, 1),
        _layer_spec((1, LANES), l, 1),
        _layer_spec((1, LANES), l, 1),
        _layer_spec((1, HEAD_D), l, 1),
    ]
    args = [proj3, proj3, ab3, conv_all, s_all, p["conv_w"], p["a_log"], p["dt_bias"], p["delta_norm_w"]]
    aliases = {}
    if aliased:
        in_specs.append(pl.BlockSpec(memory_space=pl.ANY))
        args.append(s_out_prev)
        aliases = {len(args) - 1: 2}
    return pl.pallas_call(
        kern,
        grid=(n // nb,),
        in_specs=in_specs,
        out_specs=[
            pl.BlockSpec((nb, c, D_QK), lambda i: (i, 0, 0)),
            pl.BlockSpec((nb, CONV_W - 1, D_CONV), lambda i: (i, 0, 0)),
            pl.BlockSpec(state_blk, lambda i: (l, i, 0, 0, 0)),
        ],
        out_shape=[
            jax.ShapeDtypeStruct((n, c, D_QK), F32),
            jax.ShapeDtypeStruct((n, CONV_W - 1, D_CONV), F32),
            jax.ShapeDtypeStruct((depth, n, N_HEADS, HEAD_D, HEAD_D), F32),
        ],
        scratch_shapes=[
            pltpu.VMEM((nb, c + SUBLANES, D_CONV), F32),
            pltpu.VMEM((nb, c, D_CONV), F32),
        ],
        input_output_aliases=aliases,
        compiler_params=_params(("arbitrary",)),
        name="delta_dec",
    )(*args)


def _delta_chunk(refs, ic, c):
    qkvc_ref, gc_ref, beta_ref, s_ref, z_ref, nw_ref, o_ref = refs
    assert 2 * c == LANES
    ns = gc_ref.shape[0]
    ri = lax.broadcasted_iota(jnp.int32, (c, LANES), 0)
    lane = lax.broadcasted_iota(jnp.int32, (c, LANES), 1)
    half = lane >= c
    cj = jnp.where(half, lane - c, lane)
    causal = ri >= cj
    strict = ri > cj
    diag = ri == cj
    eye = jnp.where(diag, 1.0, 0.0).astype(F32)
    zeros = jnp.zeros((c, HEAD_D), F32)
    zeros2 = jnp.zeros((c, 2 * HEAD_D), F32)
    n_levels = c.bit_length() - 2

    def blockdiag(m):
        return jnp.concatenate([jnp.where(half, 0.0, m), jnp.where(half, m, 0.0)], axis=0)

    def side_by_side(a, b):
        return jnp.concatenate([jnp.concatenate([a, zeros], axis=1),
                                jnp.concatenate([zeros, b], axis=1)], axis=0)

    rows = pl.ds(pl.multiple_of(ic * c, c), c)
    heads = [(s, h) for s in range(ns) for h in range(N_HEADS)]
    pairs = [(s, pr) for s in range(ns) for pr in range(N_HEADS // 2)]
    gc = [gc_ref[s, rows, :] for s in range(ns)]
    beta = [beta_ref[s, rows, :] for s in range(ns)]

    def col(part, s, h):
        return qkvc_ref[s, rows, part * D_QK + h * HEAD_D:part * D_QK + (h + 1) * HEAD_D]

    g = {(s, h): jnp.broadcast_to(gc[s][:, h:h + 1], (c, HEAD_D)) for s, h in heads}
    b = {(s, h): jnp.broadcast_to(beta[s][:, N_HEADS + h:N_HEADS + h + 1], (c, HEAD_D))
         for s, h in heads}
    kb = {(s, h): col(1, s, h) * b[s, h] for s, h in heads}
    kq = {}
    for s, pr in pairs:
        h1, h2 = 2 * pr, 2 * pr + 1
        lhs = jnp.concatenate([jnp.concatenate([kb[s, h1], kb[s, h2]], axis=1),
                               jnp.concatenate([col(0, s, h1), col(0, s, h2)], axis=1)], axis=0)
        kq[s, pr] = _mm(lhs, side_by_side(col(1, s, h1), col(1, s, h2)), _NT)
    x, qkd, tk = {}, {}, {}
    for s, pr in pairs:
        gcp = jnp.where(half, g[s, 2 * pr + 1], g[s, 2 * pr])
        rowp = jnp.sum(jnp.where(diag, gcp, 0.0), axis=0, keepdims=True)
        dec = jnp.where(causal, jnp.exp(jnp.where(causal, gcp - rowp, 0.0)), 0.0)
        x[s, pr] = jnp.where(strict, -(kq[s, pr][:c] * dec), 0.0)
        qkd[s, pr] = kq[s, pr][c:] * dec
        tk[s, pr] = eye + x[s, pr]
    pk = {sp: _mm(x[sp], blockdiag(x[sp])) for sp in pairs}
    for lvl in range(1, n_levels + 1):
        if lvl < n_levels:
            r = {sp: _mm(jnp.concatenate([pk[sp], tk[sp]], axis=0), blockdiag(pk[sp])) for sp in pairs}
            pk = {sp: r[sp][:c] for sp in pairs}
            tk = {sp: tk[sp] + r[sp][c:] for sp in pairs}
        else:
            tk = {sp: tk[sp] + _mm(tk[sp], blockdiag(pk[sp])) for sp in pairs}
    e = {sh: jnp.exp(g[sh]) for sh in heads}
    sol = {}
    for s, h in heads:
        rhs = jnp.concatenate([col(2, s, h) * b[s, h], kb[s, h] * e[s, h]], axis=1)
        rhs = jnp.concatenate([rhs, zeros2] if h % 2 == 0 else [zeros2, rhs], axis=0)
        sol[s, h] = _mm(tk[s, h // 2], rhs)
    wq = {(s, h): _mm(jnp.concatenate([sol[s, h][:, HEAD_D:], col(0, s, h) * e[s, h]], axis=0),
                      s_ref[s, h]) for s, h in heads}
    vn = {sh: sol[sh][:, :HEAD_D] - wq[sh][:c] for sh in heads}
    op = {(s, pr): _mm(qkd[s, pr], side_by_side(vn[s, 2 * pr], vn[s, 2 * pr + 1]))
          for s, pr in pairs}
    for s, h in heads:
        glast = gc[s][c - 1:c, h:h + 1]
        kdec = col(1, s, h) * jnp.exp(glast - g[s, h])
        s_ref[s, h] = s_ref[s, h] * jnp.exp(glast) + _mm(kdec, vn[s, h], _TN)
    for s, h in heads:
        o = _rms(wq[s, h][c:] + op[s, h // 2][:, (h % 2) * HEAD_D:(h % 2 + 1) * HEAD_D], nw_ref[...])
        cols = slice(h * HEAD_D, (h + 1) * HEAD_D)
        o_ref[s, rows, cols] = (o * z_ref[s, rows, cols]).astype(o_ref.dtype)


def _delta_kernel(qkvc_ref, z_ref, ab_ref, s0_ref, alog_ref, dtb_ref, nw_ref,
                  o_ref, snew_ref, s_ref, gc_ref, beta_ref, *, tt, c):
    t = pl.program_id(1)
    nt = pl.num_programs(1)
    ns = qkvc_ref.shape[0]

    @pl.when(t == 0)
    def _():
        s_ref[...] = s0_ref[...]

    rt = lax.broadcasted_iota(jnp.int32, (c, c), 0)
    ct = lax.broadcasted_iota(jnp.int32, (c, c), 1)
    tri = jnp.where(rt >= ct, 1.0, 0.0).astype(F32)
    for s in range(ns):
        ab = ab_ref[s]
        g = -jnp.exp(alog_ref[...]) * jax.nn.softplus(ab + dtb_ref[...])
        beta_ref[s] = jax.nn.sigmoid(ab)
        for ic in range(tt // c):
            gc_ref[s, ic * c:(ic + 1) * c, :] = _mm_hi(tri, g[ic * c:(ic + 1) * c, :])

    refs = (qkvc_ref, gc_ref, beta_ref, s_ref, z_ref, nw_ref, o_ref)

    def body(ic, carry):
        _delta_chunk(refs, ic, c)
        return carry

    lax.fori_loop(0, tt // c, body, 0)

    @pl.when(t == nt - 1)
    def _():
        snew_ref[...] = s_ref[...]


def _delta(proj3, ab3, s0, p, l):
    n, t, _ = proj3.shape
    tt, c = min(TT_DELTA, t), DELTA_CHUNK
    ns = min(SEQS_DELTA, n)
    kern = functools.partial(_delta_kernel, tt=tt, c=c)
    tile = lambda i, j: (i, j, 0)
    return pl.pallas_call(
        kern,
        grid=(n // ns, t // tt),
        in_specs=[
            pl.BlockSpec((ns, tt, D_CONV), tile),
            pl.BlockSpec((ns, tt, D_QK), lambda i, j: (i, j, D_CONV // D_QK)),
            pl.BlockSpec((ns, tt, LANES), tile),
            pl.BlockSpec((ns, N_HEADS, HEAD_D, HEAD_D), lambda i, j: (i, 0, 0, 0)),
            _layer_spec((1, LANES), l, 2),
            _layer_spec((1, LANES), l, 2),
            _layer_spec((1, HEAD_D), l, 2),
        ],
        out_specs=[
            pl.BlockSpec((ns, tt, D_QK), tile),
            pl.BlockSpec((ns, N_HEADS, HEAD_D, HEAD_D), lambda i, j: (i, 0, 0, 0)),
        ],
        out_shape=[
            jax.ShapeDtypeStruct((n, t, D_QK), BF16),
            jax.ShapeDtypeStruct((n, N_HEADS, HEAD_D, HEAD_D), F32),
        ],
        scratch_shapes=[
            pltpu.VMEM((ns, N_HEADS, HEAD_D, HEAD_D), F32),
            pltpu.VMEM((ns, tt, LANES), F32),
            pltpu.VMEM((ns, tt, LANES), F32),
        ],
        compiler_params=_params(("arbitrary", "arbitrary")),
        name="delta",
    )(proj3, proj3, ab3, s0, p["a_log"], p["dt_bias"], p["delta_norm_w"])


def _mix_kernel(x_ref, oa_ref, npre_ref, whi_ref, lnw_ref, lnb_ref, ws_ref, bs_ref,
                wa_ref, wb_ref, wo_ref, nw_ref, y_ref, v_ref, uv_ref, gate_ref, ob_ref, *, keep_all_v):
    c = MLP_CHUNK
    tm = x_ref.shape[0]
    n_ch = tm // c
    tn = D_MODEL // n_ch
    gw = 2 * D_MODEL // n_ch
    ri = lax.broadcasted_iota(jnp.int32, (c, c), 0)
    ci = lax.broadcasted_iota(jnp.int32, (c, c), 1)
    gd = D_B // N_GROUPS
    ws = [jnp.where(ri >= ci, ws_ref[g], 0.0).astype(BF16) for g in range(N_GROUPS)]
    h = _rms(x_ref[...], npre_ref[...]).astype(BF16)
    uv_ref[...] = _mm(h, whi_ref[:, :2 * D_B])
    pa = []
    for ch in range(n_ch):
        pa.append(_mm(oa_ref[...], wa_ref[:, ch * tn:(ch + 1) * tn]))
        g0 = 2 * D_B + ch * gw
        gate_ref[:, ch * gw:(ch + 1) * gw] = jax.nn.sigmoid(_mm(h, whi_ref[:, g0:g0 + gw]))
        rows = slice(ch * c, (ch + 1) * c)
        x = uv_ref[rows, :]
        gel = 0.5 * x * (1.0 + lax.erf(x * (2.0 ** -0.5)))
        u = gel[:, :D_B]
        v = gel[:, D_B:]
        vc = v - jnp.mean(v, axis=-1, keepdims=True)
        var = jnp.mean(jnp.square(vc), axis=-1, keepdims=True)
        v = vc * lax.rsqrt(var + 1e-5) * lnw_ref[...] + lnb_ref[...]
        if keep_all_v:
            v_ref[rows, :] = v
        elif ch == n_ch - 1:
            v_ref[0] = v
        for g in range(N_GROUPS):
            cols = slice(g * gd, (g + 1) * gd)
            mixed = _mm(ws[g], v[:, cols]) + bs_ref[:, g:g + 1]
            ob_ref[rows, cols] = (u[:, cols] * mixed).astype(BF16)
    pa = jnp.concatenate(pa, axis=1)
    pb = _mm(ob_ref[...], wb_ref[...])
    merged = gate_ref[:, :D_MODEL] * pa + gate_ref[:, D_MODEL:] * pb
    y = _mm(merged, wo_ref[...])
    y_ref[...] = x_ref[...] + _rms(y, nw_ref[...])


def _mix(oa, x2d, p, l, seq_len):
    m = x2d.shape[0]
    tm = min(TM, m)
    c = MLP_CHUNK
    row = lambda i: (i, 0)
    short = seq_len < c
    if short:
        assert c % seq_len == 0 and tm % c == 0
        ws_key, bs_key = "ws_short", "bs_short"
        v_spec = pl.BlockSpec((tm, D_B), row)
        v_shape = jax.ShapeDtypeStruct((m, D_B), F32)
    else:
        assert seq_len % tm == 0
        ws_key, bs_key = "w_spatial", "bs_t"
        tiles_per_seq = seq_len // tm
        v_spec = pl.BlockSpec((1, c, D_B), lambda i: (i // tiles_per_seq, 0, 0))
        v_shape = jax.ShapeDtypeStruct((m // seq_len, c, D_B), F32)
    resident = dict(pipeline_mode=pl.Buffered(1))
    wspec = pl.BlockSpec((None, D_MODEL, D_MODEL), lambda i: (l, 0, 0), **resident)
    return pl.pallas_call(
        functools.partial(_mix_kernel, keep_all_v=short),
        grid=(m // tm,),
        in_specs=[
            pl.BlockSpec((tm, D_MODEL), row),
            pl.BlockSpec((tm, D_QK), row),
            _layer_spec((1, D_MODEL), l, 1),
            pl.BlockSpec((None, D_MODEL, D_MAIN - AB_OFF), lambda i: (l, 0, 0), **resident),
            _layer_spec((1, D_B), l, 1),
            _layer_spec((1, D_B), l, 1),
            _layer_spec((N_GROUPS, c, c), l, 1),
            _layer_spec((c, LANES), l, 1),
            wspec, wspec, wspec,
            _layer_spec((1, D_MODEL), l, 1),
        ],
        out_specs=[pl.BlockSpec((tm, D_MODEL), row), v_spec],
        out_shape=[jax.ShapeDtypeStruct((m, D_MODEL), F32), v_shape],
        scratch_shapes=[pltpu.VMEM((tm, 2 * D_B), F32), pltpu.VMEM((tm, 2 * D_MODEL), F32),
                        pltpu.VMEM((tm, D_B), BF16)],
        compiler_params=_params(("arbitrary",)),
        name="mix",
    )(x2d, oa, p["norm_pre_mix"], p["w_hi"], p["sgu_ln_w"], p["sgu_ln_b"], p[ws_key], p[bs_key],
      p["w_proj_a"], p["w_proj_b"], p["w_out"], p["norm_post_mix"])


def _ffn_kernel(x_ref, npre_ref, wi_ref, wd_ref, npost_ref, y_ref, act_ref):
    x = x_ref[...]
    h = _rms(x, npre_ref[...]).astype(BF16)
    tf = TF_FFN
    for f in range(D_FF // tf):
        gate = _mm(h, wi_ref[:, f * tf:(f + 1) * tf])
        up = _mm(h, wi_ref[:, D_FF + f * tf:D_FF + (f + 1) * tf])
        act_ref[:, f * tf:(f + 1) * tf] = (gate * jax.nn.sigmoid(gate) * up).astype(BF16)
    y = _mm(act_ref[...], wd_ref[...])
    y_ref[...] = x + _rms(y, npost_ref[...])


def _ffn(x2d, p, l):
    m = x2d.shape[0]
    tm = min(TM_FFN, m)
    resident = dict(pipeline_mode=pl.Buffered(1))
    return pl.pallas_call(
        _ffn_kernel,
        grid=(m // tm,),
        in_specs=[
            pl.BlockSpec((tm, D_MODEL), lambda i: (i, 0)),
            _layer_spec((1, D_MODEL), l, 1),
            pl.BlockSpec((None, D_MODEL, 2 * D_FF), lambda i: (l, 0, 0), **resident),
            pl.BlockSpec((None, D_FF, D_MODEL), lambda i: (l, 0, 0), **resident),
            _layer_spec((1, D_MODEL), l, 1),
        ],
        out_specs=pl.BlockSpec((tm, D_MODEL), lambda i: (i, 0)),
        out_shape=jax.ShapeDtypeStruct((m, D_MODEL), F32),
        scratch_shapes=[pltpu.VMEM((tm, D_FF), BF16)],
        compiler_params=_params(("arbitrary",)),
        name="ffn",
    )(x2d, p["norm_pre_ffn"], p["w_ffn_in"], p["w_ffn_out"], p["norm_post_ffn"])


def _front_long(x2d, n, t, conv_buf, s0, p, l):
    proj, ab, new_buf = _inproj_conv(x2d, conv_buf, p, l, t)
    o_a, s_new = _delta(proj.reshape(n, t, AB_OFF), ab.reshape(n, t, LANES), s0, p, l)
    return o_a, new_buf, s_new


def _front_short(x2d, n, t, conv_all, s_all, p, l, s_out_prev):
    proj, ab = _inproj(x2d, p, l)
    return _delta_dec(proj.reshape(n, t, AB_OFF), ab.reshape(n, t, LANES), conv_all, s_all, p, l,
                      s_out_prev)


def _trunk_layer(x, p, l, front):
    n, t, _ = x.shape
    m = n * t
    x2d = x.reshape(m, D_MODEL)
    o_a, new_buf, s_new = front(x2d, n, t)
    x1, v_rows = _mix(o_a.reshape(m, D_QK), x2d, p, l, t)
    x2 = _ffn(x1, p, l)
    return x2.reshape(n, t, D_MODEL), s_new, new_buf, v_rows.reshape(n, -1, D_B)


def _prepare_params(t_short, norm_pre_mix, w_in, conv_w, a_log, dt_bias, delta_norm_w, sgu_ln_w,
                    sgu_ln_b, w_spatial, b_spatial, w_proj_a, w_proj_b, w_out, norm_post_mix,
                    norm_pre_ffn, w_ffn_in, w_ffn_out, norm_post_ffn):
    depth = w_in.shape[0]
    w_lo = w_in[:, :, :AB_OFF].astype(BF16)
    w_hi = w_in[:, :, AB_OFF + 2 * N_HEADS:].astype(BF16)
    w_ab = jnp.pad(w_in[:, :, AB_OFF:AB_OFF + 2 * N_HEADS], ((0, 0), (0, 0), (0, LANES - 2 * N_HEADS)))
    row = lambda v: v.reshape(depth, 1, -1)
    lanes = lambda v: jnp.pad(v, ((0, 0), (0, LANES - v.shape[1]))).reshape(depth, 1, LANES)
    bs_t = jnp.pad(jnp.swapaxes(b_spatial, 1, 2), ((0, 0), (0, 0), (0, LANES - N_GROUPS)))
    rep = MLP_CHUNK // t_short
    idx = jnp.arange(MLP_CHUNK)
    same_block = (idx[:, None] // t_short) == (idx[None, :] // t_short)
    onehot = (idx[:, None] % t_short == jnp.arange(t_short)[None, :]).astype(F32)
    tiled = jnp.einsum("ri,lgij,cj->lgrc", onehot, w_spatial[:, :, :t_short, :t_short], onehot,
                       precision=_HI)
    ws_short = jnp.where(same_block, tiled, 0.0)
    return dict(
        norm_pre_mix=row(norm_pre_mix), w_lo=w_lo, w_hi=w_hi, w_ab=w_ab.astype(BF16),
        conv_w=conv_w, a_log=lanes(a_log), dt_bias=lanes(dt_bias), delta_norm_w=row(delta_norm_w),
        sgu_ln_w=row(sgu_ln_w), sgu_ln_b=row(sgu_ln_b), w_spatial=w_spatial, bs_t=bs_t,
        ws_short=ws_short, bs_short=jnp.tile(bs_t[:, :t_short], (1, rep, 1)),
        w_proj_a=w_proj_a.astype(BF16), w_proj_b=w_proj_b.astype(BF16), w_out=w_out.astype(BF16),
        norm_post_mix=row(norm_post_mix), norm_pre_ffn=row(norm_pre_ffn),
        w_ffn_in=w_ffn_in.astype(BF16), w_ffn_out=w_ffn_out.astype(BF16),
        norm_post_ffn=row(norm_post_ffn))


def kernel(x_prompt, x_sample, state_delta, state_conv, norm_pre_mix, w_in, conv_w, a_log, dt_bias,
           delta_norm_w, sgu_ln_w, sgu_ln_b, w_spatial, b_spatial, w_proj_a, w_proj_b, w_out,
           norm_post_mix, norm_pre_ffn, w_ffn_in, w_ffn_out, norm_post_ffn):
    depth = w_in.shape[0]
    nb, seq, _ = x_prompt.shape
    ndec, dec_seq, _ = x_sample.shape
    assert seq % DELTA_CHUNK == 0 and seq % MLP_CHUNK == 0
    assert dec_seq % SUBLANES == 0 and dec_seq < DELTA_CHUNK and ndec % DEC_SEQS_PER_STEP == 0
    p = _prepare_params(dec_seq, norm_pre_mix, w_in, conv_w, a_log, dt_bias, delta_norm_w, sgu_ln_w,
                        sgu_ln_b, w_spatial, b_spatial, w_proj_a, w_proj_b, w_out, norm_post_mix,
                        norm_pre_ffn, w_ffn_in, w_ffn_out, norm_post_ffn)
    y_p, y_s = x_prompt, x_sample
    conv0 = jnp.zeros((nb, CONV_W - 1, D_CONV), x_prompt.dtype)
    s_zero = jnp.zeros((nb, N_HEADS, HEAD_D, HEAD_D), state_delta.dtype)
    sd_p, sc_p, cv_p, sc_s, cv_s = [], [], [], [], []
    sd_s = None
    for l in range(depth):
        y_p, s_new, buf_new, v_rows = _trunk_layer(
            y_p, p, l, lambda x2d, n, t: _front_long(x2d, n, t, conv0, s_zero, p, l))
        sd_p.append(s_new)
        sc_p.append(buf_new)
        cv_p.append(v_rows)
        y_s, sd_s, buf_new, v_rows = _trunk_layer(
            y_s, p, l, lambda x2d, n, t: _front_short(x2d, n, t, state_conv, state_delta, p, l, sd_s))
        sc_s.append(buf_new)
        cv_s.append(v_rows)
    return (y_p, y_s, jnp.stack(sd_p), jnp.stack(sc_p), jnp.stack(cv_p),
            sd_s, jnp.stack(sc_s), jnp.stack(cv_s))
```

```python
import functools

import jax
import jax.numpy as jnp
from jax import lax
from jax.experimental import pallas as pl
from jax.experimental.pallas import tpu as pltpu

F32 = jnp.float32
BF16 = jnp.bfloat16

D_MODEL = 1024
N_HEADS = 8
HEAD_D = 128
D_QK = N_HEADS * HEAD_D
D_CONV = 3 * D_QK
CONV_W = 4
DELTA_CHUNK = 64
MLP_CHUNK = 128
N_GROUPS = 8
D_B = 1024
D_FF = 2816
D_MAIN = D_CONV + D_QK + 2 * D_B + 2 * D_MODEL
AB_OFF = D_CONV + D_QK
LANES = 128
SUBLANES = 8

TM = 512
TM_FFN = 1024
TF_FFN = 256
TT_DELTA = 128
SEQS_DELTA = 4
DEC_SEQS_PER_STEP = 8
VMEM_LIMIT = 48 * 1024 * 1024

_HI = lax.Precision.HIGHEST
_NT = (((1,), (1,)), ((), ()))
_TN = (((0,), (0,)), ((), ()))
_NN = (((1,), (0,)), ((), ()))


def _mm(a, b, dims=_NN):
    return lax.dot_general(a.astype(BF16), b.astype(BF16), dims, preferred_element_type=F32)


def _mm_hi(a, b, dims=_NN):
    return lax.dot_general(a, b, dims, precision=_HI, preferred_element_type=F32)


def _rms(x, w, eps=1e-6):
    return x * lax.rsqrt(jnp.mean(jnp.square(x), axis=-1, keepdims=True) + eps) * w


def _params(sem):
    return pltpu.CompilerParams(dimension_semantics=sem, vmem_limit_bytes=VMEM_LIMIT)


def _layer_spec(shape, l, ngrid):
    zeros = (0,) * len(shape)
    if ngrid == 1:
        return pl.BlockSpec((None,) + shape, lambda i: (l,) + zeros)
    return pl.BlockSpec((None,) + shape, lambda i, j: (l,) + zeros)


def _inproj_kernel(x_ref, nw_ref, wlo_ref, wab_ref, o_ref, ab_ref):
    h = _rms(x_ref[...], nw_ref[...]).astype(BF16)
    ab_ref[...] = _mm(h, wab_ref[...])
    o_ref[...] = _mm(h, wlo_ref[...])


def _inproj(x2d, p, l):
    m = x2d.shape[0]
    tm = min(TM, m)
    return pl.pallas_call(
        _inproj_kernel,
        grid=(m // tm,),
        in_specs=[
            pl.BlockSpec((tm, D_MODEL), lambda i: (i, 0)),
            _layer_spec((1, D_MODEL), l, 1),
            pl.BlockSpec((None, D_MODEL, AB_OFF), lambda i: (l, 0, 0), pipeline_mode=pl.Buffered(1)),
            _layer_spec((D_MODEL, LANES), l, 1),
        ],
        out_specs=[
            pl.BlockSpec((tm, AB_OFF), lambda i: (i, 0)),
            pl.BlockSpec((tm, LANES), lambda i: (i, 0)),
        ],
        out_shape=[
            jax.ShapeDtypeStruct((m, AB_OFF), F32),
            jax.ShapeDtypeStruct((m, LANES), F32),
        ],
        compiler_params=_params(("arbitrary",)),
        name="inproj",
    )(x2d, p["norm_pre_mix"], p["w_lo"], p["w_ab"])


def _conv_silu_norm(y, part):
    y = y * jax.nn.sigmoid(y)
    if part < 2:
        inv = lax.rsqrt(jnp.sum(jnp.square(y), axis=-1, keepdims=True) + 1e-6)
        if part == 0:
            inv = inv * (HEAD_D ** -0.5)
        y = y * inv
    return y


def _causal_conv(xg, w_of_tap):
    sub = lax.broadcasted_iota(jnp.int32, (xg.shape[0] - 1,) + xg.shape[1:], 1)
    y = None
    for i in range(CONV_W):
        k = CONV_W - 1 - i
        if k == 0:
            tap = xg[1:]
        else:
            rot = pltpu.roll(xg, k, axis=1)
            tap = jnp.where(sub >= k, rot[1:], rot[:-1])
        term = w_of_tap(i) * tap
        y = term if y is None else y + term
    return y


def _delta_dec_kernel(qkv_ref, z_ref, ab_ref, cbuf_ref, s0_ref, cw_ref, alog_ref, dtb_ref, nw_ref,
                      *rest, nb, c, aliased):
    if aliased:
        rest = rest[1:]
    o_ref, nbuf_ref, snew_ref, xbuf_ref, qkvc_ref = rest
    pad = SUBLANES
    hist = CONV_W - 1
    xbuf_ref[:, pad - hist:pad, :] = cbuf_ref[...]
    xbuf_ref[:, pad:pad + c, :] = qkv_ref[...]
    for part in range(3):
        for h in range(N_HEADS):
            c0 = part * D_QK + h * HEAD_D
            cols = slice(c0, c0 + HEAD_D)
            y = cw_ref[0:1, cols] * xbuf_ref[:, pad - hist:pad - hist + c, cols]
            for i in range(1, CONV_W):
                y = y + cw_ref[i:i + 1, cols] * xbuf_ref[:, pad - hist + i:pad - hist + i + c, cols]
            qkvc_ref[:, :, cols] = _conv_silu_norm(y, part)
    nbuf_ref[...] = xbuf_ref[:, pad + c - hist:pad + c, :]

    ri = lax.broadcasted_iota(jnp.int32, (c, c), 0)
    ci = lax.broadcasted_iota(jnp.int32, (c, c), 1)
    causal = ri >= ci
    strict = ri > ci
    diag = ri == ci
    tri = jnp.where(causal, 1.0, 0.0).astype(F32)
    eye = jnp.where(diag, 1.0, 0.0).astype(F32)
    n_levels = c.bit_length() - 2
    items = [(i, h) for i in range(nb) for h in range(N_HEADS)]
    idx = {it: n for n, it in enumerate(items)}
    every = range(len(items))

    gc, beta = [], []
    for i in range(nb):
        g = -jnp.exp(alog_ref[...]) * jax.nn.softplus(ab_ref[i] + dtb_ref[...])
        gc.append(_mm_hi(tri, g))
        beta.append(jax.nn.sigmoid(ab_ref[i]))

    def col(part, i, h):
        return qkvc_ref[i, :, part * D_QK + h * HEAD_D:part * D_QK + (h + 1) * HEAD_D]

    gcol = [gc[i][:, h:h + 1] for i, h in items]
    bcol = [beta[i][:, N_HEADS + h:N_HEADS + h + 1] for i, h in items]
    kb = [col(1, i, h) * bcol[idx[i, h]] for i, h in items]
    kq = [_mm(jnp.concatenate([kb[idx[i, h]], col(0, i, h)], axis=0), col(1, i, h), _NT)
          for i, h in items]
    x, qk, tk = [], [], []
    for n in every:
        grow = jnp.sum(jnp.where(diag, gcol[n], 0.0), axis=0, keepdims=True)
        dec = jnp.where(causal, jnp.exp(jnp.where(causal, gcol[n] - grow, 0.0)), 0.0)
        x.append(jnp.where(strict, -(kq[n][:c] * dec), 0.0))
        qk.append(kq[n][c:] * dec)
        tk.append(eye + x[n])
    pk = [_mm(x[n], x[n]) for n in every]
    for lvl in range(1, n_levels + 1):
        if lvl < n_levels:
            r = [_mm(jnp.concatenate([pk[n], tk[n]], axis=0), pk[n]) for n in every]
            pk = [r[n][:c] for n in every]
            tk = [tk[n] + r[n][c:] for n in every]
        else:
            tk = [tk[n] + _mm(tk[n], pk[n]) for n in every]
    e = [jnp.exp(gcol[n]) for n in every]
    sol = [_mm(tk[idx[i, h]], jnp.concatenate([col(2, i, h) * bcol[idx[i, h]],
                                                kb[idx[i, h]] * e[idx[i, h]]], axis=1))
           for i, h in items]
    wq = [_mm(jnp.concatenate([sol[idx[i, h]][:, HEAD_D:], col(0, i, h) * e[idx[i, h]]], axis=0),
              s0_ref[i, h]) for i, h in items]
    vn = [sol[n][:, :HEAD_D] - wq[n][:c] for n in every]
    op = [_mm(qk[n], vn[n]) for n in every]
    for i, h in items:
        n = idx[i, h]
        glast = gc[i][c - 1:c, h:h + 1]
        kdec = col(1, i, h) * jnp.exp(glast - gcol[n])
        snew_ref[i, h] = s0_ref[i, h] * jnp.exp(glast) + _mm(kdec, vn[n], _TN)
    for i, h in items:
        n = idx[i, h]
        o = _rms(wq[n][c:] + op[n], nw_ref[...])
        cols = slice(h * HEAD_D, (h + 1) * HEAD_D)
        zz = z_ref[i, :, cols]
        o_ref[i, :, cols] = o * (zz * jax.nn.sigmoid(zz))


def _delta_dec(proj3, ab3, conv_all, s_all, p, l, s_out_prev):
    n, c, _ = proj3.shape
    depth = s_all.shape[0]
    nb = DEC_SEQS_PER_STEP
    aliased = s_out_prev is not None
    kern = functools.partial(_delta_dec_kernel, nb=nb, c=c, aliased=aliased)
    state_blk = (None, nb, N_HEADS, HEAD_D, HEAD_D)
    in_specs = [
        pl.BlockSpec((nb, c, D_CONV), lambda i: (i, 0, 0)),
        pl.BlockSpec((nb, c, D_QK), lambda i: (i, 0, D_CONV // D_QK)),
        pl.BlockSpec((nb, c, LANES), lambda i: (i, 0, 0)),
        pl.BlockSpec((None, nb, CONV_W - 1, D_CONV), lambda i: (l, i, 0, 0)),
        pl.BlockSpec(state_blk, lambda i: (l, i, 0, 0, 0)),
        _layer_spec((CONV_W, D_CONV), l, 1),
        _layer_spec((1, LANES), l, 1),
        _layer_spec((1, LANES), l, 1),
        _layer_spec((1, HEAD_D), l, 1),
    ]
    args = [proj3, proj3, ab3, conv_all, s_all, p["conv_w"], p["a_log"], p["dt_bias"], p["delta_norm_w"]]
    aliases = {}
    if aliased:
        in_specs.append(pl.BlockSpec(memory_space=pl.ANY))
        args.append(s_out_prev)
        aliases = {len(args) - 1: 2}
    return pl.pallas_call(
        kern,
        grid=(n // nb,),
        in_specs=in_specs,
        out_specs=[
            pl.BlockSpec((nb, c, D_QK), lambda i: (i, 0, 0)),
            pl.BlockSpec((nb, CONV_W - 1, D_CONV), lambda i: (i, 0, 0)),
            pl.BlockSpec(state_blk, lambda i: (l, i, 0, 0, 0)),
        ],
        out_shape=[
            jax.ShapeDtypeStruct((n, c, D_QK), F32),
            jax.ShapeDtypeStruct((n, CONV_W - 1, D_CONV), F32),
            jax.ShapeDtypeStruct((depth, n, N_HEADS, HEAD_D, HEAD_D), F32),
        ],
        scratch_shapes=[
            pltpu.VMEM((nb, c + SUBLANES, D_CONV), F32),
            pltpu.VMEM((nb, c, D_CONV), F32),
        ],
        input_output_aliases=aliases,
        compiler_params=_params(("arbitrary",)),
        name="delta_dec",
    )(*args)


def _delta_chunk(refs, ic, c):
    qkvc_ref, gc_ref, beta_ref, s_ref, nw_ref, o_ref = refs
    assert 2 * c == LANES
    ns = gc_ref.shape[0]
    ri = lax.broadcasted_iota(jnp.int32, (c, LANES), 0)
    lane = lax.broadcasted_iota(jnp.int32, (c, LANES), 1)
    half = lane >= c
    cj = jnp.where(half, lane - c, lane)
    causal = ri >= cj
    strict = ri > cj
    diag = ri == cj
    eye = jnp.where(diag, 1.0, 0.0).astype(F32)
    zeros = jnp.zeros((c, HEAD_D), F32)
    zeros2 = jnp.zeros((c, 2 * HEAD_D), F32)
    n_levels = c.bit_length() - 2

    def blockdiag(m):
        return jnp.concatenate([jnp.where(half, 0.0, m), jnp.where(half, m, 0.0)], axis=0)

    def side_by_side(a, b):
        return jnp.concatenate([jnp.concatenate([a, zeros], axis=1),
                                jnp.concatenate([zeros, b], axis=1)], axis=0)

    rows = pl.ds(pl.multiple_of(ic * c, c), c)
    heads = [(s, h) for s in range(ns) for h in range(N_HEADS)]
    pairs = [(s, pr) for s in range(ns) for pr in range(N_HEADS // 2)]
    gc = [gc_ref[s, rows, :] for s in range(ns)]
    beta = [beta_ref[s, rows, :] for s in range(ns)]

    def col(part, s, h):
        return qkvc_ref[s, rows, part * D_QK + h * HEAD_D:part * D_QK + (h + 1) * HEAD_D]

    g = {(s, h): jnp.broadcast_to(gc[s][:, h:h + 1], (c, HEAD_D)) for s, h in heads}
    b = {(s, h): jnp.broadcast_to(beta[s][:, N_HEADS + h:N_HEADS + h + 1], (c, HEAD_D))
         for s, h in heads}
    kb = {(s, h): col(1, s, h) * b[s, h] for s, h in heads}
    kq = {}
    for s, pr in pairs:
        h1, h2 = 2 * pr, 2 * pr + 1
        lhs = jnp.concatenate([jnp.concatenate([kb[s, h1], kb[s, h2]], axis=1),
                               jnp.concatenate([col(0, s, h1), col(0, s, h2)], axis=1)], axis=0)
        kq[s, pr] = _mm(lhs, side_by_side(col(1, s, h1), col(1, s, h2)), _NT)
    x, qkd, tk = {}, {}, {}
    for s, pr in pairs:
        gcp = jnp.where(half, g[s, 2 * pr + 1], g[s, 2 * pr])
        rowp = jnp.sum(jnp.where(diag, gcp, 0.0), axis=0, keepdims=True)
        dec = jnp.where(causal, jnp.exp(jnp.where(causal, gcp - rowp, 0.0)), 0.0)
        x[s, pr] = jnp.where(strict, -(kq[s, pr][:c] * dec), 0.0)
        qkd[s, pr] = kq[s, pr][c:] * dec
        tk[s, pr] = eye + x[s, pr]
    pk = {sp: _mm(x[sp], blockdiag(x[sp])) for sp in pairs}
    for lvl in range(1, n_levels + 1):
        if lvl < n_levels:
            r = {sp: _mm(jnp.concatenate([pk[sp], tk[sp]], axis=0), blockdiag(pk[sp])) for sp in pairs}
            pk = {sp: r[sp][:c] for sp in pairs}
            tk = {sp: tk[sp] + r[sp][c:] for sp in pairs}
        else:
            tk = {sp: tk[sp] + _mm(tk[sp], blockdiag(pk[sp])) for sp in pairs}
    e = {sh: jnp.exp(g[sh]) for sh in heads}
    sol = {}
    for s, h in heads:
        rhs = jnp.concatenate([col(2, s, h) * b[s, h], kb[s, h] * e[s, h]], axis=1)
        rhs = jnp.concatenate([rhs, zeros2] if h % 2 == 0 else [zeros2, rhs], axis=0)
        sol[s, h] = _mm(tk[s, h // 2], rhs)
    wq = {(s, h): _mm(jnp.concatenate([sol[s, h][:, HEAD_D:], col(0, s, h) * e[s, h]], axis=0),
                      s_ref[s, h]) for s, h in heads}
    vn = {sh: sol[sh][:, :HEAD_D] - wq[sh][:c] for sh in heads}
    op = {(s, pr): _mm(qkd[s, pr], side_by_side(vn[s, 2 * pr], vn[s, 2 * pr + 1]))
          for s, pr in pairs}
    for s, h in heads:
        glast = gc[s][c - 1:c, h:h + 1]
        kdec = col(1, s, h) * jnp.exp(glast - g[s, h])
        s_ref[s, h] = s_ref[s, h] * jnp.exp(glast) + _mm(kdec, vn[s, h], _TN)
    for s, h in heads:
        o = _rms(wq[s, h][c:] + op[s, h // 2][:, (h % 2) * HEAD_D:(h % 2 + 1) * HEAD_D], nw_ref[...])
        cols = slice(h * HEAD_D, (h + 1) * HEAD_D)
        zz = qkvc_ref[s, rows, D_CONV + h * HEAD_D:D_CONV + (h + 1) * HEAD_D]
        o_ref[s, rows, cols] = (o * (zz * jax.nn.sigmoid(zz))).astype(o_ref.dtype)


def _delta_kernel(x_ref, npre_ref, wlo_ref, wab_ref, cbuf_ref, s0_ref, cw_ref, alog_ref, dtb_ref, nw_ref,
                  o_ref, nbuf_ref, snew_ref,
                  s_ref, hist_ref, qkvc_ref, gc_ref, beta_ref, *, tt, c):
    t = pl.program_id(1)
    nt = pl.num_programs(1)
    ns = x_ref.shape[0]
    hist = CONV_W - 1

    @pl.when(t == 0)
    def _():
        s_ref[...] = s0_ref[...]
        hist_ref[...] = jnp.zeros(hist_ref.shape, F32)
        hist_ref[:, SUBLANES - hist:, :] = cbuf_ref[...]

    hn = _rms(x_ref[...].reshape(ns * tt, D_MODEL), npre_ref[...]).astype(BF16)
    qkvc_ref[...] = _mm(hn, wlo_ref[...]).reshape(ns, tt, AB_OFF)
    ab_all = _mm(hn, wab_ref[...]).reshape(ns, tt, LANES)

    rt = lax.broadcasted_iota(jnp.int32, (c, c), 0)
    ct = lax.broadcasted_iota(jnp.int32, (c, c), 1)
    tri = jnp.where(rt >= ct, 1.0, 0.0).astype(F32)
    for s in range(ns):
        ab = ab_all[s]
        g = -jnp.exp(alog_ref[...]) * jax.nn.softplus(ab + dtb_ref[...])
        beta_ref[s] = jax.nn.sigmoid(ab)
        for ic in range(tt // c):
            gc_ref[s, ic * c:(ic + 1) * c, :] = _mm_hi(tri, g[ic * c:(ic + 1) * c, :])

    for s in range(ns):
        for part in range(3):
            for h in range(N_HEADS):
                c0 = part * D_QK + h * HEAD_D
                cols = slice(c0, c0 + HEAD_D)
                raw = qkvc_ref[s, :, cols]
                xg = jnp.concatenate([hist_ref[s, :, cols], raw], axis=0)
                xg = xg.reshape(tt // SUBLANES + 1, SUBLANES, HEAD_D)
                y = _causal_conv(xg, lambda i: cw_ref[i:i + 1, cols])
                qkvc_ref[s, :, cols] = _conv_silu_norm(y, part).reshape(tt, HEAD_D)
                hist_ref[s, :, cols] = raw[tt - SUBLANES:, :]

    @pl.when(t == nt - 1)
    def _():
        nbuf_ref[...] = hist_ref[:, SUBLANES - hist:, :]

    refs = (qkvc_ref, gc_ref, beta_ref, s_ref, nw_ref, o_ref)

    def body(ic, carry):
        _delta_chunk(refs, ic, c)
        return carry

    lax.fori_loop(0, tt // c, body, 0)

    @pl.when(t == nt - 1)
    def _():
        snew_ref[...] = s_ref[...]


def _delta(x, conv_buf, s0, p, l):
    n, t, _ = x.shape
    tt, c = min(TT_DELTA, t), DELTA_CHUNK
    ns = min(SEQS_DELTA, n)
    kern = functools.partial(_delta_kernel, tt=tt, c=c)
    tile = lambda i, j: (i, j, 0)
    return pl.pallas_call(
        kern,
        grid=(n // ns, t // tt),
        in_specs=[
            pl.BlockSpec((ns, tt, D_MODEL), tile),
            _layer_spec((1, D_MODEL), l, 2),
            pl.BlockSpec((None, D_MODEL, AB_OFF), lambda i, j: (l, 0, 0), pipeline_mode=pl.Buffered(1)),
            _layer_spec((D_MODEL, LANES), l, 2),
            pl.BlockSpec((ns, CONV_W - 1, D_CONV), lambda i, j: (i, 0, 0)),
            pl.BlockSpec((ns, N_HEADS, HEAD_D, HEAD_D), lambda i, j: (i, 0, 0, 0)),
            _layer_spec((CONV_W, D_CONV), l, 2),
            _layer_spec((1, LANES), l, 2),
            _layer_spec((1, LANES), l, 2),
            _layer_spec((1, HEAD_D), l, 2),
        ],
        out_specs=[
            pl.BlockSpec((ns, tt, D_QK), tile),
            pl.BlockSpec((ns, CONV_W - 1, D_CONV), lambda i, j: (i, 0, 0)),
            pl.BlockSpec((ns, N_HEADS, HEAD_D, HEAD_D), lambda i, j: (i, 0, 0, 0)),
        ],
        out_shape=[
            jax.ShapeDtypeStruct((n, t, D_QK), BF16),
            jax.ShapeDtypeStruct((n, CONV_W - 1, D_CONV), F32),
            jax.ShapeDtypeStruct((n, N_HEADS, HEAD_D, HEAD_D), F32),
        ],
        scratch_shapes=[
            pltpu.VMEM((ns, N_HEADS, HEAD_D, HEAD_D), F32),
            pltpu.VMEM((ns, SUBLANES, D_CONV), F32),
            pltpu.VMEM((ns, tt, AB_OFF), F32),
            pltpu.VMEM((ns, tt, LANES), F32),
            pltpu.VMEM((ns, tt, LANES), F32),
        ],
        compiler_params=_params(("arbitrary", "arbitrary")),
        name="delta",
    )(x, p["norm_pre_mix"], p["w_lo"], p["w_ab"], conv_buf, s0, p["conv_w"], p["a_log"], p["dt_bias"],
      p["delta_norm_w"])


def _mix_kernel(x_ref, oa_ref, npre_ref, whi_ref, lnw_ref, lnb_ref, ws_ref, bs_ref,
                wa_ref, wb_ref, wo_ref, nw_ref, y_ref, v_ref, uv_ref, gate_ref, ob_ref, *, keep_all_v):
    c = MLP_CHUNK
    tm = x_ref.shape[0]
    n_ch = tm // c
    tn = D_MODEL // n_ch
    gw = 2 * D_MODEL // n_ch
    ri = lax.broadcasted_iota(jnp.int32, (c, c), 0)
    ci = lax.broadcasted_iota(jnp.int32, (c, c), 1)
    gd = D_B // N_GROUPS
    ws = [jnp.where(ri >= ci, ws_ref[g], 0.0).astype(BF16) for g in range(N_GROUPS)]
    h = _rms(x_ref[...], npre_ref[...]).astype(BF16)
    uv_ref[...] = _mm(h, whi_ref[:, :2 * D_B])
    pa = []
    for ch in range(n_ch):
        pa.append(_mm(oa_ref[...], wa_ref[:, ch * tn:(ch + 1) * tn]))
        g0 = 2 * D_B + ch * gw
        gate_ref[:, ch * gw:(ch + 1) * gw] = jax.nn.sigmoid(_mm(h, whi_ref[:, g0:g0 + gw]))
        rows = slice(ch * c, (ch + 1) * c)
        x = uv_ref[rows, :]
        gel = 0.5 * x * (1.0 + lax.erf(x * (2.0 ** -0.5)))
        u = gel[:, :D_B]
        v = gel[:, D_B:]
        vc = v - jnp.mean(v, axis=-1, keepdims=True)
        var = jnp.mean(jnp.square(vc), axis=-1, keepdims=True)
        v = vc * lax.rsqrt(var + 1e-5) * lnw_ref[...] + lnb_ref[...]
        if keep_all_v:
            v_ref[rows, :] = v
        elif ch == n_ch - 1:
            v_ref[0] = v
        for g in range(N_GROUPS):
            cols = slice(g * gd, (g + 1) * gd)
            mixed = _mm(ws[g], v[:, cols]) + bs_ref[:, g:g + 1]
            ob_ref[rows, cols] = (u[:, cols] * mixed).astype(BF16)
    pa = jnp.concatenate(pa, axis=1)
    pb = _mm(ob_ref[...], wb_ref[...])
    merged = gate_ref[:, :D_MODEL] * pa + gate_ref[:, D_MODEL:] * pb
    y = _mm(merged, wo_ref[...])
    y_ref[...] = x_ref[...] + _rms(y, nw_ref[...])


def _mix(oa, x2d, p, l, seq_len):
    m = x2d.shape[0]
    tm = min(TM, m)
    c = MLP_CHUNK
    row = lambda i: (i, 0)
    short = seq_len < c
    if short:
        assert c % seq_len == 0 and tm % c == 0
        ws_key, bs_key = "ws_short", "bs_short"
        v_spec = pl.BlockSpec((tm, D_B), row)
        v_shape = jax.ShapeDtypeStruct((m, D_B), F32)
    else:
        assert seq_len % tm == 0
        ws_key, bs_key = "w_spatial", "bs_t"
        tiles_per_seq = seq_len // tm
        v_spec = pl.BlockSpec((1, c, D_B), lambda i: (i // tiles_per_seq, 0, 0))
        v_shape = jax.ShapeDtypeStruct((m // seq_len, c, D_B), F32)
    resident = dict(pipeline_mode=pl.Buffered(1))
    wspec = pl.BlockSpec((None, D_MODEL, D_MODEL), lambda i: (l, 0, 0), **resident)
    return pl.pallas_call(
        functools.partial(_mix_kernel, keep_all_v=short),
        grid=(m // tm,),
        in_specs=[
            pl.BlockSpec((tm, D_MODEL), row),
            pl.BlockSpec((tm, D_QK), row),
            _layer_spec((1, D_MODEL), l, 1),
            pl.BlockSpec((None, D_MODEL, D_MAIN - AB_OFF), lambda i: (l, 0, 0), **resident),
            _layer_spec((1, D_B), l, 1),
            _layer_spec((1, D_B), l, 1),
            _layer_spec((N_GROUPS, c, c), l, 1),
            _layer_spec((c, LANES), l, 1),
            wspec, wspec, wspec,
            _layer_spec((1, D_MODEL), l, 1),
        ],
        out_specs=[pl.BlockSpec((tm, D_MODEL), row), v_spec],
        out_shape=[jax.ShapeDtypeStruct((m, D_MODEL), F32), v_shape],
        scratch_shapes=[pltpu.VMEM((tm, 2 * D_B), F32), pltpu.VMEM((tm, 2 * D_MODEL), F32),
                        pltpu.VMEM((tm, D_B), BF16)],
        compiler_params=_params(("arbitrary",)),
        name="mix",
    )(x2d, oa, p["norm_pre_mix"], p["w_hi"], p["sgu_ln_w"], p["sgu_ln_b"], p[ws_key], p[bs_key],
      p["w_proj_a"], p["w_proj_b"], p["w_out"], p["norm_post_mix"])


def _ffn_kernel(x_ref, npre_ref, wi_ref, wd_ref, npost_ref, y_ref, act_ref):
    x = x_ref[...]
    h = _rms(x, npre_ref[...]).astype(BF16)
    tf = TF_FFN
    for f in range(D_FF // tf):
        gate = _mm(h, wi_ref[:, f * tf:(f + 1) * tf])
        up = _mm(h, wi_ref[:, D_FF + f * tf:D_FF + (f + 1) * tf])
        act_ref[:, f * tf:(f + 1) * tf] = (gate * jax.nn.sigmoid(gate) * up).astype(BF16)
    y = _mm(act_ref[...], wd_ref[...])
    y_ref[...] = x + _rms(y, npost_ref[...])


def _ffn(x2d, p, l):
    m = x2d.shape[0]
    tm = min(TM_FFN, m)
    resident = dict(pipeline_mode=pl.Buffered(1))
    return pl.pallas_call(
        _ffn_kernel,
        grid=(m // tm,),
        in_specs=[
            pl.BlockSpec((tm, D_MODEL), lambda i: (i, 0)),
            _layer_spec((1, D_MODEL), l, 1),
            pl.BlockSpec((None, D_MODEL, 2 * D_FF), lambda i: (l, 0, 0), **resident),
            pl.BlockSpec((None, D_FF, D_MODEL), lambda i: (l, 0, 0), **resident),
            _layer_spec((1, D_MODEL), l, 1),
        ],
        out_specs=pl.BlockSpec((tm, D_MODEL), lambda i: (i, 0)),
        out_shape=jax.ShapeDtypeStruct((m, D_MODEL), F32),
        scratch_shapes=[pltpu.VMEM((tm, D_FF), BF16)],
        compiler_params=_params(("arbitrary",)),
        name="ffn",
    )(x2d, p["norm_pre_ffn"], p["w_ffn_in"], p["w_ffn_out"], p["norm_post_ffn"])


def _front_short(x, conv_all, s_all, p, l, s_out_prev):
    n, t, _ = x.shape
    proj, ab = _inproj(x.reshape(n * t, D_MODEL), p, l)
    return _delta_dec(proj.reshape(n, t, AB_OFF), ab.reshape(n, t, LANES), conv_all, s_all, p, l,
                      s_out_prev)


def _trunk_layer(x, p, l, front):
    n, t, _ = x.shape
    m = n * t
    x2d = x.reshape(m, D_MODEL)
    o_a, new_buf, s_new = front(x)
    x1, v_rows = _mix(o_a.reshape(m, D_QK), x2d, p, l, t)
    x2 = _ffn(x1, p, l)
    return x2.reshape(n, t, D_MODEL), s_new, new_buf, v_rows.reshape(n, -1, D_B)


def _prepare_params(t_short, norm_pre_mix, w_in, conv_w, a_log, dt_bias, delta_norm_w, sgu_ln_w,
                    sgu_ln_b, w_spatial, b_spatial, w_proj_a, w_proj_b, w_out, norm_post_mix,
                    norm_pre_ffn, w_ffn_in, w_ffn_out, norm_post_ffn):
    depth = w_in.shape[0]
    w_lo = w_in[:, :, :AB_OFF].astype(BF16)
    w_hi = w_in[:, :, AB_OFF + 2 * N_HEADS:].astype(BF16)
    w_ab = jnp.pad(w_in[:, :, AB_OFF:AB_OFF + 2 * N_HEADS], ((0, 0), (0, 0), (0, LANES - 2 * N_HEADS)))
    row = lambda v: v.reshape(depth, 1, -1)
    lanes = lambda v: jnp.pad(v, ((0, 0), (0, LANES - v.shape[1]))).reshape(depth, 1, LANES)
    bs_t = jnp.pad(jnp.swapaxes(b_spatial, 1, 2), ((0, 0), (0, 0), (0, LANES - N_GROUPS)))
    rep = MLP_CHUNK // t_short
    idx = jnp.arange(MLP_CHUNK)
    same_block = (idx[:, None] // t_short) == (idx[None, :] // t_short)
    onehot = (idx[:, None] % t_short == jnp.arange(t_short)[None, :]).astype(F32)
    tiled = jnp.einsum("ri,lgij,cj->lgrc", onehot, w_spatial[:, :, :t_short, :t_short], onehot,
                       precision=_HI)
    ws_short = jnp.where(same_block, tiled, 0.0)
    return dict(
        norm_pre_mix=row(norm_pre_mix), w_lo=w_lo, w_hi=w_hi, w_ab=w_ab.astype(BF16),
        conv_w=conv_w, a_log=lanes(a_log), dt_bias=lanes(dt_bias), delta_norm_w=row(delta_norm_w),
        sgu_ln_w=row(sgu_ln_w), sgu_ln_b=row(sgu_ln_b), w_spatial=w_spatial, bs_t=bs_t,
        ws_short=ws_short, bs_short=jnp.tile(bs_t[:, :t_short], (1, rep, 1)),
        w_proj_a=w_proj_a.astype(BF16), w_proj_b=w_proj_b.astype(BF16), w_out=w_out.astype(BF16),
        norm_post_mix=row(norm_post_mix), norm_pre_ffn=row(norm_pre_ffn),
        w_ffn_in=w_ffn_in.astype(BF16), w_ffn_out=w_ffn_out.astype(BF16),
        norm_post_ffn=row(norm_post_ffn))


def kernel(x_prompt, x_sample, state_delta, state_conv, norm_pre_mix, w_in, conv_w, a_log, dt_bias,
           delta_norm_w, sgu_ln_w, sgu_ln_b, w_spatial, b_spatial, w_proj_a, w_proj_b, w_out,
           norm_post_mix, norm_pre_ffn, w_ffn_in, w_ffn_out, norm_post_ffn):
    depth = w_in.shape[0]
    nb, seq, _ = x_prompt.shape
    ndec, dec_seq, _ = x_sample.shape
    assert seq % DELTA_CHUNK == 0 and seq % MLP_CHUNK == 0
    assert dec_seq % SUBLANES == 0 and dec_seq < DELTA_CHUNK and ndec % DEC_SEQS_PER_STEP == 0
    p = _prepare_params(dec_seq, norm_pre_mix, w_in, conv_w, a_log, dt_bias, delta_norm_w, sgu_ln_w,
                        sgu_ln_b, w_spatial, b_spatial, w_proj_a, w_proj_b, w_out, norm_post_mix,
                        norm_pre_ffn, w_ffn_in, w_ffn_out, norm_post_ffn)
    y_p, y_s = x_prompt, x_sample
    conv0 = jnp.zeros((nb, CONV_W - 1, D_CONV), x_prompt.dtype)
    s_zero = jnp.zeros((nb, N_HEADS, HEAD_D, HEAD_D), state_delta.dtype)
    sd_p, sc_p, cv_p, sc_s, cv_s = [], [], [], [], []
    sd_s = None
    for l in range(depth):
        y_p, s_new, buf_new, v_rows = _trunk_layer(
            y_p, p, l, lambda x: _delta(x, conv0, s_zero, p, l))
        sd_p.append(s_new)
        sc_p.append(buf_new)
        cv_p.append(v_rows)
        y_s, sd_s, buf_new, v_rows = _trunk_layer(
            y_s, p, l, lambda x: _front_short(x, state_conv, state_delta, p, l, sd_s))
        sc_s.append(buf_new)
        cv_s.append(v_rows)
    return (y_p, y_s, jnp.stack(sd_p), jnp.stack(sc_p), jnp.stack(cv_p),
            sd_s, jnp.stack(sc_s), jnp.stack(cv_s))
```

```python
import functools

import jax
import jax.numpy as jnp
from jax import lax
from jax.experimental import pallas as pl
from jax.experimental.pallas import tpu as pltpu

F32 = jnp.float32
BF16 = jnp.bfloat16

D_MODEL = 1024
N_HEADS = 8
HEAD_D = 128
D_QK = N_HEADS * HEAD_D
D_CONV = 3 * D_QK
CONV_W = 4
DELTA_CHUNK = 64
MLP_CHUNK = 128
N_GROUPS = 8
D_B = 1024
D_FF = 2816
D_MAIN = D_CONV + D_QK + 2 * D_B + 2 * D_MODEL
AB_OFF = D_CONV + D_QK
LANES = 128
SUBLANES = 8

TM = 512
TM_FFN = 1024
TF_FFN = 256
TT_DELTA = 128
SEQS_DELTA = 4
DEC_SEQS_PER_STEP = 8
VMEM_LIMIT = 48 * 1024 * 1024

_HI = lax.Precision.HIGHEST
_NT = (((1,), (1,)), ((), ()))
_TN = (((0,), (0,)), ((), ()))
_NN = (((1,), (0,)), ((), ()))


def _mm(a, b, dims=_NN):
    return lax.dot_general(a.astype(BF16), b.astype(BF16), dims, preferred_element_type=F32)


def _mm_hi(a, b, dims=_NN):
    return lax.dot_general(a, b, dims, precision=_HI, preferred_element_type=F32)


def _rms(x, w, eps=1e-6):
    return x * lax.rsqrt(jnp.mean(jnp.square(x), axis=-1, keepdims=True) + eps) * w


def _params(sem):
    return pltpu.CompilerParams(dimension_semantics=sem, vmem_limit_bytes=VMEM_LIMIT)


def _layer_spec(shape, l, ngrid):
    zeros = (0,) * len(shape)
    if ngrid == 1:
        return pl.BlockSpec((None,) + shape, lambda i: (l,) + zeros)
    return pl.BlockSpec((None,) + shape, lambda i, j: (l,) + zeros)


def _inproj_kernel(x_ref, nw_ref, wlo_ref, wab_ref, o_ref, ab_ref):
    h = _rms(x_ref[...], nw_ref[...]).astype(BF16)
    ab_ref[...] = _mm(h, wab_ref[...])
    o_ref[...] = _mm(h, wlo_ref[...])


def _inproj(x2d, p, l):
    m = x2d.shape[0]
    tm = min(TM, m)
    return pl.pallas_call(
        _inproj_kernel,
        grid=(m // tm,),
        in_specs=[
            pl.BlockSpec((tm, D_MODEL), lambda i: (i, 0)),
            _layer_spec((1, D_MODEL), l, 1),
            pl.BlockSpec((None, D_MODEL, AB_OFF), lambda i: (l, 0, 0), pipeline_mode=pl.Buffered(1)),
            pl.BlockSpec((None, D_MODEL, LANES), lambda i: (l, 0, AB_OFF // LANES)),
        ],
        out_specs=[
            pl.BlockSpec((tm, AB_OFF), lambda i: (i, 0)),
            pl.BlockSpec((tm, LANES), lambda i: (i, 0)),
        ],
        out_shape=[
            jax.ShapeDtypeStruct((m, AB_OFF), F32),
            jax.ShapeDtypeStruct((m, LANES), F32),
        ],
        compiler_params=_params(("arbitrary",)),
        name="inproj",
    )(x2d, p["norm_pre_mix"], p["w_lo"], p["w_ab"])


def _conv_silu_norm(y, part):
    y = y * jax.nn.sigmoid(y)
    if part < 2:
        inv = lax.rsqrt(jnp.sum(jnp.square(y), axis=-1, keepdims=True) + 1e-6)
        if part == 0:
            inv = inv * (HEAD_D ** -0.5)
        y = y * inv
    return y


def _causal_conv(xg, w_of_tap):
    sub = lax.broadcasted_iota(jnp.int32, (xg.shape[0] - 1,) + xg.shape[1:], 1)
    y = None
    for i in range(CONV_W):
        k = CONV_W - 1 - i
        if k == 0:
            tap = xg[1:]
        else:
            rot = pltpu.roll(xg, k, axis=1)
            tap = jnp.where(sub >= k, rot[1:], rot[:-1])
        term = w_of_tap(i) * tap
        y = term if y is None else y + term
    return y


def _delta_dec_kernel(qkv_ref, z_ref, ab_ref, cbuf_ref, s0_ref, cw_ref, alog_ref, dtb_ref, nw_ref,
                      *rest, nb, c, aliased):
    if aliased:
        rest = rest[1:]
    o_ref, nbuf_ref, snew_ref, xbuf_ref, qkvc_ref = rest
    pad = SUBLANES
    hist = CONV_W - 1
    xbuf_ref[:, pad - hist:pad, :] = cbuf_ref[...]
    xbuf_ref[:, pad:pad + c, :] = qkv_ref[...]
    for part in range(3):
        for h in range(N_HEADS):
            c0 = part * D_QK + h * HEAD_D
            cols = slice(c0, c0 + HEAD_D)
            y = cw_ref[0:1, cols] * xbuf_ref[:, pad - hist:pad - hist + c, cols]
            for i in range(1, CONV_W):
                y = y + cw_ref[i:i + 1, cols] * xbuf_ref[:, pad - hist + i:pad - hist + i + c, cols]
            qkvc_ref[:, :, cols] = _conv_silu_norm(y, part)
    nbuf_ref[...] = xbuf_ref[:, pad + c - hist:pad + c, :]

    ri = lax.broadcasted_iota(jnp.int32, (c, c), 0)
    ci = lax.broadcasted_iota(jnp.int32, (c, c), 1)
    causal = ri >= ci
    strict = ri > ci
    diag = ri == ci
    tri = jnp.where(causal, 1.0, 0.0).astype(F32)
    eye = jnp.where(diag, 1.0, 0.0).astype(F32)
    n_levels = c.bit_length() - 2
    items = [(i, h) for i in range(nb) for h in range(N_HEADS)]
    idx = {it: n for n, it in enumerate(items)}
    every = range(len(items))

    gc, beta = [], []
    for i in range(nb):
        g = -jnp.exp(alog_ref[...]) * jax.nn.softplus(ab_ref[i] + dtb_ref[...])
        gc.append(_mm_hi(tri, g))
        beta.append(jax.nn.sigmoid(ab_ref[i]))

    def col(part, i, h):
        return qkvc_ref[i, :, part * D_QK + h * HEAD_D:part * D_QK + (h + 1) * HEAD_D]

    gcol = [gc[i][:, h:h + 1] for i, h in items]
    bcol = [beta[i][:, N_HEADS + h:N_HEADS + h + 1] for i, h in items]
    kb = [col(1, i, h) * bcol[idx[i, h]] for i, h in items]
    kq = [_mm(jnp.concatenate([kb[idx[i, h]], col(0, i, h)], axis=0), col(1, i, h), _NT)
          for i, h in items]
    x, qk, tk = [], [], []
    for n in every:
        grow = jnp.sum(jnp.where(diag, gcol[n], 0.0), axis=0, keepdims=True)
        dec = jnp.where(causal, jnp.exp(jnp.where(causal, gcol[n] - grow, 0.0)), 0.0)
        x.append(jnp.where(strict, -(kq[n][:c] * dec), 0.0))
        qk.append(kq[n][c:] * dec)
        tk.append(eye + x[n])
    pk = [_mm(x[n], x[n]) for n in every]
    for lvl in range(1, n_levels + 1):
        if lvl < n_levels:
            r = [_mm(jnp.concatenate([pk[n], tk[n]], axis=0), pk[n]) for n in every]
            pk = [r[n][:c] for n in every]
            tk = [tk[n] + r[n][c:] for n in every]
        else:
            tk = [tk[n] + _mm(tk[n], pk[n]) for n in every]
    e = [jnp.exp(gcol[n]) for n in every]
    sol = [_mm(tk[idx[i, h]], jnp.concatenate([col(2, i, h) * bcol[idx[i, h]],
                                                kb[idx[i, h]] * e[idx[i, h]]], axis=1))
           for i, h in items]
    wq = [_mm(jnp.concatenate([sol[idx[i, h]][:, HEAD_D:], col(0, i, h) * e[idx[i, h]]], axis=0),
              s0_ref[i, h]) for i, h in items]
    vn = [sol[n][:, :HEAD_D] - wq[n][:c] for n in every]
    op = [_mm(qk[n], vn[n]) for n in every]
    for i, h in items:
        n = idx[i, h]
        glast = gc[i][c - 1:c, h:h + 1]
        kdec = col(1, i, h) * jnp.exp(glast - gcol[n])
        snew_ref[i, h] = s0_ref[i, h] * jnp.exp(glast) + _mm(kdec, vn[n], _TN)
    for i, h in items:
        n = idx[i, h]
        o = _rms(wq[n][c:] + op[n], nw_ref[...])
        cols = slice(h * HEAD_D, (h + 1) * HEAD_D)
        zz = z_ref[i, :, cols]
        o_ref[i, :, cols] = o * (zz * jax.nn.sigmoid(zz))


def _delta_dec(proj3, ab3, conv_all, s_all, p, l, s_out_prev):
    n, c, _ = proj3.shape
    depth = s_all.shape[0]
    nb = DEC_SEQS_PER_STEP
    aliased = s_out_prev is not None
    kern = functools.partial(_delta_dec_kernel, nb=nb, c=c, aliased=aliased)
    state_blk = (None, nb, N_HEADS, HEAD_D, HEAD_D)
    in_specs = [
        pl.BlockSpec((nb, c, D_CONV), lambda i: (i, 0, 0)),
        pl.BlockSpec((nb, c, D_QK), lambda i: (i, 0, D_CONV // D_QK)),
        pl.BlockSpec((nb, c, LANES), lambda i: (i, 0, 0)),
        pl.BlockSpec((None, nb, CONV_W - 1, D_CONV), lambda i: (l, i, 0, 0)),
        pl.BlockSpec(state_blk, lambda i: (l, i, 0, 0, 0)),
        _layer_spec((CONV_W, D_CONV), l, 1),
        _layer_spec((1, LANES), l, 1),
        _layer_spec((1, LANES), l, 1),
        _layer_spec((1, HEAD_D), l, 1),
    ]
    args = [proj3, proj3, ab3, conv_all, s_all, p["conv_w"], p["a_log"], p["dt_bias"], p["delta_norm_w"]]
    aliases = {}
    if aliased:
        in_specs.append(pl.BlockSpec(memory_space=pl.ANY))
        args.append(s_out_prev)
        aliases = {len(args) - 1: 2}
    return pl.pallas_call(
        kern,
        grid=(n // nb,),
        in_specs=in_specs,
        out_specs=[
            pl.BlockSpec((nb, c, D_QK), lambda i: (i, 0, 0)),
            pl.BlockSpec((nb, CONV_W - 1, D_CONV), lambda i: (i, 0, 0)),
            pl.BlockSpec(state_blk, lambda i: (l, i, 0, 0, 0)),
        ],
        out_shape=[
            jax.ShapeDtypeStruct((n, c, D_QK), F32),
            jax.ShapeDtypeStruct((n, CONV_W - 1, D_CONV), F32),
            jax.ShapeDtypeStruct((depth, n, N_HEADS, HEAD_D, HEAD_D), F32),
        ],
        scratch_shapes=[
            pltpu.VMEM((nb, c + SUBLANES, D_CONV), F32),
            pltpu.VMEM((nb, c, D_CONV), F32),
        ],
        input_output_aliases=aliases,
        compiler_params=_params(("arbitrary",)),
        name="delta_dec",
    )(*args)


def _delta_chunk(refs, ic, c):
    qkvc_ref, gc_ref, beta_ref, s_ref, nw_ref, o_ref = refs
    assert 2 * c == LANES
    ns = gc_ref.shape[0]
    ri = lax.broadcasted_iota(jnp.int32, (c, LANES), 0)
    lane = lax.broadcasted_iota(jnp.int32, (c, LANES), 1)
    half = lane >= c
    cj = jnp.where(half, lane - c, lane)
    causal = ri >= cj
    strict = ri > cj
    diag = ri == cj
    eye = jnp.where(diag, 1.0, 0.0).astype(F32)
    zeros = jnp.zeros((c, HEAD_D), F32)
    zeros2 = jnp.zeros((c, 2 * HEAD_D), F32)
    n_levels = c.bit_length() - 2

    def blockdiag(m):
        return jnp.concatenate([jnp.where(half, 0.0, m), jnp.where(half, m, 0.0)], axis=0)

    def side_by_side(a, b):
        return jnp.concatenate([jnp.concatenate([a, zeros], axis=1),
                                jnp.concatenate([zeros, b], axis=1)], axis=0)

    rows = pl.ds(pl.multiple_of(ic * c, c), c)
    heads = [(s, h) for s in range(ns) for h in range(N_HEADS)]
    pairs = [(s, pr) for s in range(ns) for pr in range(N_HEADS // 2)]
    gc = [gc_ref[s, rows, :] for s in range(ns)]
    beta = [beta_ref[s, rows, :] for s in range(ns)]

    def col(part, s, h):
        return qkvc_ref[s, rows, part * D_QK + h * HEAD_D:part * D_QK + (h + 1) * HEAD_D]

    g = {(s, h): jnp.broadcast_to(gc[s][:, h:h + 1], (c, HEAD_D)) for s, h in heads}
    b = {(s, h): jnp.broadcast_to(beta[s][:, N_HEADS + h:N_HEADS + h + 1], (c, HEAD_D))
         for s, h in heads}
    kb = {(s, h): col(1, s, h) * b[s, h] for s, h in heads}
    kq = {}
    for s, pr in pairs:
        h1, h2 = 2 * pr, 2 * pr + 1
        lhs = jnp.concatenate([jnp.concatenate([kb[s, h1], kb[s, h2]], axis=1),
                               jnp.concatenate([col(0, s, h1), col(0, s, h2)], axis=1)], axis=0)
        kq[s, pr] = _mm(lhs, side_by_side(col(1, s, h1), col(1, s, h2)), _NT)
    x, qkd, tk = {}, {}, {}
    for s, pr in pairs:
        gcp = jnp.where(half, g[s, 2 * pr + 1], g[s, 2 * pr])
        rowp = jnp.sum(jnp.where(diag, gcp, 0.0), axis=0, keepdims=True)
        dec = jnp.where(causal, jnp.exp(jnp.where(causal, gcp - rowp, 0.0)), 0.0)
        x[s, pr] = jnp.where(strict, -(kq[s, pr][:c] * dec), 0.0)
        qkd[s, pr] = kq[s, pr][c:] * dec
        tk[s, pr] = eye + x[s, pr]
    pk = {sp: _mm(x[sp], blockdiag(x[sp])) for sp in pairs}
    for lvl in range(1, n_levels + 1):
        if lvl < n_levels:
            r = {sp: _mm(jnp.concatenate([pk[sp], tk[sp]], axis=0), blockdiag(pk[sp])) for sp in pairs}
            pk = {sp: r[sp][:c] for sp in pairs}
            tk = {sp: tk[sp] + r[sp][c:] for sp in pairs}
        else:
            tk = {sp: tk[sp] + _mm(tk[sp], blockdiag(pk[sp])) for sp in pairs}
    e = {sh: jnp.exp(g[sh]) for sh in heads}
    sol = {}
    for s, h in heads:
        rhs = jnp.concatenate([col(2, s, h) * b[s, h], kb[s, h] * e[s, h]], axis=1)
        rhs = jnp.concatenate([rhs, zeros2] if h % 2 == 0 else [zeros2, rhs], axis=0)
        sol[s, h] = _mm(tk[s, h // 2], rhs)
    wq = {(s, h): _mm(jnp.concatenate([sol[s, h][:, HEAD_D:], col(0, s, h) * e[s, h]], axis=0),
                      s_ref[s, h]) for s, h in heads}
    vn = {sh: sol[sh][:, :HEAD_D] - wq[sh][:c] for sh in heads}
    op = {(s, pr): _mm(qkd[s, pr], side_by_side(vn[s, 2 * pr], vn[s, 2 * pr + 1]))
          for s, pr in pairs}
    for s, h in heads:
        glast = gc[s][c - 1:c, h:h + 1]
        kdec = col(1, s, h) * jnp.exp(glast - g[s, h])
        s_ref[s, h] = s_ref[s, h] * jnp.exp(glast) + _mm(kdec, vn[s, h], _TN)
    for s, h in heads:
        o = _rms(wq[s, h][c:] + op[s, h // 2][:, (h % 2) * HEAD_D:(h % 2 + 1) * HEAD_D], nw_ref[...])
        cols = slice(h * HEAD_D, (h + 1) * HEAD_D)
        zz = qkvc_ref[s, rows, D_CONV + h * HEAD_D:D_CONV + (h + 1) * HEAD_D]
        o_ref[s, rows, cols] = (o * (zz * jax.nn.sigmoid(zz))).astype(o_ref.dtype)


def _delta_kernel(x_ref, npre_ref, wlo_ref, wab_ref, cbuf_ref, s0_ref, cw_ref, alog_ref, dtb_ref, nw_ref,
                  o_ref, nbuf_ref, snew_ref,
                  s_ref, hist_ref, qkvc_ref, gc_ref, beta_ref, *, tt, c):
    t = pl.program_id(1)
    nt = pl.num_programs(1)
    ns = x_ref.shape[0]
    hist = CONV_W - 1

    @pl.when(t == 0)
    def _():
        s_ref[...] = s0_ref[...]
        hist_ref[...] = jnp.zeros(hist_ref.shape, F32)
        hist_ref[:, SUBLANES - hist:, :] = cbuf_ref[...]

    hn = _rms(x_ref[...].reshape(ns * tt, D_MODEL), npre_ref[...]).astype(BF16)
    qkvc_ref[...] = _mm(hn, wlo_ref[...]).reshape(ns, tt, AB_OFF)
    ab_all = _mm(hn, wab_ref[...]).reshape(ns, tt, LANES)

    rt = lax.broadcasted_iota(jnp.int32, (c, c), 0)
    ct = lax.broadcasted_iota(jnp.int32, (c, c), 1)
    tri = jnp.where(rt >= ct, 1.0, 0.0).astype(F32)
    for s in range(ns):
        ab = ab_all[s]
        g = -jnp.exp(alog_ref[...]) * jax.nn.softplus(ab + dtb_ref[...])
        beta_ref[s] = jax.nn.sigmoid(ab)
        for ic in range(tt // c):
            gc_ref[s, ic * c:(ic + 1) * c, :] = _mm_hi(tri, g[ic * c:(ic + 1) * c, :])

    for s in range(ns):
        for part in range(3):
            for h in range(N_HEADS):
                c0 = part * D_QK + h * HEAD_D
                cols = slice(c0, c0 + HEAD_D)
                raw = qkvc_ref[s, :, cols]
                xg = jnp.concatenate([hist_ref[s, :, cols], raw], axis=0)
                xg = xg.reshape(tt // SUBLANES + 1, SUBLANES, HEAD_D)
                y = _causal_conv(xg, lambda i: cw_ref[i:i + 1, cols])
                qkvc_ref[s, :, cols] = _conv_silu_norm(y, part).reshape(tt, HEAD_D)
                hist_ref[s, :, cols] = raw[tt - SUBLANES:, :]

    @pl.when(t == nt - 1)
    def _():
        nbuf_ref[...] = hist_ref[:, SUBLANES - hist:, :]

    refs = (qkvc_ref, gc_ref, beta_ref, s_ref, nw_ref, o_ref)

    def body(ic, carry):
        _delta_chunk(refs, ic, c)
        return carry

    lax.fori_loop(0, tt // c, body, 0)

    @pl.when(t == nt - 1)
    def _():
        snew_ref[...] = s_ref[...]


def _delta(x, conv_buf, s0, p, l):
    n, t, _ = x.shape
    tt, c = min(TT_DELTA, t), DELTA_CHUNK
    ns = min(SEQS_DELTA, n)
    kern = functools.partial(_delta_kernel, tt=tt, c=c)
    tile = lambda i, j: (i, j, 0)
    return pl.pallas_call(
        kern,
        grid=(n // ns, t // tt),
        in_specs=[
            pl.BlockSpec((ns, tt, D_MODEL), tile),
            _layer_spec((1, D_MODEL), l, 2),
            pl.BlockSpec((None, D_MODEL, AB_OFF), lambda i, j: (l, 0, 0), pipeline_mode=pl.Buffered(1)),
            pl.BlockSpec((None, D_MODEL, LANES), lambda i, j: (l, 0, AB_OFF // LANES)),
            pl.BlockSpec((ns, CONV_W - 1, D_CONV), lambda i, j: (i, 0, 0)),
            pl.BlockSpec((ns, N_HEADS, HEAD_D, HEAD_D), lambda i, j: (i, 0, 0, 0)),
            _layer_spec((CONV_W, D_CONV), l, 2),
            _layer_spec((1, LANES), l, 2),
            _layer_spec((1, LANES), l, 2),
            _layer_spec((1, HEAD_D), l, 2),
        ],
        out_specs=[
            pl.BlockSpec((ns, tt, D_QK), tile),
            pl.BlockSpec((ns, CONV_W - 1, D_CONV), lambda i, j: (i, 0, 0)),
            pl.BlockSpec((ns, N_HEADS, HEAD_D, HEAD_D), lambda i, j: (i, 0, 0, 0)),
        ],
        out_shape=[
            jax.ShapeDtypeStruct((n, t, D_QK), BF16),
            jax.ShapeDtypeStruct((n, CONV_W - 1, D_CONV), F32),
            jax.ShapeDtypeStruct((n, N_HEADS, HEAD_D, HEAD_D), F32),
        ],
        scratch_shapes=[
            pltpu.VMEM((ns, N_HEADS, HEAD_D, HEAD_D), F32),
            pltpu.VMEM((ns, SUBLANES, D_CONV), F32),
            pltpu.VMEM((ns, tt, AB_OFF), F32),
            pltpu.VMEM((ns, tt, LANES), F32),
            pltpu.VMEM((ns, tt, LANES), F32),
        ],
        compiler_params=_params(("arbitrary", "arbitrary")),
        name="delta",
    )(x, p["norm_pre_mix"], p["w_lo"], p["w_ab"], conv_buf, s0, p["conv_w"], p["a_log"], p["dt_bias"],
      p["delta_norm_w"])


def _mix_kernel(x_ref, oa_ref, npre_ref, whi_ref, lnw_ref, lnb_ref, ws_ref, bs_ref,
                wa_ref, wb_ref, wo_ref, nw_ref, y_ref, v_ref, uv_ref, gate_ref, ob_ref, *, keep_all_v):
    c = MLP_CHUNK
    tm = x_ref.shape[0]
    n_ch = tm // c
    tn = D_MODEL // n_ch
    gw = 2 * D_MODEL // n_ch
    ri = lax.broadcasted_iota(jnp.int32, (c, c), 0)
    ci = lax.broadcasted_iota(jnp.int32, (c, c), 1)
    gd = D_B // N_GROUPS
    ws = [jnp.where(ri >= ci, ws_ref[g], 0.0).astype(BF16) for g in range(N_GROUPS)]
    h = _rms(x_ref[...], npre_ref[...]).astype(BF16)
    uv_ref[...] = _mm(h, whi_ref[:, :2 * D_B])
    pa = []
    for ch in range(n_ch):
        pa.append(_mm(oa_ref[...], wa_ref[:, ch * tn:(ch + 1) * tn]))
        g0 = 2 * D_B + ch * gw
        gate_ref[:, ch * gw:(ch + 1) * gw] = jax.nn.sigmoid(_mm(h, whi_ref[:, g0:g0 + gw]))
        rows = slice(ch * c, (ch + 1) * c)
        x = uv_ref[rows, :]
        gel = 0.5 * x * (1.0 + lax.erf(x * (2.0 ** -0.5)))
        u = gel[:, :D_B]
        v = gel[:, D_B:]
        vc = v - jnp.mean(v, axis=-1, keepdims=True)
        var = jnp.mean(jnp.square(vc), axis=-1, keepdims=True)
        v = vc * lax.rsqrt(var + 1e-5) * lnw_ref[...] + lnb_ref[...]
        if keep_all_v:
            v_ref[rows, :] = v
        elif ch == n_ch - 1:
            v_ref[0] = v
        for g in range(N_GROUPS):
            cols = slice(g * gd, (g + 1) * gd)
            mixed = _mm(ws[g], v[:, cols]) + bs_ref[:, g:g + 1]
            ob_ref[rows, cols] = (u[:, cols] * mixed).astype(BF16)
    pa = jnp.concatenate(pa, axis=1)
    pb = _mm(ob_ref[...], wb_ref[...])
    merged = gate_ref[:, :D_MODEL] * pa + gate_ref[:, D_MODEL:] * pb
    y = _mm(merged, wo_ref[...])
    y_ref[...] = x_ref[...] + _rms(y, nw_ref[...])


def _mix(oa, x2d, p, l, seq_len):
    m = x2d.shape[0]
    tm = min(TM, m)
    c = MLP_CHUNK
    row = lambda i: (i, 0)
    short = seq_len < c
    if short:
        assert c % seq_len == 0 and tm % c == 0
        ws_key, bs_key = "ws_short", "bs_short"
        v_spec = pl.BlockSpec((tm, D_B), row)
        v_shape = jax.ShapeDtypeStruct((m, D_B), F32)
    else:
        assert seq_len % tm == 0
        ws_key, bs_key = "w_spatial", "bs_t"
        tiles_per_seq = seq_len // tm
        v_spec = pl.BlockSpec((1, c, D_B), lambda i: (i // tiles_per_seq, 0, 0))
        v_shape = jax.ShapeDtypeStruct((m // seq_len, c, D_B), F32)
    resident = dict(pipeline_mode=pl.Buffered(1))
    wspec = pl.BlockSpec((None, D_MODEL, D_MODEL), lambda i: (l, 0, 0), **resident)
    return pl.pallas_call(
        functools.partial(_mix_kernel, keep_all_v=short),
        grid=(m // tm,),
        in_specs=[
            pl.BlockSpec((tm, D_MODEL), row),
            pl.BlockSpec((tm, D_QK), row),
            _layer_spec((1, D_MODEL), l, 1),
            pl.BlockSpec((None, D_MODEL, D_MAIN - AB_OFF), lambda i: (l, 0, 0), **resident),
            _layer_spec((1, D_B), l, 1),
            _layer_spec((1, D_B), l, 1),
            _layer_spec((N_GROUPS, c, c), l, 1),
            _layer_spec((c, LANES), l, 1),
            wspec, wspec, wspec,
            _layer_spec((1, D_MODEL), l, 1),
        ],
        out_specs=[pl.BlockSpec((tm, D_MODEL), row), v_spec],
        out_shape=[jax.ShapeDtypeStruct((m, D_MODEL), F32), v_shape],
        scratch_shapes=[pltpu.VMEM((tm, 2 * D_B), F32), pltpu.VMEM((tm, 2 * D_MODEL), F32),
                        pltpu.VMEM((tm, D_B), BF16)],
        compiler_params=_params(("arbitrary",)),
        name="mix",
    )(x2d, oa, p["norm_pre_mix"], p["w_hi"], p["sgu_ln_w"], p["sgu_ln_b"], p[ws_key], p[bs_key],
      p["w_proj_a"], p["w_proj_b"], p["w_out"], p["norm_post_mix"])


def _ffn_kernel(x_ref, npre_ref, wi_ref, wd_ref, npost_ref, y_ref, act_ref):
    x = x_ref[...]
    h = _rms(x, npre_ref[...]).astype(BF16)
    tf = TF_FFN
    for f in range(D_FF // tf):
        gate = _mm(h, wi_ref[:, f * tf:(f + 1) * tf])
        up = _mm(h, wi_ref[:, D_FF + f * tf:D_FF + (f + 1) * tf])
        act_ref[:, f * tf:(f + 1) * tf] = (gate * jax.nn.sigmoid(gate) * up).astype(BF16)
    y = _mm(act_ref[...], wd_ref[...])
    y_ref[...] = x + _rms(y, npost_ref[...])


def _ffn(x2d, p, l):
    m = x2d.shape[0]
    tm = min(TM_FFN, m)
    resident = dict(pipeline_mode=pl.Buffered(1))
    return pl.pallas_call(
        _ffn_kernel,
        grid=(m // tm,),
        in_specs=[
            pl.BlockSpec((tm, D_MODEL), lambda i: (i, 0)),
            _layer_spec((1, D_MODEL), l, 1),
            pl.BlockSpec((None, D_MODEL, 2 * D_FF), lambda i: (l, 0, 0), **resident),
            pl.BlockSpec((None, D_FF, D_MODEL), lambda i: (l, 0, 0), **resident),
            _layer_spec((1, D_MODEL), l, 1),
        ],
        out_specs=pl.BlockSpec((tm, D_MODEL), lambda i: (i, 0)),
        out_shape=jax.ShapeDtypeStruct((m, D_MODEL), F32),
        scratch_shapes=[pltpu.VMEM((tm, D_FF), BF16)],
        compiler_params=_params(("arbitrary",)),
        name="ffn",
    )(x2d, p["norm_pre_ffn"], p["w_ffn_in"], p["w_ffn_out"], p["norm_post_ffn"])


def _front_short(x, conv_all, s_all, p, l, s_out_prev):
    n, t, _ = x.shape
    proj, ab = _inproj(x.reshape(n * t, D_MODEL), p, l)
    return _delta_dec(proj.reshape(n, t, AB_OFF), ab.reshape(n, t, LANES), conv_all, s_all, p, l,
                      s_out_prev)


def _trunk_layer(x, p, l, front):
    n, t, _ = x.shape
    m = n * t
    x2d = x.reshape(m, D_MODEL)
    o_a, new_buf, s_new = front(x)
    x1, v_rows = _mix(o_a.reshape(m, D_QK), x2d, p, l, t)
    x2 = _ffn(x1, p, l)
    return x2.reshape(n, t, D_MODEL), s_new, new_buf, v_rows.reshape(n, -1, D_B)


def _prepare_params(t_short, norm_pre_mix, w_in, conv_w, a_log, dt_bias, delta_norm_w, sgu_ln_w,
                    sgu_ln_b, w_spatial, b_spatial, w_proj_a, w_proj_b, w_out, norm_post_mix,
                    norm_pre_ffn, w_ffn_in, w_ffn_out, norm_post_ffn):
    depth = w_in.shape[0]
    w_bf = w_in.astype(BF16)
    w_hi = w_bf[:, :, AB_OFF + 2 * N_HEADS:]
    row = lambda v: v.reshape(depth, 1, -1)
    lanes = lambda v: jnp.pad(v, ((0, 0), (0, LANES - v.shape[1]))).reshape(depth, 1, LANES)
    bs_t = jnp.pad(jnp.swapaxes(b_spatial, 1, 2), ((0, 0), (0, 0), (0, LANES - N_GROUPS)))
    rep = MLP_CHUNK // t_short
    idx = jnp.arange(MLP_CHUNK)
    same_block = (idx[:, None] // t_short) == (idx[None, :] // t_short)
    onehot = (idx[:, None] % t_short == jnp.arange(t_short)[None, :]).astype(F32)
    tiled = jnp.einsum("ri,lgij,cj->lgrc", onehot, w_spatial[:, :, :t_short, :t_short], onehot,
                       precision=_HI)
    ws_short = jnp.where(same_block, tiled, 0.0)
    return dict(
        norm_pre_mix=row(norm_pre_mix), w_lo=w_bf, w_hi=w_hi, w_ab=w_bf,
        conv_w=conv_w, a_log=lanes(a_log), dt_bias=lanes(dt_bias), delta_norm_w=row(delta_norm_w),
        sgu_ln_w=row(sgu_ln_w), sgu_ln_b=row(sgu_ln_b), w_spatial=w_spatial, bs_t=bs_t,
        ws_short=ws_short, bs_short=jnp.tile(bs_t[:, :t_short], (1, rep, 1)),
        w_proj_a=w_proj_a.astype(BF16), w_proj_b=w_proj_b.astype(BF16), w_out=w_out.astype(BF16),
        norm_post_mix=row(norm_post_mix), norm_pre_ffn=row(norm_pre_ffn),
        w_ffn_in=w_ffn_in.astype(BF16), w_ffn_out=w_ffn_out.astype(BF16),
        norm_post_ffn=row(norm_post_ffn))


def kernel(x_prompt, x_sample, state_delta, state_conv, norm_pre_mix, w_in, conv_w, a_log, dt_bias,
           delta_norm_w, sgu_ln_w, sgu_ln_b, w_spatial, b_spatial, w_proj_a, w_proj_b, w_out,
           norm_post_mix, norm_pre_ffn, w_ffn_in, w_ffn_out, norm_post_ffn):
    depth = w_in.shape[0]
    nb, seq, _ = x_prompt.shape
    ndec, dec_seq, _ = x_sample.shape
    assert seq % DELTA_CHUNK == 0 and seq % MLP_CHUNK == 0
    assert dec_seq % SUBLANES == 0 and dec_seq < DELTA_CHUNK and ndec % DEC_SEQS_PER_STEP == 0
    p = _prepare_params(dec_seq, norm_pre_mix, w_in, conv_w, a_log, dt_bias, delta_norm_w, sgu_ln_w,
                        sgu_ln_b, w_spatial, b_spatial, w_proj_a, w_proj_b, w_out, norm_post_mix,
                        norm_pre_ffn, w_ffn_in, w_ffn_out, norm_post_ffn)
    y_p, y_s = x_prompt, x_sample
    conv0 = jnp.zeros((nb, CONV_W - 1, D_CONV), x_prompt.dtype)
    s_zero = jnp.zeros((nb, N_HEADS, HEAD_D, HEAD_D), state_delta.dtype)
    sd_p, sc_p, cv_p, sc_s, cv_s = [], [], [], [], []
    sd_s = None
    for l in range(depth):
        y_p, s_new, buf_new, v_rows = _trunk_layer(
            y_p, p, l, lambda x: _delta(x, conv0, s_zero, p, l))
        sd_p.append(s_new)
        sc_p.append(buf_new)
        cv_p.append(v_rows)
        y_s, sd_s, buf_new, v_rows = _trunk_layer(
            y_s, p, l, lambda x: _front_short(x, state_conv, state_delta, p, l, sd_s))
        sc_s.append(buf_new)
        cv_s.append(v_rows)
    return (y_p, y_s, jnp.stack(sd_p), jnp.stack(sc_p), jnp.stack(cv_p),
            sd_s, jnp.stack(sc_s), jnp.stack(cv_s))
```

```python
import functools

import jax
import jax.numpy as jnp
from jax import lax
from jax.experimental import pallas as pl
from jax.experimental.pallas import tpu as pltpu

F32 = jnp.float32
BF16 = jnp.bfloat16

D_MODEL = 1024
N_HEADS = 8
HEAD_D = 128
D_QK = N_HEADS * HEAD_D
D_CONV = 3 * D_QK
CONV_W = 4
DELTA_CHUNK = 64
MLP_CHUNK = 128
N_GROUPS = 8
D_B = 1024
D_FF = 2816
D_MAIN = D_CONV + D_QK + 2 * D_B + 2 * D_MODEL
AB_OFF = D_CONV + D_QK
LANES = 128
SUBLANES = 8

TM = 512
TM_FFN = 1024
TF_FFN = 256
TT_DELTA = 128
SEQS_DELTA = 4
DEC_SEQS_PER_STEP = 8
VMEM_LIMIT = 48 * 1024 * 1024

_HI = lax.Precision.HIGHEST
_NT = (((1,), (1,)), ((), ()))
_TN = (((0,), (0,)), ((), ()))
_NN = (((1,), (0,)), ((), ()))


def _mm(a, b, dims=_NN):
    return lax.dot_general(a.astype(BF16), b.astype(BF16), dims, preferred_element_type=F32)


def _mm_hi(a, b, dims=_NN):
    return lax.dot_general(a, b, dims, precision=_HI, preferred_element_type=F32)


def _rms(x, w, eps=1e-6):
    return x * lax.rsqrt(jnp.mean(jnp.square(x), axis=-1, keepdims=True) + eps) * w


def _params(sem):
    return pltpu.CompilerParams(dimension_semantics=sem, vmem_limit_bytes=VMEM_LIMIT)


def _layer_spec(shape, l, ngrid):
    zeros = (0,) * len(shape)
    if ngrid == 1:
        return pl.BlockSpec((None,) + shape, lambda i: (l,) + zeros)
    return pl.BlockSpec((None,) + shape, lambda i, j: (l,) + zeros)


def _inproj_kernel(x_ref, nw_ref, wlo_ref, wab_ref, o_ref, ab_ref):
    h = _rms(x_ref[...], nw_ref[...]).astype(BF16)
    ab_ref[...] = _mm(h, wab_ref[...])
    o_ref[...] = _mm(h, wlo_ref[...])


def _inproj(x2d, p, l):
    m = x2d.shape[0]
    tm = min(TM, m)
    return pl.pallas_call(
        _inproj_kernel,
        grid=(m // tm,),
        in_specs=[
            pl.BlockSpec((tm, D_MODEL), lambda i: (i, 0)),
            _layer_spec((1, D_MODEL), l, 1),
            pl.BlockSpec((None, D_MODEL, AB_OFF), lambda i: (l, 0, 0), pipeline_mode=pl.Buffered(1)),
            pl.BlockSpec((None, D_MODEL, LANES), lambda i: (l, 0, AB_OFF // LANES)),
        ],
        out_specs=[
            pl.BlockSpec((tm, AB_OFF), lambda i: (i, 0)),
            pl.BlockSpec((tm, LANES), lambda i: (i, 0)),
        ],
        out_shape=[
            jax.ShapeDtypeStruct((m, AB_OFF), F32),
            jax.ShapeDtypeStruct((m, LANES), F32),
        ],
        compiler_params=_params(("arbitrary",)),
        name="inproj",
    )(x2d, p["norm_pre_mix"], p["w_lo"], p["w_ab"])


def _conv_silu_norm(y, part):
    y = y * jax.nn.sigmoid(y)
    if part < 2:
        inv = lax.rsqrt(jnp.sum(jnp.square(y), axis=-1, keepdims=True) + 1e-6)
        if part == 0:
            inv = inv * (HEAD_D ** -0.5)
        y = y * inv
    return y


def _causal_conv(xg, w_of_tap):
    sub = lax.broadcasted_iota(jnp.int32, (xg.shape[0] - 1,) + xg.shape[1:], 1)
    y = None
    for i in range(CONV_W):
        k = CONV_W - 1 - i
        if k == 0:
            tap = xg[1:]
        else:
            rot = pltpu.roll(xg, k, axis=1)
            tap = jnp.where(sub >= k, rot[1:], rot[:-1])
        term = w_of_tap(i) * tap
        y = term if y is None else y + term
    return y


def _delta_dec_kernel(qkv_ref, z_ref, ab_ref, cbuf_ref, s0_ref, cw_ref, alog_ref, dtb_ref, nw_ref,
                      *rest, nb, c, aliased):
    if aliased:
        rest = rest[2:]
    o_ref, nbuf_ref, snew_ref, xbuf_ref, qkvc_ref = rest
    pad = SUBLANES
    hist = CONV_W - 1
    xbuf_ref[:, pad - hist:pad, :] = cbuf_ref[...]
    xbuf_ref[:, pad:pad + c, :] = qkv_ref[...]
    for part in range(3):
        for h in range(N_HEADS):
            c0 = part * D_QK + h * HEAD_D
            cols = slice(c0, c0 + HEAD_D)
            y = cw_ref[0:1, cols] * xbuf_ref[:, pad - hist:pad - hist + c, cols]
            for i in range(1, CONV_W):
                y = y + cw_ref[i:i + 1, cols] * xbuf_ref[:, pad - hist + i:pad - hist + i + c, cols]
            qkvc_ref[:, :, cols] = _conv_silu_norm(y, part)
    nbuf_ref[...] = xbuf_ref[:, pad + c - hist:pad + c, :]

    ri = lax.broadcasted_iota(jnp.int32, (c, c), 0)
    ci = lax.broadcasted_iota(jnp.int32, (c, c), 1)
    causal = ri >= ci
    strict = ri > ci
    diag = ri == ci
    tri = jnp.where(causal, 1.0, 0.0).astype(F32)
    eye = jnp.where(diag, 1.0, 0.0).astype(F32)
    n_levels = c.bit_length() - 2
    items = [(i, h) for i in range(nb) for h in range(N_HEADS)]
    idx = {it: n for n, it in enumerate(items)}
    every = range(len(items))

    gc, beta = [], []
    for i in range(nb):
        g = -jnp.exp(alog_ref[...]) * jax.nn.softplus(ab_ref[i] + dtb_ref[...])
        gc.append(_mm_hi(tri, g))
        beta.append(jax.nn.sigmoid(ab_ref[i]))

    def col(part, i, h):
        return qkvc_ref[i, :, part * D_QK + h * HEAD_D:part * D_QK + (h + 1) * HEAD_D]

    gcol = [gc[i][:, h:h + 1] for i, h in items]
    bcol = [beta[i][:, N_HEADS + h:N_HEADS + h + 1] for i, h in items]
    kb = [col(1, i, h) * bcol[idx[i, h]] for i, h in items]
    kq = [_mm(jnp.concatenate([kb[idx[i, h]], col(0, i, h)], axis=0), col(1, i, h), _NT)
          for i, h in items]
    x, qk, tk = [], [], []
    for n in every:
        grow = jnp.sum(jnp.where(diag, gcol[n], 0.0), axis=0, keepdims=True)
        dec = jnp.where(causal, jnp.exp(jnp.where(causal, gcol[n] - grow, 0.0)), 0.0)
        x.append(jnp.where(strict, -(kq[n][:c] * dec), 0.0))
        qk.append(kq[n][c:] * dec)
        tk.append(eye + x[n])
    pk = [_mm(x[n], x[n]) for n in every]
    for lvl in range(1, n_levels + 1):
        if lvl < n_levels:
            r = [_mm(jnp.concatenate([pk[n], tk[n]], axis=0), pk[n]) for n in every]
            pk = [r[n][:c] for n in every]
            tk = [tk[n] + r[n][c:] for n in every]
        else:
            tk = [tk[n] + _mm(tk[n], pk[n]) for n in every]
    e = [jnp.exp(gcol[n]) for n in every]
    sol = [_mm(tk[idx[i, h]], jnp.concatenate([col(2, i, h) * bcol[idx[i, h]],
                                                kb[idx[i, h]] * e[idx[i, h]]], axis=1))
           for i, h in items]
    wq = [_mm(jnp.concatenate([sol[idx[i, h]][:, HEAD_D:], col(0, i, h) * e[idx[i, h]]], axis=0),
              s0_ref[i, h]) for i, h in items]
    vn = [sol[n][:, :HEAD_D] - wq[n][:c] for n in every]
    op = [_mm(qk[n], vn[n]) for n in every]
    for i, h in items:
        n = idx[i, h]
        glast = gc[i][c - 1:c, h:h + 1]
        kdec = col(1, i, h) * jnp.exp(glast - gcol[n])
        snew_ref[i, h] = s0_ref[i, h] * jnp.exp(glast) + _mm(kdec, vn[n], _TN)
    for i, h in items:
        n = idx[i, h]
        o = _rms(wq[n][c:] + op[n], nw_ref[...])
        cols = slice(h * HEAD_D, (h + 1) * HEAD_D)
        zz = z_ref[i, :, cols]
        o_ref[i, :, cols] = o * (zz * jax.nn.sigmoid(zz))


def _delta_dec(proj3, ab3, conv_all, s_all, p, l, prev):
    n, c, _ = proj3.shape
    depth = s_all.shape[0]
    nb = DEC_SEQS_PER_STEP
    aliased = prev is not None
    kern = functools.partial(_delta_dec_kernel, nb=nb, c=c, aliased=aliased)
    state_blk = (None, nb, N_HEADS, HEAD_D, HEAD_D)
    in_specs = [
        pl.BlockSpec((nb, c, D_CONV), lambda i: (i, 0, 0)),
        pl.BlockSpec((nb, c, D_QK), lambda i: (i, 0, D_CONV // D_QK)),
        pl.BlockSpec((nb, c, LANES), lambda i: (i, 0, 0)),
        pl.BlockSpec((None, nb, CONV_W - 1, D_CONV), lambda i: (l, i, 0, 0)),
        pl.BlockSpec(state_blk, lambda i: (l, i, 0, 0, 0)),
        _layer_spec((CONV_W, D_CONV), l, 1),
        _layer_spec((1, LANES), l, 1),
        _layer_spec((1, LANES), l, 1),
        _layer_spec((1, HEAD_D), l, 1),
    ]
    args = [proj3, proj3, ab3, conv_all, s_all, p["conv_w"], p["a_log"], p["dt_bias"], p["delta_norm_w"]]
    aliases = {}
    if aliased:
        in_specs += [pl.BlockSpec(memory_space=pl.ANY), pl.BlockSpec(memory_space=pl.ANY)]
        args += list(prev)
        aliases = {len(args) - 2: 1, len(args) - 1: 2}
    return pl.pallas_call(
        kern,
        grid=(n // nb,),
        in_specs=in_specs,
        out_specs=[
            pl.BlockSpec((nb, c, D_QK), lambda i: (i, 0, 0)),
            pl.BlockSpec((None, nb, CONV_W - 1, D_CONV), lambda i: (l, i, 0, 0)),
            pl.BlockSpec(state_blk, lambda i: (l, i, 0, 0, 0)),
        ],
        out_shape=[
            jax.ShapeDtypeStruct((n, c, D_QK), F32),
            jax.ShapeDtypeStruct((depth, n, CONV_W - 1, D_CONV), F32),
            jax.ShapeDtypeStruct((depth, n, N_HEADS, HEAD_D, HEAD_D), F32),
        ],
        scratch_shapes=[
            pltpu.VMEM((nb, c + SUBLANES, D_CONV), F32),
            pltpu.VMEM((nb, c, D_CONV), F32),
        ],
        input_output_aliases=aliases,
        compiler_params=_params(("arbitrary",)),
        name="delta_dec",
    )(*args)


def _delta_chunk(refs, ic, c):
    qkvc_ref, gc_ref, beta_ref, s_ref, nw_ref, o_ref = refs
    assert 2 * c == LANES
    ns = gc_ref.shape[0]
    ri = lax.broadcasted_iota(jnp.int32, (c, LANES), 0)
    lane = lax.broadcasted_iota(jnp.int32, (c, LANES), 1)
    half = lane >= c
    cj = jnp.where(half, lane - c, lane)
    causal = ri >= cj
    strict = ri > cj
    diag = ri == cj
    eye = jnp.where(diag, 1.0, 0.0).astype(F32)
    zeros = jnp.zeros((c, HEAD_D), F32)
    zeros2 = jnp.zeros((c, 2 * HEAD_D), F32)
    n_levels = c.bit_length() - 2

    def blockdiag(m):
        return jnp.concatenate([jnp.where(half, 0.0, m), jnp.where(half, m, 0.0)], axis=0)

    def side_by_side(a, b):
        return jnp.concatenate([jnp.concatenate([a, zeros], axis=1),
                                jnp.concatenate([zeros, b], axis=1)], axis=0)

    rows = pl.ds(pl.multiple_of(ic * c, c), c)
    heads = [(s, h) for s in range(ns) for h in range(N_HEADS)]
    pairs = [(s, pr) for s in range(ns) for pr in range(N_HEADS // 2)]
    gc = [gc_ref[s, rows, :] for s in range(ns)]
    beta = [beta_ref[s, rows, :] for s in range(ns)]

    def col(part, s, h):
        return qkvc_ref[s, rows, part * D_QK + h * HEAD_D:part * D_QK + (h + 1) * HEAD_D]

    g = {(s, h): jnp.broadcast_to(gc[s][:, h:h + 1], (c, HEAD_D)) for s, h in heads}
    b = {(s, h): jnp.broadcast_to(beta[s][:, N_HEADS + h:N_HEADS + h + 1], (c, HEAD_D))
         for s, h in heads}
    kb = {(s, h): col(1, s, h) * b[s, h] for s, h in heads}
    kq = {}
    for s, pr in pairs:
        h1, h2 = 2 * pr, 2 * pr + 1
        lhs = jnp.concatenate([jnp.concatenate([kb[s, h1], kb[s, h2]], axis=1),
                               jnp.concatenate([col(0, s, h1), col(0, s, h2)], axis=1)], axis=0)
        kq[s, pr] = _mm(lhs, side_by_side(col(1, s, h1), col(1, s, h2)), _NT)
    x, qkd, tk = {}, {}, {}
    for s, pr in pairs:
        gcp = jnp.where(half, g[s, 2 * pr + 1], g[s, 2 * pr])
        rowp = jnp.sum(jnp.where(diag, gcp, 0.0), axis=0, keepdims=True)
        dec = jnp.where(causal, jnp.exp(jnp.where(causal, gcp - rowp, 0.0)), 0.0)
        x[s, pr] = jnp.where(strict, -(kq[s, pr][:c] * dec), 0.0)
        qkd[s, pr] = kq[s, pr][c:] * dec
        tk[s, pr] = eye + x[s, pr]
    pk = {sp: _mm(x[sp], blockdiag(x[sp])) for sp in pairs}
    for lvl in range(1, n_levels + 1):
        if lvl < n_levels:
            r = {sp: _mm(jnp.concatenate([pk[sp], tk[sp]], axis=0), blockdiag(pk[sp])) for sp in pairs}
            pk = {sp: r[sp][:c] for sp in pairs}
            tk = {sp: tk[sp] + r[sp][c:] for sp in pairs}
        else:
            tk = {sp: tk[sp] + _mm(tk[sp], blockdiag(pk[sp])) for sp in pairs}
    e = {sh: jnp.exp(g[sh]) for sh in heads}
    sol = {}
    for s, h in heads:
        rhs = jnp.concatenate([col(2, s, h) * b[s, h], kb[s, h] * e[s, h]], axis=1)
        rhs = jnp.concatenate([rhs, zeros2] if h % 2 == 0 else [zeros2, rhs], axis=0)
        sol[s, h] = _mm(tk[s, h // 2], rhs)
    wq = {(s, h): _mm(jnp.concatenate([sol[s, h][:, HEAD_D:], col(0, s, h) * e[s, h]], axis=0),
                      s_ref[s, h]) for s, h in heads}
    vn = {sh: sol[sh][:, :HEAD_D] - wq[sh][:c] for sh in heads}
    op = {(s, pr): _mm(qkd[s, pr], side_by_side(vn[s, 2 * pr], vn[s, 2 * pr + 1]))
          for s, pr in pairs}
    for s, h in heads:
        glast = gc[s][c - 1:c, h:h + 1]
        kdec = col(1, s, h) * jnp.exp(glast - g[s, h])
        s_ref[s, h] = s_ref[s, h] * jnp.exp(glast) + _mm(kdec, vn[s, h], _TN)
    for s, h in heads:
        o = _rms(wq[s, h][c:] + op[s, h // 2][:, (h % 2) * HEAD_D:(h % 2 + 1) * HEAD_D], nw_ref[...])
        cols = slice(h * HEAD_D, (h + 1) * HEAD_D)
        zz = qkvc_ref[s, rows, D_CONV + h * HEAD_D:D_CONV + (h + 1) * HEAD_D]
        o_ref[s, rows, cols] = (o * (zz * jax.nn.sigmoid(zz))).astype(o_ref.dtype)


def _delta_kernel(x_ref, npre_ref, wlo_ref, wab_ref, cbuf_ref, s0_ref, cw_ref, alog_ref, dtb_ref, nw_ref,
                  o_ref, nbuf_ref, snew_ref,
                  s_ref, hist_ref, qkvc_ref, gc_ref, beta_ref, *, tt, c):
    t = pl.program_id(1)
    nt = pl.num_programs(1)
    ns = x_ref.shape[0]
    hist = CONV_W - 1

    @pl.when(t == 0)
    def _():
        s_ref[...] = s0_ref[...]
        hist_ref[...] = jnp.zeros(hist_ref.shape, F32)
        hist_ref[:, SUBLANES - hist:, :] = cbuf_ref[...]

    hn = _rms(x_ref[...].reshape(ns * tt, D_MODEL), npre_ref[...]).astype(BF16)
    qkvc_ref[...] = _mm(hn, wlo_ref[...]).reshape(ns, tt, AB_OFF)
    ab_all = _mm(hn, wab_ref[...]).reshape(ns, tt, LANES)

    rt = lax.broadcasted_iota(jnp.int32, (c, c), 0)
    ct = lax.broadcasted_iota(jnp.int32, (c, c), 1)
    tri = jnp.where(rt >= ct, 1.0, 0.0).astype(F32)
    for s in range(ns):
        ab = ab_all[s]
        g = -jnp.exp(alog_ref[...]) * jax.nn.softplus(ab + dtb_ref[...])
        beta_ref[s] = jax.nn.sigmoid(ab)
        for ic in range(tt // c):
            gc_ref[s, ic * c:(ic + 1) * c, :] = _mm_hi(tri, g[ic * c:(ic + 1) * c, :])

    for s in range(ns):
        for part in range(3):
            for h in range(N_HEADS):
                c0 = part * D_QK + h * HEAD_D
                cols = slice(c0, c0 + HEAD_D)
                raw = qkvc_ref[s, :, cols]
                xg = jnp.concatenate([hist_ref[s, :, cols], raw], axis=0)
                xg = xg.reshape(tt // SUBLANES + 1, SUBLANES, HEAD_D)
                y = _causal_conv(xg, lambda i: cw_ref[i:i + 1, cols])
                qkvc_ref[s, :, cols] = _conv_silu_norm(y, part).reshape(tt, HEAD_D)
                hist_ref[s, :, cols] = raw[tt - SUBLANES:, :]

    @pl.when(t == nt - 1)
    def _():
        nbuf_ref[...] = hist_ref[:, SUBLANES - hist:, :]

    refs = (qkvc_ref, gc_ref, beta_ref, s_ref, nw_ref, o_ref)

    def body(ic, carry):
        _delta_chunk(refs, ic, c)
        return carry

    lax.fori_loop(0, tt // c, body, 0)

    @pl.when(t == nt - 1)
    def _():
        snew_ref[...] = s_ref[...]


def _delta(x, conv_buf, s0, p, l):
    n, t, _ = x.shape
    tt, c = min(TT_DELTA, t), DELTA_CHUNK
    ns = min(SEQS_DELTA, n)
    kern = functools.partial(_delta_kernel, tt=tt, c=c)
    tile = lambda i, j: (i, j, 0)
    return pl.pallas_call(
        kern,
        grid=(n // ns, t // tt),
        in_specs=[
            pl.BlockSpec((ns, tt, D_MODEL), tile),
            _layer_spec((1, D_MODEL), l, 2),
            pl.BlockSpec((None, D_MODEL, AB_OFF), lambda i, j: (l, 0, 0), pipeline_mode=pl.Buffered(1)),
            pl.BlockSpec((None, D_MODEL, LANES), lambda i, j: (l, 0, AB_OFF // LANES)),
            pl.BlockSpec((ns, CONV_W - 1, D_CONV), lambda i, j: (i, 0, 0)),
            pl.BlockSpec((ns, N_HEADS, HEAD_D, HEAD_D), lambda i, j: (i, 0, 0, 0)),
            _layer_spec((CONV_W, D_CONV), l, 2),
            _layer_spec((1, LANES), l, 2),
            _layer_spec((1, LANES), l, 2),
            _layer_spec((1, HEAD_D), l, 2),
        ],
        out_specs=[
            pl.BlockSpec((ns, tt, D_QK), tile),
            pl.BlockSpec((ns, CONV_W - 1, D_CONV), lambda i, j: (i, 0, 0)),
            pl.BlockSpec((ns, N_HEADS, HEAD_D, HEAD_D), lambda i, j: (i, 0, 0, 0)),
        ],
        out_shape=[
            jax.ShapeDtypeStruct((n, t, D_QK), BF16),
            jax.ShapeDtypeStruct((n, CONV_W - 1, D_CONV), F32),
            jax.ShapeDtypeStruct((n, N_HEADS, HEAD_D, HEAD_D), F32),
        ],
        scratch_shapes=[
            pltpu.VMEM((ns, N_HEADS, HEAD_D, HEAD_D), F32),
            pltpu.VMEM((ns, SUBLANES, D_CONV), F32),
            pltpu.VMEM((ns, tt, AB_OFF), F32),
            pltpu.VMEM((ns, tt, LANES), F32),
            pltpu.VMEM((ns, tt, LANES), F32),
        ],
        compiler_params=_params(("arbitrary", "arbitrary")),
        name="delta",
    )(x, p["norm_pre_mix"], p["w_lo"], p["w_ab"], conv_buf, s0, p["conv_w"], p["a_log"], p["dt_bias"],
      p["delta_norm_w"])


def _mix_kernel(x_ref, oa_ref, npre_ref, whi_ref, lnw_ref, lnb_ref, ws_ref, bs_ref,
                wa_ref, wb_ref, wo_ref, nw_ref, y_ref, v_ref, uv_ref, gate_ref, ob_ref, *, keep_all_v):
    c = MLP_CHUNK
    tm = x_ref.shape[0]
    n_ch = tm // c
    tn = D_MODEL // n_ch
    gw = 2 * D_MODEL // n_ch
    ri = lax.broadcasted_iota(jnp.int32, (c, c), 0)
    ci = lax.broadcasted_iota(jnp.int32, (c, c), 1)
    gd = D_B // N_GROUPS
    ws = [jnp.where(ri >= ci, ws_ref[g], 0.0).astype(BF16) for g in range(N_GROUPS)]
    h = _rms(x_ref[...], npre_ref[...]).astype(BF16)
    uv_ref[...] = _mm(h, whi_ref[:, :2 * D_B])
    pa = []
    for ch in range(n_ch):
        pa.append(_mm(oa_ref[...], wa_ref[:, ch * tn:(ch + 1) * tn]))
        g0 = 2 * D_B + ch * gw
        gate_ref[:, ch * gw:(ch + 1) * gw] = jax.nn.sigmoid(_mm(h, whi_ref[:, g0:g0 + gw]))
        rows = slice(ch * c, (ch + 1) * c)
        x = uv_ref[rows, :]
        gel = 0.5 * x * (1.0 + lax.erf(x * (2.0 ** -0.5)))
        u = gel[:, :D_B]
        v = gel[:, D_B:]
        vc = v - jnp.mean(v, axis=-1, keepdims=True)
        var = jnp.mean(jnp.square(vc), axis=-1, keepdims=True)
        v = vc * lax.rsqrt(var + 1e-5) * lnw_ref[...] + lnb_ref[...]
        if keep_all_v:
            v_ref[rows, :] = v
        elif ch == n_ch - 1:
            v_ref[0] = v
        for g in range(N_GROUPS):
            cols = slice(g * gd, (g + 1) * gd)
            mixed = _mm(ws[g], v[:, cols]) + bs_ref[:, g:g + 1]
            ob_ref[rows, cols] = (u[:, cols] * mixed).astype(BF16)
    pa = jnp.concatenate(pa, axis=1)
    pb = _mm(ob_ref[...], wb_ref[...])
    merged = gate_ref[:, :D_MODEL] * pa + gate_ref[:, D_MODEL:] * pb
    y = _mm(merged, wo_ref[...])
    y_ref[...] = x_ref[...] + _rms(y, nw_ref[...])


def _mix(oa, x2d, p, l, seq_len):
    m = x2d.shape[0]
    tm = min(TM, m)
    c = MLP_CHUNK
    row = lambda i: (i, 0)
    short = seq_len < c
    if short:
        assert c % seq_len == 0 and tm % c == 0
        ws_key, bs_key = "ws_short", "bs_short"
        v_spec = pl.BlockSpec((tm, D_B), row)
        v_shape = jax.ShapeDtypeStruct((m, D_B), F32)
    else:
        assert seq_len % tm == 0
        ws_key, bs_key = "w_spatial", "bs_t"
        tiles_per_seq = seq_len // tm
        v_spec = pl.BlockSpec((1, c, D_B), lambda i: (i // tiles_per_seq, 0, 0))
        v_shape = jax.ShapeDtypeStruct((m // seq_len, c, D_B), F32)
    resident = dict(pipeline_mode=pl.Buffered(1))
    wspec = pl.BlockSpec((None, D_MODEL, D_MODEL), lambda i: (l, 0, 0), **resident)
    return pl.pallas_call(
        functools.partial(_mix_kernel, keep_all_v=short),
        grid=(m // tm,),
        in_specs=[
            pl.BlockSpec((tm, D_MODEL), row),
            pl.BlockSpec((tm, D_QK), row),
            _layer_spec((1, D_MODEL), l, 1),
            pl.BlockSpec((None, D_MODEL, D_MAIN - AB_OFF), lambda i: (l, 0, 0), **resident),
            _layer_spec((1, D_B), l, 1),
            _layer_spec((1, D_B), l, 1),
            _layer_spec((N_GROUPS, c, c), l, 1),
            _layer_spec((c, LANES), l, 1),
            wspec, wspec, wspec,
            _layer_spec((1, D_MODEL), l, 1),
        ],
        out_specs=[pl.BlockSpec((tm, D_MODEL), row), v_spec],
        out_shape=[jax.ShapeDtypeStruct((m, D_MODEL), F32), v_shape],
        scratch_shapes=[pltpu.VMEM((tm, 2 * D_B), F32), pltpu.VMEM((tm, 2 * D_MODEL), F32),
                        pltpu.VMEM((tm, D_B), BF16)],
        compiler_params=_params(("arbitrary",)),
        name="mix",
    )(x2d, oa, p["norm_pre_mix"], p["w_hi"], p["sgu_ln_w"], p["sgu_ln_b"], p[ws_key], p[bs_key],
      p["w_proj_a"], p["w_proj_b"], p["w_out"], p["norm_post_mix"])


def _ffn_kernel(x_ref, npre_ref, wi_ref, wd_ref, npost_ref, y_ref, act_ref):
    x = x_ref[...]
    h = _rms(x, npre_ref[...]).astype(BF16)
    tf = TF_FFN
    for f in range(D_FF // tf):
        gate = _mm(h, wi_ref[:, f * tf:(f + 1) * tf])
        up = _mm(h, wi_ref[:, D_FF + f * tf:D_FF + (f + 1) * tf])
        act_ref[:, f * tf:(f + 1) * tf] = (gate * jax.nn.sigmoid(gate) * up).astype(BF16)
    y = _mm(act_ref[...], wd_ref[...])
    y_ref[...] = x + _rms(y, npost_ref[...])


def _ffn(x2d, p, l):
    m = x2d.shape[0]
    tm = min(TM_FFN, m)
    resident = dict(pipeline_mode=pl.Buffered(1))
    return pl.pallas_call(
        _ffn_kernel,
        grid=(m // tm,),
        in_specs=[
            pl.BlockSpec((tm, D_MODEL), lambda i: (i, 0)),
            _layer_spec((1, D_MODEL), l, 1),
            pl.BlockSpec((None, D_MODEL, 2 * D_FF), lambda i: (l, 0, 0), **resident),
            pl.BlockSpec((None, D_FF, D_MODEL), lambda i: (l, 0, 0), **resident),
            _layer_spec((1, D_MODEL), l, 1),
        ],
        out_specs=pl.BlockSpec((tm, D_MODEL), lambda i: (i, 0)),
        out_shape=jax.ShapeDtypeStruct((m, D_MODEL), F32),
        scratch_shapes=[pltpu.VMEM((tm, D_FF), BF16)],
        compiler_params=_params(("arbitrary",)),
        name="ffn",
    )(x2d, p["norm_pre_ffn"], p["w_ffn_in"], p["w_ffn_out"], p["norm_post_ffn"])


def _front_short(x, conv_all, s_all, p, l, prev):
    n, t, _ = x.shape
    proj, ab = _inproj(x.reshape(n * t, D_MODEL), p, l)
    return _delta_dec(proj.reshape(n, t, AB_OFF), ab.reshape(n, t, LANES), conv_all, s_all, p, l, prev)


def _trunk_layer(x, p, l, front):
    n, t, _ = x.shape
    m = n * t
    x2d = x.reshape(m, D_MODEL)
    o_a, new_buf, s_new = front(x)
    x1, v_rows = _mix(o_a.reshape(m, D_QK), x2d, p, l, t)
    x2 = _ffn(x1, p, l)
    return x2.reshape(n, t, D_MODEL), s_new, new_buf, v_rows.reshape(n, -1, D_B)


def _prepare_params(t_short, norm_pre_mix, w_in, conv_w, a_log, dt_bias, delta_norm_w, sgu_ln_w,
                    sgu_ln_b, w_spatial, b_spatial, w_proj_a, w_proj_b, w_out, norm_post_mix,
                    norm_pre_ffn, w_ffn_in, w_ffn_out, norm_post_ffn):
    depth = w_in.shape[0]
    w_bf = w_in.astype(BF16)
    w_hi = w_bf[:, :, AB_OFF + 2 * N_HEADS:]
    row = lambda v: v.reshape(depth, 1, -1)
    lanes = lambda v: jnp.pad(v, ((0, 0), (0, LANES - v.shape[1]))).reshape(depth, 1, LANES)
    bs_t = jnp.pad(jnp.swapaxes(b_spatial, 1, 2), ((0, 0), (0, 0), (0, LANES - N_GROUPS)))
    rep = MLP_CHUNK // t_short
    idx = jnp.arange(MLP_CHUNK)
    same_block = (idx[:, None] // t_short) == (idx[None, :] // t_short)
    onehot = (idx[:, None] % t_short == jnp.arange(t_short)[None, :]).astype(F32)
    tiled = jnp.einsum("ri,lgij,cj->lgrc", onehot, w_spatial[:, :, :t_short, :t_short], onehot,
                       precision=_HI)
    ws_short = jnp.where(same_block, tiled, 0.0)
    return dict(
        norm_pre_mix=row(norm_pre_mix), w_lo=w_bf, w_hi=w_hi, w_ab=w_bf,
        conv_w=conv_w, a_log=lanes(a_log), dt_bias=lanes(dt_bias), delta_norm_w=row(delta_norm_w),
        sgu_ln_w=row(sgu_ln_w), sgu_ln_b=row(sgu_ln_b), w_spatial=w_spatial, bs_t=bs_t,
        ws_short=ws_short, bs_short=jnp.tile(bs_t[:, :t_short], (1, rep, 1)),
        w_proj_a=w_proj_a.astype(BF16), w_proj_b=w_proj_b.astype(BF16), w_out=w_out.astype(BF16),
        norm_post_mix=row(norm_post_mix), norm_pre_ffn=row(norm_pre_ffn),
        w_ffn_in=w_ffn_in.astype(BF16), w_ffn_out=w_ffn_out.astype(BF16),
        norm_post_ffn=row(norm_post_ffn))


def kernel(x_prompt, x_sample, state_delta, state_conv, norm_pre_mix, w_in, conv_w, a_log, dt_bias,
           delta_norm_w, sgu_ln_w, sgu_ln_b, w_spatial, b_spatial, w_proj_a, w_proj_b, w_out,
           norm_post_mix, norm_pre_ffn, w_ffn_in, w_ffn_out, norm_post_ffn):
    depth = w_in.shape[0]
    nb, seq, _ = x_prompt.shape
    ndec, dec_seq, _ = x_sample.shape
    assert seq % DELTA_CHUNK == 0 and seq % MLP_CHUNK == 0
    assert dec_seq % SUBLANES == 0 and dec_seq < DELTA_CHUNK and ndec % DEC_SEQS_PER_STEP == 0
    p = _prepare_params(dec_seq, norm_pre_mix, w_in, conv_w, a_log, dt_bias, delta_norm_w, sgu_ln_w,
                        sgu_ln_b, w_spatial, b_spatial, w_proj_a, w_proj_b, w_out, norm_post_mix,
                        norm_pre_ffn, w_ffn_in, w_ffn_out, norm_post_ffn)
    y_p, y_s = x_prompt, x_sample
    conv0 = jnp.zeros((nb, CONV_W - 1, D_CONV), x_prompt.dtype)
    s_zero = jnp.zeros((nb, N_HEADS, HEAD_D, HEAD_D), state_delta.dtype)
    sd_p, sc_p, cv_p, cv_s = [], [], [], []
    prev_s = None
    for l in range(depth):
        y_p, s_new, buf_new, v_rows = _trunk_layer(
            y_p, p, l, lambda x: _delta(x, conv0, s_zero, p, l))
        sd_p.append(s_new)
        sc_p.append(buf_new)
        cv_p.append(v_rows)
        y_s, sd_s, sc_s, v_rows = _trunk_layer(
            y_s, p, l, lambda x: _front_short(x, state_conv, state_delta, p, l, prev_s))
        prev_s = (sc_s, sd_s)
        cv_s.append(v_rows)
    return (y_p, y_s, jnp.stack(sd_p), jnp.stack(sc_p), jnp.stack(cv_p),
            sd_s, sc_s, jnp.stack(cv_s))
```

```python
import functools

import jax
import jax.numpy as jnp
from jax import lax
from jax.experimental import pallas as pl
from jax.experimental.pallas import tpu as pltpu

F32 = jnp.float32
BF16 = jnp.bfloat16

D_MODEL = 1024
N_HEADS = 8
HEAD_D = 128
D_QK = N_HEADS * HEAD_D
D_CONV = 3 * D_QK
CONV_W = 4
DELTA_CHUNK = 64
MLP_CHUNK = 128
N_GROUPS = 8
D_B = 1024
D_FF = 2816
D_MAIN = D_CONV + D_QK + 2 * D_B + 2 * D_MODEL
AB_OFF = D_CONV + D_QK
LANES = 128
SUBLANES = 8

TM = 512
TM_FFN = 1024
TF_FFN = 256
TT_DELTA = 128
SEQS_DELTA = 4
DEC_SEQS_PER_STEP = 8
VMEM_LIMIT = 48 * 1024 * 1024

_HI = lax.Precision.HIGHEST
_NT = (((1,), (1,)), ((), ()))
_TN = (((0,), (0,)), ((), ()))
_NN = (((1,), (0,)), ((), ()))


def _mm(a, b, dims=_NN):
    return lax.dot_general(a.astype(BF16), b.astype(BF16), dims, preferred_element_type=F32)


def _mm_hi(a, b, dims=_NN):
    return lax.dot_general(a, b, dims, precision=_HI, preferred_element_type=F32)


def _rms(x, w, eps=1e-6):
    return x * lax.rsqrt(jnp.mean(jnp.square(x), axis=-1, keepdims=True) + eps) * w


def _params(sem):
    return pltpu.CompilerParams(dimension_semantics=sem, vmem_limit_bytes=VMEM_LIMIT)


def _layer_spec(shape, l, ngrid):
    zeros = (0,) * len(shape)
    if ngrid == 1:
        return pl.BlockSpec((None,) + shape, lambda i: (l,) + zeros)
    return pl.BlockSpec((None,) + shape, lambda i, j: (l,) + zeros)


def _inproj_kernel(x_ref, nw_ref, wlo_ref, wab_ref, o_ref, ab_ref):
    h = _rms(x_ref[...], nw_ref[...]).astype(BF16)
    ab_ref[...] = _mm(h, wab_ref[...])
    o_ref[...] = _mm(h, wlo_ref[...])


def _inproj(x2d, p, l):
    m = x2d.shape[0]
    tm = min(TM, m)
    return pl.pallas_call(
        _inproj_kernel,
        grid=(m // tm,),
        in_specs=[
            pl.BlockSpec((tm, D_MODEL), lambda i: (i, 0)),
            _layer_spec((1, D_MODEL), l, 1),
            pl.BlockSpec((None, D_MODEL, AB_OFF), lambda i: (l, 0, 0), pipeline_mode=pl.Buffered(1)),
            pl.BlockSpec((None, D_MODEL, LANES), lambda i: (l, 0, AB_OFF // LANES)),
        ],
        out_specs=[
            pl.BlockSpec((tm, AB_OFF), lambda i: (i, 0)),
            pl.BlockSpec((tm, LANES), lambda i: (i, 0)),
        ],
        out_shape=[
            jax.ShapeDtypeStruct((m, AB_OFF), F32),
            jax.ShapeDtypeStruct((m, LANES), F32),
        ],
        compiler_params=_params(("arbitrary",)),
        name="inproj",
    )(x2d, p["norm_pre_mix"], p["w_lo"], p["w_ab"])


def _conv_silu_norm(y, part):
    y = y * jax.nn.sigmoid(y)
    if part < 2:
        inv = lax.rsqrt(jnp.sum(jnp.square(y), axis=-1, keepdims=True) + 1e-6)
        if part == 0:
            inv = inv * (HEAD_D ** -0.5)
        y = y * inv
    return y


def _causal_conv(xg, w_of_tap):
    sub = lax.broadcasted_iota(jnp.int32, (xg.shape[0] - 1,) + xg.shape[1:], 1)
    y = None
    for i in range(CONV_W):
        k = CONV_W - 1 - i
        if k == 0:
            tap = xg[1:]
        else:
            rot = pltpu.roll(xg, k, axis=1)
            tap = jnp.where(sub >= k, rot[1:], rot[:-1])
        term = w_of_tap(i) * tap
        y = term if y is None else y + term
    return y


def _delta_dec_kernel(qkv_ref, z_ref, ab_ref, cbuf_ref, s0_ref, cw_ref, alog_ref, dtb_ref, nw_ref,
                      *rest, nb, c, aliased):
    if aliased:
        rest = rest[2:]
    o_ref, nbuf_ref, snew_ref, xbuf_ref, qkvc_ref = rest
    pad = SUBLANES
    hist = CONV_W - 1
    xbuf_ref[:, pad - hist:pad, :] = cbuf_ref[...]
    xbuf_ref[:, pad:pad + c, :] = qkv_ref[...]
    for part in range(3):
        for h in range(N_HEADS):
            c0 = part * D_QK + h * HEAD_D
            cols = slice(c0, c0 + HEAD_D)
            y = cw_ref[0:1, cols] * xbuf_ref[:, pad - hist:pad - hist + c, cols]
            for i in range(1, CONV_W):
                y = y + cw_ref[i:i + 1, cols] * xbuf_ref[:, pad - hist + i:pad - hist + i + c, cols]
            qkvc_ref[:, :, cols] = _conv_silu_norm(y, part)
    nbuf_ref[...] = xbuf_ref[:, pad + c - hist:pad + c, :]

    ri = lax.broadcasted_iota(jnp.int32, (c, c), 0)
    ci = lax.broadcasted_iota(jnp.int32, (c, c), 1)
    causal = ri >= ci
    strict = ri > ci
    diag = ri == ci
    tri = jnp.where(causal, 1.0, 0.0).astype(F32)
    eye = jnp.where(diag, 1.0, 0.0).astype(F32)
    n_levels = c.bit_length() - 2
    items = [(i, h) for i in range(nb) for h in range(N_HEADS)]
    idx = {it: n for n, it in enumerate(items)}
    every = range(len(items))

    gc, beta = [], []
    for i in range(nb):
        g = -jnp.exp(alog_ref[...]) * jax.nn.softplus(ab_ref[i] + dtb_ref[...])
        gc.append(_mm_hi(tri, g))
        beta.append(jax.nn.sigmoid(ab_ref[i]))

    def col(part, i, h):
        return qkvc_ref[i, :, part * D_QK + h * HEAD_D:part * D_QK + (h + 1) * HEAD_D]

    gcol = [gc[i][:, h:h + 1] for i, h in items]
    bcol = [beta[i][:, N_HEADS + h:N_HEADS + h + 1] for i, h in items]
    kb = [col(1, i, h) * bcol[idx[i, h]] for i, h in items]
    kq = [_mm(jnp.concatenate([kb[idx[i, h]], col(0, i, h)], axis=0), col(1, i, h), _NT)
          for i, h in items]
    x, qk, tk = [], [], []
    for n in every:
        grow = jnp.sum(jnp.where(diag, gcol[n], 0.0), axis=0, keepdims=True)
        dec = jnp.where(causal, jnp.exp(jnp.where(causal, gcol[n] - grow, 0.0)), 0.0)
        x.append(jnp.where(strict, -(kq[n][:c] * dec), 0.0))
        qk.append(kq[n][c:] * dec)
        tk.append(eye + x[n])
    pk = [_mm(x[n], x[n]) for n in every]
    for lvl in range(1, n_levels + 1):
        if lvl < n_levels:
            r = [_mm(jnp.concatenate([pk[n], tk[n]], axis=0), pk[n]) for n in every]
            pk = [r[n][:c] for n in every]
            tk = [tk[n] + r[n][c:] for n in every]
        else:
            tk = [tk[n] + _mm(tk[n], pk[n]) for n in every]
    e = [jnp.exp(gcol[n]) for n in every]
    sol = [_mm(tk[idx[i, h]], jnp.concatenate([col(2, i, h) * bcol[idx[i, h]],
                                                kb[idx[i, h]] * e[idx[i, h]]], axis=1))
           for i, h in items]
    wq = [_mm(jnp.concatenate([sol[idx[i, h]][:, HEAD_D:], col(0, i, h) * e[idx[i, h]]], axis=0),
              s0_ref[i, h]) for i, h in items]
    vn = [sol[n][:, :HEAD_D] - wq[n][:c] for n in every]
    op = [_mm(qk[n], vn[n]) for n in every]
    for i, h in items:
        n = idx[i, h]
        glast = gc[i][c - 1:c, h:h + 1]
        kdec = col(1, i, h) * jnp.exp(glast - gcol[n])
        snew_ref[i, h] = s0_ref[i, h] * jnp.exp(glast) + _mm(kdec, vn[n], _TN)
    for i, h in items:
        n = idx[i, h]
        o = _rms(wq[n][c:] + op[n], nw_ref[...])
        cols = slice(h * HEAD_D, (h + 1) * HEAD_D)
        zz = z_ref[i, :, cols]
        o_ref[i, :, cols] = o * (zz * jax.nn.sigmoid(zz))


def _delta_dec(proj3, ab3, conv_all, s_all, p, l, prev):
    n, c, _ = proj3.shape
    depth = s_all.shape[0]
    nb = DEC_SEQS_PER_STEP
    aliased = prev is not None
    kern = functools.partial(_delta_dec_kernel, nb=nb, c=c, aliased=aliased)
    state_blk = (None, nb, N_HEADS, HEAD_D, HEAD_D)
    in_specs = [
        pl.BlockSpec((nb, c, D_CONV), lambda i: (i, 0, 0)),
        pl.BlockSpec((nb, c, D_QK), lambda i: (i, 0, D_CONV // D_QK)),
        pl.BlockSpec((nb, c, LANES), lambda i: (i, 0, 0)),
        pl.BlockSpec((None, nb, CONV_W - 1, D_CONV), lambda i: (l, i, 0, 0)),
        pl.BlockSpec(state_blk, lambda i: (l, i, 0, 0, 0)),
        _layer_spec((CONV_W, D_CONV), l, 1),
        _layer_spec((1, LANES), l, 1),
        _layer_spec((1, LANES), l, 1),
        _layer_spec((1, HEAD_D), l, 1),
    ]
    args = [proj3, proj3, ab3, conv_all, s_all, p["conv_w"], p["a_log"], p["dt_bias"], p["delta_norm_w"]]
    aliases = {}
    if aliased:
        in_specs += [pl.BlockSpec(memory_space=pl.ANY), pl.BlockSpec(memory_space=pl.ANY)]
        args += list(prev)
        aliases = {len(args) - 2: 1, len(args) - 1: 2}
    return pl.pallas_call(
        kern,
        grid=(n // nb,),
        in_specs=in_specs,
        out_specs=[
            pl.BlockSpec((nb, c, D_QK), lambda i: (i, 0, 0)),
            pl.BlockSpec((None, nb, CONV_W - 1, D_CONV), lambda i: (l, i, 0, 0)),
            pl.BlockSpec(state_blk, lambda i: (l, i, 0, 0, 0)),
        ],
        out_shape=[
            jax.ShapeDtypeStruct((n, c, D_QK), F32),
            jax.ShapeDtypeStruct((depth, n, CONV_W - 1, D_CONV), F32),
            jax.ShapeDtypeStruct((depth, n, N_HEADS, HEAD_D, HEAD_D), F32),
        ],
        scratch_shapes=[
            pltpu.VMEM((nb, c + SUBLANES, D_CONV), F32),
            pltpu.VMEM((nb, c, D_CONV), F32),
        ],
        input_output_aliases=aliases,
        compiler_params=_params(("arbitrary",)),
        name="delta_dec",
    )(*args)


def _delta_chunk(refs, ic, c):
    qkvc_ref, gc_ref, beta_ref, s_ref, nw_ref, o_ref = refs
    assert 2 * c == LANES
    ns = gc_ref.shape[0]
    ri = lax.broadcasted_iota(jnp.int32, (c, LANES), 0)
    lane = lax.broadcasted_iota(jnp.int32, (c, LANES), 1)
    half = lane >= c
    cj = jnp.where(half, lane - c, lane)
    causal = ri >= cj
    strict = ri > cj
    diag = ri == cj
    eye = jnp.where(diag, 1.0, 0.0).astype(F32)
    zeros = jnp.zeros((c, HEAD_D), F32)
    zeros2 = jnp.zeros((c, 2 * HEAD_D), F32)
    n_levels = c.bit_length() - 2

    def blockdiag(m):
        return jnp.concatenate([jnp.where(half, 0.0, m), jnp.where(half, m, 0.0)], axis=0)

    def side_by_side(a, b):
        return jnp.concatenate([jnp.concatenate([a, zeros], axis=1),
                                jnp.concatenate([zeros, b], axis=1)], axis=0)

    rows = pl.ds(pl.multiple_of(ic * c, c), c)
    heads = [(s, h) for s in range(ns) for h in range(N_HEADS)]
    pairs = [(s, pr) for s in range(ns) for pr in range(N_HEADS // 2)]
    gc = [gc_ref[s, rows, :] for s in range(ns)]
    beta = [beta_ref[s, rows, :] for s in range(ns)]

    def col(part, s, h):
        return qkvc_ref[s, rows, part * D_QK + h * HEAD_D:part * D_QK + (h + 1) * HEAD_D]

    g = {(s, h): jnp.broadcast_to(gc[s][:, h:h + 1], (c, HEAD_D)) for s, h in heads}
    b = {(s, h): jnp.broadcast_to(beta[s][:, N_HEADS + h:N_HEADS + h + 1], (c, HEAD_D))
         for s, h in heads}
    kb = {(s, h): col(1, s, h) * b[s, h] for s, h in heads}
    kq = {}
    for s, pr in pairs:
        h1, h2 = 2 * pr, 2 * pr + 1
        lhs = jnp.concatenate([jnp.concatenate([kb[s, h1], kb[s, h2]], axis=1),
                               jnp.concatenate([col(0, s, h1), col(0, s, h2)], axis=1)], axis=0)
        kq[s, pr] = _mm(lhs, side_by_side(col(1, s, h1), col(1, s, h2)), _NT)
    x, qkd, tk = {}, {}, {}
    for s, pr in pairs:
        gcp = jnp.where(half, g[s, 2 * pr + 1], g[s, 2 * pr])
        rowp = jnp.sum(jnp.where(diag, gcp, 0.0), axis=0, keepdims=True)
        dec = jnp.where(causal, jnp.exp(jnp.where(causal, gcp - rowp, 0.0)), 0.0)
        x[s, pr] = jnp.where(strict, -(kq[s, pr][:c] * dec), 0.0)
        qkd[s, pr] = kq[s, pr][c:] * dec
        tk[s, pr] = eye + x[s, pr]
    pk = {sp: _mm(x[sp], blockdiag(x[sp])) for sp in pairs}
    for lvl in range(1, n_levels + 1):
        if lvl < n_levels:
            r = {sp: _mm(jnp.concatenate([pk[sp], tk[sp]], axis=0), blockdiag(pk[sp])) for sp in pairs}
            pk = {sp: r[sp][:c] for sp in pairs}
            tk = {sp: tk[sp] + r[sp][c:] for sp in pairs}
        else:
            tk = {sp: tk[sp] + _mm(tk[sp], blockdiag(pk[sp])) for sp in pairs}
    e = {sh: jnp.exp(g[sh]) for sh in heads}
    sol = {}
    for s, h in heads:
        rhs = jnp.concatenate([col(2, s, h) * b[s, h], kb[s, h] * e[s, h]], axis=1)
        rhs = jnp.concatenate([rhs, zeros2] if h % 2 == 0 else [zeros2, rhs], axis=0)
        sol[s, h] = _mm(tk[s, h // 2], rhs)
    wq = {(s, h): _mm(jnp.concatenate([sol[s, h][:, HEAD_D:], col(0, s, h) * e[s, h]], axis=0),
                      s_ref[s, h]) for s, h in heads}
    vn = {sh: sol[sh][:, :HEAD_D] - wq[sh][:c] for sh in heads}
    op = {(s, pr): _mm(qkd[s, pr], side_by_side(vn[s, 2 * pr], vn[s, 2 * pr + 1]))
          for s, pr in pairs}
    for s, h in heads:
        glast = gc[s][c - 1:c, h:h + 1]
        kdec = col(1, s, h) * jnp.exp(glast - g[s, h])
        s_ref[s, h] = s_ref[s, h] * jnp.exp(glast) + _mm(kdec, vn[s, h], _TN)
    for s, h in heads:
        o = _rms(wq[s, h][c:] + op[s, h // 2][:, (h % 2) * HEAD_D:(h % 2 + 1) * HEAD_D], nw_ref[...])
        cols = slice(h * HEAD_D, (h + 1) * HEAD_D)
        zz = qkvc_ref[s, rows, D_CONV + h * HEAD_D:D_CONV + (h + 1) * HEAD_D]
        o_ref[s, rows, cols] = (o * (zz * jax.nn.sigmoid(zz))).astype(o_ref.dtype)


def _delta_kernel(x_ref, npre_ref, wlo_ref, wab_ref, cbuf_ref, s0_ref, cw_ref, alog_ref, dtb_ref, nw_ref,
                  o_ref, nbuf_ref, snew_ref,
                  s_ref, hist_ref, qkvc_ref, gc_ref, beta_ref, *, tt, c):
    t = pl.program_id(1)
    nt = pl.num_programs(1)
    ns = x_ref.shape[0]
    hist = CONV_W - 1

    @pl.when(t == 0)
    def _():
        s_ref[...] = s0_ref[...]
        hist_ref[...] = jnp.zeros(hist_ref.shape, F32)
        hist_ref[:, SUBLANES - hist:, :] = cbuf_ref[...]

    hn = _rms(x_ref[...].reshape(ns * tt, D_MODEL), npre_ref[...]).astype(BF16)
    qkvc_ref[...] = _mm(hn, wlo_ref[...]).reshape(ns, tt, AB_OFF)
    ab_all = _mm(hn, wab_ref[...]).reshape(ns, tt, LANES)

    rt = lax.broadcasted_iota(jnp.int32, (c, c), 0)
    ct = lax.broadcasted_iota(jnp.int32, (c, c), 1)
    tri = jnp.where(rt >= ct, 1.0, 0.0).astype(F32)
    for s in range(ns):
        ab = ab_all[s]
        g = -jnp.exp(alog_ref[...]) * jax.nn.softplus(ab + dtb_ref[...])
        beta_ref[s] = jax.nn.sigmoid(ab)
        for ic in range(tt // c):
            gc_ref[s, ic * c:(ic + 1) * c, :] = _mm_hi(tri, g[ic * c:(ic + 1) * c, :])

    for s in range(ns):
        for part in range(3):
            for h in range(N_HEADS):
                c0 = part * D_QK + h * HEAD_D
                cols = slice(c0, c0 + HEAD_D)
                raw = qkvc_ref[s, :, cols]
                xg = jnp.concatenate([hist_ref[s, :, cols], raw], axis=0)
                xg = xg.reshape(tt // SUBLANES + 1, SUBLANES, HEAD_D)
                y = _causal_conv(xg, lambda i: cw_ref[i:i + 1, cols])
                qkvc_ref[s, :, cols] = _conv_silu_norm(y, part).reshape(tt, HEAD_D)
                hist_ref[s, :, cols] = raw[tt - SUBLANES:, :]

    @pl.when(t == nt - 1)
    def _():
        nbuf_ref[...] = hist_ref[:, SUBLANES - hist:, :]

    refs = (qkvc_ref, gc_ref, beta_ref, s_ref, nw_ref, o_ref)

    def body(ic, carry):
        _delta_chunk(refs, ic, c)
        return carry

    lax.fori_loop(0, tt // c, body, 0)

    @pl.when(t == nt - 1)
    def _():
        snew_ref[...] = s_ref[...]


def _delta(x, conv_buf, s0, p, l):
    n, t, _ = x.shape
    tt, c = min(TT_DELTA, t), DELTA_CHUNK
    ns = min(SEQS_DELTA, n)
    kern = functools.partial(_delta_kernel, tt=tt, c=c)
    tile = lambda i, j: (i, j, 0)
    return pl.pallas_call(
        kern,
        grid=(n // ns, t // tt),
        in_specs=[
            pl.BlockSpec((ns, tt, D_MODEL), tile),
            _layer_spec((1, D_MODEL), l, 2),
            pl.BlockSpec((None, D_MODEL, AB_OFF), lambda i, j: (l, 0, 0), pipeline_mode=pl.Buffered(1)),
            pl.BlockSpec((None, D_MODEL, LANES), lambda i, j: (l, 0, AB_OFF // LANES)),
            pl.BlockSpec((ns, CONV_W - 1, D_CONV), lambda i, j: (i, 0, 0)),
            pl.BlockSpec((ns, N_HEADS, HEAD_D, HEAD_D), lambda i, j: (i, 0, 0, 0)),
            _layer_spec((CONV_W, D_CONV), l, 2),
            _layer_spec((1, LANES), l, 2),
            _layer_spec((1, LANES), l, 2),
            _layer_spec((1, HEAD_D), l, 2),
        ],
        out_specs=[
            pl.BlockSpec((ns, tt, D_QK), tile),
            pl.BlockSpec((ns, CONV_W - 1, D_CONV), lambda i, j: (i, 0, 0)),
            pl.BlockSpec((ns, N_HEADS, HEAD_D, HEAD_D), lambda i, j: (i, 0, 0, 0)),
        ],
        out_shape=[
            jax.ShapeDtypeStruct((n, t, D_QK), BF16),
            jax.ShapeDtypeStruct((n, CONV_W - 1, D_CONV), F32),
            jax.ShapeDtypeStruct((n, N_HEADS, HEAD_D, HEAD_D), F32),
        ],
        scratch_shapes=[
            pltpu.VMEM((ns, N_HEADS, HEAD_D, HEAD_D), F32),
            pltpu.VMEM((ns, SUBLANES, D_CONV), F32),
            pltpu.VMEM((ns, tt, AB_OFF), F32),
            pltpu.VMEM((ns, tt, LANES), F32),
            pltpu.VMEM((ns, tt, LANES), F32),
        ],
        compiler_params=_params(("arbitrary", "arbitrary")),
        name="delta",
    )(x, p["norm_pre_mix"], p["w_lo"], p["w_ab"], conv_buf, s0, p["conv_w"], p["a_log"], p["dt_bias"],
      p["delta_norm_w"])


def _mix_kernel(x_ref, oa_ref, npre_ref, whi_ref, lnw_ref, lnb_ref, ws_ref, bs_ref,
                wa_ref, wb_ref, wo_ref, nw_ref, y_ref, v_ref, uv_ref, gate_ref, ob_ref, *, keep_all_v):
    c = MLP_CHUNK
    tm = x_ref.shape[0]
    n_ch = tm // c
    tn = D_MODEL // n_ch
    gw = 2 * D_MODEL // n_ch
    ri = lax.broadcasted_iota(jnp.int32, (c, c), 0)
    ci = lax.broadcasted_iota(jnp.int32, (c, c), 1)
    gd = D_B // N_GROUPS
    ws = [jnp.where(ri >= ci, ws_ref[g], 0.0).astype(BF16) for g in range(N_GROUPS)]
    h = _rms(x_ref[...], npre_ref[...]).astype(BF16)
    uv_ref[...] = _mm(h, whi_ref[:, :2 * D_B])
    pa = []
    for ch in range(n_ch):
        pa.append(_mm(oa_ref[...], wa_ref[:, ch * tn:(ch + 1) * tn]))
        g0 = 2 * D_B + ch * gw
        gate_ref[:, ch * gw:(ch + 1) * gw] = jax.nn.sigmoid(_mm(h, whi_ref[:, g0:g0 + gw]))
        rows = slice(ch * c, (ch + 1) * c)
        x = uv_ref[rows, :]
        gel = 0.5 * x * (1.0 + lax.erf(x * (2.0 ** -0.5)))
        u = gel[:, :D_B]
        v = gel[:, D_B:]
        vc = v - jnp.mean(v, axis=-1, keepdims=True)
        var = jnp.mean(jnp.square(vc), axis=-1, keepdims=True)
        v = vc * lax.rsqrt(var + 1e-5) * lnw_ref[...] + lnb_ref[...]
        if keep_all_v:
            v_ref[rows, :] = v
        elif ch == n_ch - 1:
            v_ref[0] = v
        for g in range(N_GROUPS):
            cols = slice(g * gd, (g + 1) * gd)
            mixed = _mm(ws[g], v[:, cols]) + bs_ref[:, g:g + 1]
            ob_ref[rows, cols] = (u[:, cols] * mixed).astype(BF16)
    pa = jnp.concatenate(pa, axis=1)
    pb = _mm(ob_ref[...], wb_ref[...])
    merged = gate_ref[:, :D_MODEL] * pa + gate_ref[:, D_MODEL:] * pb
    y = _mm(merged, wo_ref[...])
    y_ref[...] = x_ref[...] + _rms(y, nw_ref[...])


def _mix(oa, x2d, p, l, seq_len):
    m = x2d.shape[0]
    tm = min(TM, m)
    c = MLP_CHUNK
    row = lambda i: (i, 0)
    short = seq_len < c
    if short:
        assert c % seq_len == 0 and tm % c == 0
        ws_key, bs_key = "ws_short", "bs_short"
        v_spec = pl.BlockSpec((tm, D_B), row)
        v_shape = jax.ShapeDtypeStruct((m, D_B), F32)
    else:
        assert seq_len % tm == 0
        ws_key, bs_key = "w_spatial", "bs_t"
        tiles_per_seq = seq_len // tm
        v_spec = pl.BlockSpec((1, c, D_B), lambda i: (i // tiles_per_seq, 0, 0))
        v_shape = jax.ShapeDtypeStruct((m // seq_len, c, D_B), F32)
    resident = dict(pipeline_mode=pl.Buffered(1))
    wspec = pl.BlockSpec((None, D_MODEL, D_MODEL), lambda i: (l, 0, 0), **resident)
    return pl.pallas_call(
        functools.partial(_mix_kernel, keep_all_v=short),
        grid=(m // tm,),
        in_specs=[
            pl.BlockSpec((tm, D_MODEL), row),
            pl.BlockSpec((tm, D_QK), row),
            _layer_spec((1, D_MODEL), l, 1),
            pl.BlockSpec((None, D_MODEL, D_MAIN - AB_OFF), lambda i: (l, 0, 0), **resident),
            _layer_spec((1, D_B), l, 1),
            _layer_spec((1, D_B), l, 1),
            _layer_spec((N_GROUPS, c, c), l, 1),
            _layer_spec((c, LANES), l, 1),
            wspec, wspec, wspec,
            _layer_spec((1, D_MODEL), l, 1),
        ],
        out_specs=[pl.BlockSpec((tm, D_MODEL), row), v_spec],
        out_shape=[jax.ShapeDtypeStruct((m, D_MODEL), F32), v_shape],
        scratch_shapes=[pltpu.VMEM((tm, 2 * D_B), F32), pltpu.VMEM((tm, 2 * D_MODEL), F32),
                        pltpu.VMEM((tm, D_B), BF16)],
        compiler_params=_params(("arbitrary",)),
        name="mix",
    )(x2d, oa, p["norm_pre_mix"], p["w_hi"], p["sgu_ln_w"], p["sgu_ln_b"], p[ws_key], p[bs_key],
      p["w_proj_a"], p["w_proj_b"], p["w_out"], p["norm_post_mix"])


def _ffn_kernel(x_ref, npre_ref, wi_ref, wd_ref, npost_ref, y_ref, act_ref):
    tm = x_ref.shape[0]
    tf = TF_FFN
    halves = (slice(0, tm // 2), slice(tm // 2, tm)) if tm % (4 * SUBLANES) == 0 else (slice(0, tm),)

    def pre(r):
        return _rms(x_ref[r, :], npre_ref[...]).astype(BF16)

    def chunk(h, r, f):
        gate = _mm(h, wi_ref[:, f * tf:(f + 1) * tf])
        up = _mm(h, wi_ref[:, D_FF + f * tf:D_FF + (f + 1) * tf])
        act_ref[r, f * tf:(f + 1) * tf] = (gate * jax.nn.sigmoid(gate) * up).astype(BF16)

    def post(r, y):
        y_ref[r, :] = x_ref[r, :] + _rms(y, npost_ref[...])

    n_f = D_FF // tf
    h = pre(halves[0])
    y_prev, r_prev = None, None
    for idx, r in enumerate(halves):
        h_next = None
        for f in range(n_f):
            chunk(h, r, f)
            if f == 0 and idx + 1 < len(halves):
                h_next = pre(halves[idx + 1])
            if f == 0 and y_prev is not None:
                post(r_prev, y_prev)
        y_prev, r_prev = _mm(act_ref[r, :], wd_ref[...]), r
        h = h_next
    post(r_prev, y_prev)


def _ffn(x2d, p, l):
    m = x2d.shape[0]
    tm = min(TM_FFN, m)
    resident = dict(pipeline_mode=pl.Buffered(1))
    return pl.pallas_call(
        _ffn_kernel,
        grid=(m // tm,),
        in_specs=[
            pl.BlockSpec((tm, D_MODEL), lambda i: (i, 0)),
            _layer_spec((1, D_MODEL), l, 1),
            pl.BlockSpec((None, D_MODEL, 2 * D_FF), lambda i: (l, 0, 0), **resident),
            pl.BlockSpec((None, D_FF, D_MODEL), lambda i: (l, 0, 0), **resident),
            _layer_spec((1, D_MODEL), l, 1),
        ],
        out_specs=pl.BlockSpec((tm, D_MODEL), lambda i: (i, 0)),
        out_shape=jax.ShapeDtypeStruct((m, D_MODEL), F32),
        scratch_shapes=[pltpu.VMEM((tm, D_FF), BF16)],
        compiler_params=_params(("arbitrary",)),
        name="ffn",
    )(x2d, p["norm_pre_ffn"], p["w_ffn_in"], p["w_ffn_out"], p["norm_post_ffn"])


def _front_short(x, conv_all, s_all, p, l, prev):
    n, t, _ = x.shape
    proj, ab = _inproj(x.reshape(n * t, D_MODEL), p, l)
    return _delta_dec(proj.reshape(n, t, AB_OFF), ab.reshape(n, t, LANES), conv_all, s_all, p, l, prev)


def _trunk_layer(x, p, l, front):
    n, t, _ = x.shape
    m = n * t
    x2d = x.reshape(m, D_MODEL)
    o_a, new_buf, s_new = front(x)
    x1, v_rows = _mix(o_a.reshape(m, D_QK), x2d, p, l, t)
    x2 = _ffn(x1, p, l)
    return x2.reshape(n, t, D_MODEL), s_new, new_buf, v_rows.reshape(n, -1, D_B)


def _prepare_params(t_short, norm_pre_mix, w_in, conv_w, a_log, dt_bias, delta_norm_w, sgu_ln_w,
                    sgu_ln_b, w_spatial, b_spatial, w_proj_a, w_proj_b, w_out, norm_post_mix,
                    norm_pre_ffn, w_ffn_in, w_ffn_out, norm_post_ffn):
    depth = w_in.shape[0]
    w_bf = w_in.astype(BF16)
    w_hi = w_bf[:, :, AB_OFF + 2 * N_HEADS:]
    row = lambda v: v.reshape(depth, 1, -1)
    lanes = lambda v: jnp.pad(v, ((0, 0), (0, LANES - v.shape[1]))).reshape(depth, 1, LANES)
    bs_t = jnp.pad(jnp.swapaxes(b_spatial, 1, 2), ((0, 0), (0, 0), (0, LANES - N_GROUPS)))
    rep = MLP_CHUNK // t_short
    idx = jnp.arange(MLP_CHUNK)
    same_block = (idx[:, None] // t_short) == (idx[None, :] // t_short)
    onehot = (idx[:, None] % t_short == jnp.arange(t_short)[None, :]).astype(F32)
    tiled = jnp.einsum("ri,lgij,cj->lgrc", onehot, w_spatial[:, :, :t_short, :t_short], onehot,
                       precision=_HI)
    ws_short = jnp.where(same_block, tiled, 0.0)
    return dict(
        norm_pre_mix=row(norm_pre_mix), w_lo=w_bf, w_hi=w_hi, w_ab=w_bf,
        conv_w=conv_w, a_log=lanes(a_log), dt_bias=lanes(dt_bias), delta_norm_w=row(delta_norm_w),
        sgu_ln_w=row(sgu_ln_w), sgu_ln_b=row(sgu_ln_b), w_spatial=w_spatial, bs_t=bs_t,
        ws_short=ws_short, bs_short=jnp.tile(bs_t[:, :t_short], (1, rep, 1)),
        w_proj_a=w_proj_a.astype(BF16), w_proj_b=w_proj_b.astype(BF16), w_out=w_out.astype(BF16),
        norm_post_mix=row(norm_post_mix), norm_pre_ffn=row(norm_pre_ffn),
        w_ffn_in=w_ffn_in.astype(BF16), w_ffn_out=w_ffn_out.astype(BF16),
        norm_post_ffn=row(norm_post_ffn))


def kernel(x_prompt, x_sample, state_delta, state_conv, norm_pre_mix, w_in, conv_w, a_log, dt_bias,
           delta_norm_w, sgu_ln_w, sgu_ln_b, w_spatial, b_spatial, w_proj_a, w_proj_b, w_out,
           norm_post_mix, norm_pre_ffn, w_ffn_in, w_ffn_out, norm_post_ffn):
    depth = w_in.shape[0]
    nb, seq, _ = x_prompt.shape
    ndec, dec_seq, _ = x_sample.shape
    assert seq % DELTA_CHUNK == 0 and seq % MLP_CHUNK == 0
    assert dec_seq % SUBLANES == 0 and dec_seq < DELTA_CHUNK and ndec % DEC_SEQS_PER_STEP == 0
    p = _prepare_params(dec_seq, norm_pre_mix, w_in, conv_w, a_log, dt_bias, delta_norm_w, sgu_ln_w,
                        sgu_ln_b, w_spatial, b_spatial, w_proj_a, w_proj_b, w_out, norm_post_mix,
                        norm_pre_ffn, w_ffn_in, w_ffn_out, norm_post_ffn)
    y_p, y_s = x_prompt, x_sample
    conv0 = jnp.zeros((nb, CONV_W - 1, D_CONV), x_prompt.dtype)
    s_zero = jnp.zeros((nb, N_HEADS, HEAD_D, HEAD_D), state_delta.dtype)
    sd_p, sc_p, cv_p, cv_s = [], [], [], []
    prev_s = None
    for l in range(depth):
        y_p, s_new, buf_new, v_rows = _trunk_layer(
            y_p, p, l, lambda x: _delta(x, conv0, s_zero, p, l))
        sd_p.append(s_new)
        sc_p.append(buf_new)
        cv_p.append(v_rows)
        y_s, sd_s, sc_s, v_rows = _trunk_layer(
            y_s, p, l, lambda x: _front_short(x, state_conv, state_delta, p, l, prev_s))
        prev_s = (sc_s, sd_s)
        cv_s.append(v_rows)
    return (y_p, y_s, jnp.stack(sd_p), jnp.stack(sc_p), jnp.stack(cv_p),
            sd_s, sc_s, jnp.stack(cv_s))
```

```python
import functools

import jax
import jax.numpy as jnp
from jax import lax
from jax.experimental import pallas as pl
from jax.experimental.pallas import tpu as pltpu

F32 = jnp.float32
BF16 = jnp.bfloat16

D_MODEL = 1024
N_HEADS = 8
HEAD_D = 128
D_QK = N_HEADS * HEAD_D
D_CONV = 3 * D_QK
CONV_W = 4
DELTA_CHUNK = 64
MLP_CHUNK = 128
N_GROUPS = 8
D_B = 1024
D_FF = 2816
D_MAIN = D_CONV + D_QK + 2 * D_B + 2 * D_MODEL
AB_OFF = D_CONV + D_QK
LANES = 128
SUBLANES = 8

TM = 512
TM_FFN = 1024
TF_FFN = 256
TT_DELTA = 128
SEQS_DELTA = 4
DEC_SEQS_PER_STEP = 16
VMEM_LIMIT = 48 * 1024 * 1024

_HI = lax.Precision.HIGHEST
_NT = (((1,), (1,)), ((), ()))
_TN = (((0,), (0,)), ((), ()))
_NN = (((1,), (0,)), ((), ()))


def _mm(a, b, dims=_NN):
    return lax.dot_general(a.astype(BF16), b.astype(BF16), dims, preferred_element_type=F32)


def _mm_hi(a, b, dims=_NN):
    return lax.dot_general(a, b, dims, precision=_HI, preferred_element_type=F32)


def _rms(x, w, eps=1e-6):
    return x * lax.rsqrt(jnp.mean(jnp.square(x), axis=-1, keepdims=True) + eps) * w


def _params(sem):
    return pltpu.CompilerParams(dimension_semantics=sem, vmem_limit_bytes=VMEM_LIMIT)


def _layer_spec(shape, l, ngrid):
    zeros = (0,) * len(shape)
    if ngrid == 1:
        return pl.BlockSpec((None,) + shape, lambda i: (l,) + zeros)
    return pl.BlockSpec((None,) + shape, lambda i, j: (l,) + zeros)


def _inproj_kernel(x_ref, nw_ref, wlo_ref, wab_ref, o_ref, ab_ref):
    h = _rms(x_ref[...], nw_ref[...]).astype(BF16)
    ab_ref[...] = _mm(h, wab_ref[...])
    o_ref[...] = _mm(h, wlo_ref[...])


def _inproj(x2d, p, l):
    m = x2d.shape[0]
    tm = min(TM, m)
    return pl.pallas_call(
        _inproj_kernel,
        grid=(m // tm,),
        in_specs=[
            pl.BlockSpec((tm, D_MODEL), lambda i: (i, 0)),
            _layer_spec((1, D_MODEL), l, 1),
            pl.BlockSpec((None, D_MODEL, AB_OFF), lambda i: (l, 0, 0), pipeline_mode=pl.Buffered(1)),
            pl.BlockSpec((None, D_MODEL, LANES), lambda i: (l, 0, AB_OFF // LANES)),
        ],
        out_specs=[
            pl.BlockSpec((tm, AB_OFF), lambda i: (i, 0)),
            pl.BlockSpec((tm, LANES), lambda i: (i, 0)),
        ],
        out_shape=[
            jax.ShapeDtypeStruct((m, AB_OFF), F32),
            jax.ShapeDtypeStruct((m, LANES), F32),
        ],
        compiler_params=_params(("arbitrary",)),
        name="inproj",
    )(x2d, p["norm_pre_mix"], p["w_lo"], p["w_ab"])


def _conv_silu_norm(y, part):
    y = y * jax.nn.sigmoid(y)
    if part < 2:
        inv = lax.rsqrt(jnp.sum(jnp.square(y), axis=-1, keepdims=True) + 1e-6)
        if part == 0:
            inv = inv * (HEAD_D ** -0.5)
        y = y * inv
    return y


def _causal_conv(xg, w_of_tap):
    sub = lax.broadcasted_iota(jnp.int32, (xg.shape[0] - 1,) + xg.shape[1:], 1)
    y = None
    for i in range(CONV_W):
        k = CONV_W - 1 - i
        if k == 0:
            tap = xg[1:]
        else:
            rot = pltpu.roll(xg, k, axis=1)
            tap = jnp.where(sub >= k, rot[1:], rot[:-1])
        term = w_of_tap(i) * tap
        y = term if y is None else y + term
    return y


def _delta_dec_kernel(qkv_ref, z_ref, ab_ref, cbuf_ref, s0_ref, cw_ref, alog_ref, dtb_ref, nw_ref,
                      *rest, nb, c, aliased):
    if aliased:
        rest = rest[2:]
    o_ref, nbuf_ref, snew_ref, xbuf_ref, qkvc_ref = rest
    pad = SUBLANES
    hist = CONV_W - 1
    xbuf_ref[:, pad - hist:pad, :] = cbuf_ref[...]
    xbuf_ref[:, pad:pad + c, :] = qkv_ref[...]
    for part in range(3):
        for h in range(N_HEADS):
            c0 = part * D_QK + h * HEAD_D
            cols = slice(c0, c0 + HEAD_D)
            y = cw_ref[0:1, cols] * xbuf_ref[:, pad - hist:pad - hist + c, cols]
            for i in range(1, CONV_W):
                y = y + cw_ref[i:i + 1, cols] * xbuf_ref[:, pad - hist + i:pad - hist + i + c, cols]
            qkvc_ref[:, :, cols] = _conv_silu_norm(y, part)
    nbuf_ref[...] = xbuf_ref[:, pad + c - hist:pad + c, :]

    ri = lax.broadcasted_iota(jnp.int32, (c, c), 0)
    ci = lax.broadcasted_iota(jnp.int32, (c, c), 1)
    causal = ri >= ci
    strict = ri > ci
    diag = ri == ci
    tri = jnp.where(causal, 1.0, 0.0).astype(F32)
    eye = jnp.where(diag, 1.0, 0.0).astype(F32)
    n_levels = c.bit_length() - 2
    items = [(i, h) for i in range(nb) for h in range(N_HEADS)]
    idx = {it: n for n, it in enumerate(items)}
    every = range(len(items))

    gc, beta = [], []
    for i in range(nb):
        g = -jnp.exp(alog_ref[...]) * jax.nn.softplus(ab_ref[i] + dtb_ref[...])
        gc.append(_mm_hi(tri, g))
        beta.append(jax.nn.sigmoid(ab_ref[i]))

    def col(part, i, h):
        return qkvc_ref[i, :, part * D_QK + h * HEAD_D:part * D_QK + (h + 1) * HEAD_D]

    gcol = [gc[i][:, h:h + 1] for i, h in items]
    bcol = [beta[i][:, N_HEADS + h:N_HEADS + h + 1] for i, h in items]
    kb = [col(1, i, h) * bcol[idx[i, h]] for i, h in items]
    kq = [_mm(jnp.concatenate([kb[idx[i, h]], col(0, i, h)], axis=0), col(1, i, h), _NT)
          for i, h in items]
    x, qk, tk = [], [], []
    for n in every:
        grow = jnp.sum(jnp.where(diag, gcol[n], 0.0), axis=0, keepdims=True)
        dec = jnp.where(causal, jnp.exp(jnp.where(causal, gcol[n] - grow, 0.0)), 0.0)
        x.append(jnp.where(strict, -(kq[n][:c] * dec), 0.0))
        qk.append(kq[n][c:] * dec)
        tk.append(eye + x[n])
    pk = [_mm(x[n], x[n]) for n in every]
    for lvl in range(1, n_levels + 1):
        if lvl < n_levels:
            r = [_mm(jnp.concatenate([pk[n], tk[n]], axis=0), pk[n]) for n in every]
            pk = [r[n][:c] for n in every]
            tk = [tk[n] + r[n][c:] for n in every]
        else:
            tk = [tk[n] + _mm(tk[n], pk[n]) for n in every]
    e = [jnp.exp(gcol[n]) for n in every]
    sol = [_mm(tk[idx[i, h]], jnp.concatenate([col(2, i, h) * bcol[idx[i, h]],
                                                kb[idx[i, h]] * e[idx[i, h]]], axis=1))
           for i, h in items]
    wq = [_mm(jnp.concatenate([sol[idx[i, h]][:, HEAD_D:], col(0, i, h) * e[idx[i, h]]], axis=0),
              s0_ref[i, h]) for i, h in items]
    vn = [sol[n][:, :HEAD_D] - wq[n][:c] for n in every]
    op = [_mm(qk[n], vn[n]) for n in every]
    for i, h in items:
        n = idx[i, h]
        glast = gc[i][c - 1:c, h:h + 1]
        kdec = col(1, i, h) * jnp.exp(glast - gcol[n])
        snew_ref[i, h] = s0_ref[i, h] * jnp.exp(glast) + _mm(kdec, vn[n], _TN)
    for i, h in items:
        n = idx[i, h]
        o = _rms(wq[n][c:] + op[n], nw_ref[...])
        cols = slice(h * HEAD_D, (h + 1) * HEAD_D)
        zz = z_ref[i, :, cols]
        o_ref[i, :, cols] = o * (zz * jax.nn.sigmoid(zz))


def _delta_dec(proj3, ab3, conv_all, s_all, p, l, prev):
    n, c, _ = proj3.shape
    depth = s_all.shape[0]
    nb = DEC_SEQS_PER_STEP
    aliased = prev is not None
    kern = functools.partial(_delta_dec_kernel, nb=nb, c=c, aliased=aliased)
    state_blk = (None, nb, N_HEADS, HEAD_D, HEAD_D)
    in_specs = [
        pl.BlockSpec((nb, c, D_CONV), lambda i: (i, 0, 0)),
        pl.BlockSpec((nb, c, D_QK), lambda i: (i, 0, D_CONV // D_QK)),
        pl.BlockSpec((nb, c, LANES), lambda i: (i, 0, 0)),
        pl.BlockSpec((None, nb, CONV_W - 1, D_CONV), lambda i: (l, i, 0, 0)),
        pl.BlockSpec(state_blk, lambda i: (l, i, 0, 0, 0)),
        _layer_spec((CONV_W, D_CONV), l, 1),
        _layer_spec((1, LANES), l, 1),
        _layer_spec((1, LANES), l, 1),
        _layer_spec((1, HEAD_D), l, 1),
    ]
    args = [proj3, proj3, ab3, conv_all, s_all, p["conv_w"], p["a_log"], p["dt_bias"], p["delta_norm_w"]]
    aliases = {}
    if aliased:
        in_specs += [pl.BlockSpec(memory_space=pl.ANY), pl.BlockSpec(memory_space=pl.ANY)]
        args += list(prev)
        aliases = {len(args) - 2: 1, len(args) - 1: 2}
    return pl.pallas_call(
        kern,
        grid=(n // nb,),
        in_specs=in_specs,
        out_specs=[
            pl.BlockSpec((nb, c, D_QK), lambda i: (i, 0, 0)),
            pl.BlockSpec((None, nb, CONV_W - 1, D_CONV), lambda i: (l, i, 0, 0)),
            pl.BlockSpec(state_blk, lambda i: (l, i, 0, 0, 0)),
        ],
        out_shape=[
            jax.ShapeDtypeStruct((n, c, D_QK), F32),
            jax.ShapeDtypeStruct((depth, n, CONV_W - 1, D_CONV), F32),
            jax.ShapeDtypeStruct((depth, n, N_HEADS, HEAD_D, HEAD_D), F32),
        ],
        scratch_shapes=[
            pltpu.VMEM((nb, c + SUBLANES, D_CONV), F32),
            pltpu.VMEM((nb, c, D_CONV), F32),
        ],
        input_output_aliases=aliases,
        compiler_params=_params(("arbitrary",)),
        name="delta_dec",
    )(*args)


def _delta_chunk(refs, ic, c):
    qkvc_ref, gc_ref, beta_ref, s_ref, nw_ref, o_ref = refs
    assert 2 * c == LANES
    ns = gc_ref.shape[0]
    ri = lax.broadcasted_iota(jnp.int32, (c, LANES), 0)
    lane = lax.broadcasted_iota(jnp.int32, (c, LANES), 1)
    half = lane >= c
    cj = jnp.where(half, lane - c, lane)
    causal = ri >= cj
    strict = ri > cj
    diag = ri == cj
    eye = jnp.where(diag, 1.0, 0.0).astype(F32)
    zeros = jnp.zeros((c, HEAD_D), F32)
    zeros2 = jnp.zeros((c, 2 * HEAD_D), F32)
    n_levels = c.bit_length() - 2

    def blockdiag(m):
        return jnp.concatenate([jnp.where(half, 0.0, m), jnp.where(half, m, 0.0)], axis=0)

    def side_by_side(a, b):
        return jnp.concatenate([jnp.concatenate([a, zeros], axis=1),
                                jnp.concatenate([zeros, b], axis=1)], axis=0)

    rows = pl.ds(pl.multiple_of(ic * c, c), c)
    heads = [(s, h) for s in range(ns) for h in range(N_HEADS)]
    pairs = [(s, pr) for s in range(ns) for pr in range(N_HEADS // 2)]
    gc = [gc_ref[s, rows, :] for s in range(ns)]
    beta = [beta_ref[s, rows, :] for s in range(ns)]

    def col(part, s, h):
        return qkvc_ref[s, rows, part * D_QK + h * HEAD_D:part * D_QK + (h + 1) * HEAD_D]

    g = {(s, h): jnp.broadcast_to(gc[s][:, h:h + 1], (c, HEAD_D)) for s, h in heads}
    b = {(s, h): jnp.broadcast_to(beta[s][:, N_HEADS + h:N_HEADS + h + 1], (c, HEAD_D))
         for s, h in heads}
    kb = {(s, h): col(1, s, h) * b[s, h] for s, h in heads}
    kq = {}
    for s, pr in pairs:
        h1, h2 = 2 * pr, 2 * pr + 1
        lhs = jnp.concatenate([jnp.concatenate([kb[s, h1], kb[s, h2]], axis=1),
                               jnp.concatenate([col(0, s, h1), col(0, s, h2)], axis=1)], axis=0)
        kq[s, pr] = _mm(lhs, side_by_side(col(1, s, h1), col(1, s, h2)), _NT)
    x, qkd, tk = {}, {}, {}
    for s, pr in pairs:
        gcp = jnp.where(half, g[s, 2 * pr + 1], g[s, 2 * pr])
        rowp = jnp.sum(jnp.where(diag, gcp, 0.0), axis=0, keepdims=True)
        dec = jnp.where(causal, jnp.exp(jnp.where(causal, gcp - rowp, 0.0)), 0.0)
        x[s, pr] = jnp.where(strict, -(kq[s, pr][:c] * dec), 0.0)
        qkd[s, pr] = kq[s, pr][c:] * dec
        tk[s, pr] = eye + x[s, pr]
    pk = {sp: _mm(x[sp], blockdiag(x[sp])) for sp in pairs}
    for lvl in range(1, n_levels + 1):
        if lvl < n_levels:
            r = {sp: _mm(jnp.concatenate([pk[sp], tk[sp]], axis=0), blockdiag(pk[sp])) for sp in pairs}
            pk = {sp: r[sp][:c] for sp in pairs}
            tk = {sp: tk[sp] + r[sp][c:] for sp in pairs}
        else:
            tk = {sp: tk[sp] + _mm(tk[sp], blockdiag(pk[sp])) for sp in pairs}
    e = {sh: jnp.exp(g[sh]) for sh in heads}
    sol = {}
    for s, h in heads:
        rhs = jnp.concatenate([col(2, s, h) * b[s, h], kb[s, h] * e[s, h]], axis=1)
        rhs = jnp.concatenate([rhs, zeros2] if h % 2 == 0 else [zeros2, rhs], axis=0)
        sol[s, h] = _mm(tk[s, h // 2], rhs)
    wq = {(s, h): _mm(jnp.concatenate([sol[s, h][:, HEAD_D:], col(0, s, h) * e[s, h]], axis=0),
                      s_ref[s, h]) for s, h in heads}
    vn = {sh: sol[sh][:, :HEAD_D] - wq[sh][:c] for sh in heads}
    op = {(s, pr): _mm(qkd[s, pr], side_by_side(vn[s, 2 * pr], vn[s, 2 * pr + 1]))
          for s, pr in pairs}
    for s, h in heads:
        glast = gc[s][c - 1:c, h:h + 1]
        kdec = col(1, s, h) * jnp.exp(glast - g[s, h])
        s_ref[s, h] = s_ref[s, h] * jnp.exp(glast) + _mm(kdec, vn[s, h], _TN)
    for s, h in heads:
        o = _rms(wq[s, h][c:] + op[s, h // 2][:, (h % 2) * HEAD_D:(h % 2 + 1) * HEAD_D], nw_ref[...])
        cols = slice(h * HEAD_D, (h + 1) * HEAD_D)
        zz = qkvc_ref[s, rows, D_CONV + h * HEAD_D:D_CONV + (h + 1) * HEAD_D]
        o_ref[s, rows, cols] = (o * (zz * jax.nn.sigmoid(zz))).astype(o_ref.dtype)


def _delta_kernel(x_ref, npre_ref, wlo_ref, wab_ref, cbuf_ref, s0_ref, cw_ref, alog_ref, dtb_ref, nw_ref,
                  o_ref, nbuf_ref, snew_ref,
                  s_ref, hist_ref, qkvc_ref, gc_ref, beta_ref, *, tt, c):
    t = pl.program_id(1)
    nt = pl.num_programs(1)
    ns = x_ref.shape[0]
    hist = CONV_W - 1

    @pl.when(t == 0)
    def _():
        s_ref[...] = s0_ref[...]
        hist_ref[...] = jnp.zeros(hist_ref.shape, F32)
        hist_ref[:, SUBLANES - hist:, :] = cbuf_ref[...]

    hn = _rms(x_ref[...].reshape(ns * tt, D_MODEL), npre_ref[...]).astype(BF16)
    qkvc_ref[...] = _mm(hn, wlo_ref[...]).reshape(ns, tt, AB_OFF)
    ab_all = _mm(hn, wab_ref[...]).reshape(ns, tt, LANES)

    rt = lax.broadcasted_iota(jnp.int32, (c, c), 0)
    ct = lax.broadcasted_iota(jnp.int32, (c, c), 1)
    tri = jnp.where(rt >= ct, 1.0, 0.0).astype(F32)
    for s in range(ns):
        ab = ab_all[s]
        g = -jnp.exp(alog_ref[...]) * jax.nn.softplus(ab + dtb_ref[...])
        beta_ref[s] = jax.nn.sigmoid(ab)
        for ic in range(tt // c):
            gc_ref[s, ic * c:(ic + 1) * c, :] = _mm_hi(tri, g[ic * c:(ic + 1) * c, :])

    for s in range(ns):
        for part in range(3):
            for h in range(N_HEADS):
                c0 = part * D_QK + h * HEAD_D
                cols = slice(c0, c0 + HEAD_D)
                raw = qkvc_ref[s, :, cols]
                xg = jnp.concatenate([hist_ref[s, :, cols], raw], axis=0)
                xg = xg.reshape(tt // SUBLANES + 1, SUBLANES, HEAD_D)
                y = _causal_conv(xg, lambda i: cw_ref[i:i + 1, cols])
                qkvc_ref[s, :, cols] = _conv_silu_norm(y, part).reshape(tt, HEAD_D)
                hist_ref[s, :, cols] = raw[tt - SUBLANES:, :]

    @pl.when(t == nt - 1)
    def _():
        nbuf_ref[...] = hist_ref[:, SUBLANES - hist:, :]

    refs = (qkvc_ref, gc_ref, beta_ref, s_ref, nw_ref, o_ref)

    def body(ic, carry):
        _delta_chunk(refs, ic, c)
        return carry

    lax.fori_loop(0, tt // c, body, 0)

    @pl.when(t == nt - 1)
    def _():
        snew_ref[...] = s_ref[...]


def _delta(x, conv_buf, s0, p, l):
    n, t, _ = x.shape
    tt, c = min(TT_DELTA, t), DELTA_CHUNK
    ns = min(SEQS_DELTA, n)
    kern = functools.partial(_delta_kernel, tt=tt, c=c)
    tile = lambda i, j: (i, j, 0)
    return pl.pallas_call(
        kern,
        grid=(n // ns, t // tt),
        in_specs=[
            pl.BlockSpec((ns, tt, D_MODEL), tile),
            _layer_spec((1, D_MODEL), l, 2),
            pl.BlockSpec((None, D_MODEL, AB_OFF), lambda i, j: (l, 0, 0), pipeline_mode=pl.Buffered(1)),
            pl.BlockSpec((None, D_MODEL, LANES), lambda i, j: (l, 0, AB_OFF // LANES)),
            pl.BlockSpec((ns, CONV_W - 1, D_CONV), lambda i, j: (i, 0, 0)),
            pl.BlockSpec((ns, N_HEADS, HEAD_D, HEAD_D), lambda i, j: (i, 0, 0, 0)),
            _layer_spec((CONV_W, D_CONV), l, 2),
            _layer_spec((1, LANES), l, 2),
            _layer_spec((1, LANES), l, 2),
            _layer_spec((1, HEAD_D), l, 2),
        ],
        out_specs=[
            pl.BlockSpec((ns, tt, D_QK), tile),
            pl.BlockSpec((ns, CONV_W - 1, D_CONV), lambda i, j: (i, 0, 0)),
            pl.BlockSpec((ns, N_HEADS, HEAD_D, HEAD_D), lambda i, j: (i, 0, 0, 0)),
        ],
        out_shape=[
            jax.ShapeDtypeStruct((n, t, D_QK), BF16),
            jax.ShapeDtypeStruct((n, CONV_W - 1, D_CONV), F32),
            jax.ShapeDtypeStruct((n, N_HEADS, HEAD_D, HEAD_D), F32),
        ],
        scratch_shapes=[
            pltpu.VMEM((ns, N_HEADS, HEAD_D, HEAD_D), F32),
            pltpu.VMEM((ns, SUBLANES, D_CONV), F32),
            pltpu.VMEM((ns, tt, AB_OFF), F32),
            pltpu.VMEM((ns, tt, LANES), F32),
            pltpu.VMEM((ns, tt, LANES), F32),
        ],
        compiler_params=_params(("arbitrary", "arbitrary")),
        name="delta",
    )(x, p["norm_pre_mix"], p["w_lo"], p["w_ab"], conv_buf, s0, p["conv_w"], p["a_log"], p["dt_bias"],
      p["delta_norm_w"])


def _mix_kernel(x_ref, oa_ref, npre_ref, whi_ref, lnw_ref, lnb_ref, ws_ref, bs_ref,
                wa_ref, wb_ref, wo_ref, nw_ref, y_ref, v_ref, uv_ref, gate_ref, ob_ref, *, keep_all_v):
    c = MLP_CHUNK
    tm = x_ref.shape[0]
    n_ch = tm // c
    tn = D_MODEL // n_ch
    gw = 2 * D_MODEL // n_ch
    ri = lax.broadcasted_iota(jnp.int32, (c, c), 0)
    ci = lax.broadcasted_iota(jnp.int32, (c, c), 1)
    gd = D_B // N_GROUPS
    ws = [jnp.where(ri >= ci, ws_ref[g], 0.0).astype(BF16) for g in range(N_GROUPS)]
    h = _rms(x_ref[...], npre_ref[...]).astype(BF16)
    uv_ref[...] = _mm(h, whi_ref[:, :2 * D_B])
    pa = []
    for ch in range(n_ch):
        pa.append(_mm(oa_ref[...], wa_ref[:, ch * tn:(ch + 1) * tn]))
        g0 = 2 * D_B + ch * gw
        gate_ref[:, ch * gw:(ch + 1) * gw] = jax.nn.sigmoid(_mm(h, whi_ref[:, g0:g0 + gw]))
        rows = slice(ch * c, (ch + 1) * c)
        x = uv_ref[rows, :]
        gel = 0.5 * x * (1.0 + lax.erf(x * (2.0 ** -0.5)))
        u = gel[:, :D_B]
        v = gel[:, D_B:]
        vc = v - jnp.mean(v, axis=-1, keepdims=True)
        var = jnp.mean(jnp.square(vc), axis=-1, keepdims=True)
        v = vc * lax.rsqrt(var + 1e-5) * lnw_ref[...] + lnb_ref[...]
        if keep_all_v:
            v_ref[rows, :] = v
        elif ch == n_ch - 1:
            v_ref[0] = v
        for g in range(N_GROUPS):
            cols = slice(g * gd, (g + 1) * gd)
            mixed = _mm(ws[g], v[:, cols]) + bs_ref[:, g:g + 1]
            ob_ref[rows, cols] = (u[:, cols] * mixed).astype(BF16)
    pa = jnp.concatenate(pa, axis=1)
    pb = _mm(ob_ref[...], wb_ref[...])
    merged = gate_ref[:, :D_MODEL] * pa + gate_ref[:, D_MODEL:] * pb
    y = _mm(merged, wo_ref[...])
    y_ref[...] = x_ref[...] + _rms(y, nw_ref[...])


def _mix(oa, x2d, p, l, seq_len):
    m = x2d.shape[0]
    tm = min(TM, m)
    c = MLP_CHUNK
    row = lambda i: (i, 0)
    short = seq_len < c
    if short:
        assert c % seq_len == 0 and tm % c == 0
        ws_key, bs_key = "ws_short", "bs_short"
        v_spec = pl.BlockSpec((tm, D_B), row)
        v_shape = jax.ShapeDtypeStruct((m, D_B), F32)
    else:
        assert seq_len % tm == 0
        ws_key, bs_key = "w_spatial", "bs_t"
        tiles_per_seq = seq_len // tm
        v_spec = pl.BlockSpec((1, c, D_B), lambda i: (i // tiles_per_seq, 0, 0))
        v_shape = jax.ShapeDtypeStruct((m // seq_len, c, D_B), F32)
    resident = dict(pipeline_mode=pl.Buffered(1))
    wspec = pl.BlockSpec((None, D_MODEL, D_MODEL), lambda i: (l, 0, 0), **resident)
    return pl.pallas_call(
        functools.partial(_mix_kernel, keep_all_v=short),
        grid=(m // tm,),
        in_specs=[
            pl.BlockSpec((tm, D_MODEL), row),
            pl.BlockSpec((tm, D_QK), row),
            _layer_spec((1, D_MODEL), l, 1),
            pl.BlockSpec((None, D_MODEL, D_MAIN - AB_OFF), lambda i: (l, 0, 0), **resident),
            _layer_spec((1, D_B), l, 1),
            _layer_spec((1, D_B), l, 1),
            _layer_spec((N_GROUPS, c, c), l, 1),
            _layer_spec((c, LANES), l, 1),
            wspec, wspec, wspec,
            _layer_spec((1, D_MODEL), l, 1),
        ],
        out_specs=[pl.BlockSpec((tm, D_MODEL), row), v_spec],
        out_shape=[jax.ShapeDtypeStruct((m, D_MODEL), F32), v_shape],
        scratch_shapes=[pltpu.VMEM((tm, 2 * D_B), F32), pltpu.VMEM((tm, 2 * D_MODEL), F32),
                        pltpu.VMEM((tm, D_B), BF16)],
        compiler_params=_params(("arbitrary",)),
        name="mix",
    )(x2d, oa, p["norm_pre_mix"], p["w_hi"], p["sgu_ln_w"], p["sgu_ln_b"], p[ws_key], p[bs_key],
      p["w_proj_a"], p["w_proj_b"], p["w_out"], p["norm_post_mix"])


def _ffn_kernel(x_ref, npre_ref, wi_ref, wd_ref, npost_ref, y_ref, act_ref):
    x = x_ref[...]
    h = _rms(x, npre_ref[...]).astype(BF16)
    tf = TF_FFN
    for f in range(D_FF // tf):
        gate = _mm(h, wi_ref[:, f * tf:(f + 1) * tf])
        up = _mm(h, wi_ref[:, D_FF + f * tf:D_FF + (f + 1) * tf])
        act_ref[:, f * tf:(f + 1) * tf] = (gate * jax.nn.sigmoid(gate) * up).astype(BF16)
    y = _mm(act_ref[...], wd_ref[...])
    y_ref[...] = x + _rms(y, npost_ref[...])


def _ffn(x2d, p, l):
    m = x2d.shape[0]
    tm = min(TM_FFN, m)
    resident = dict(pipeline_mode=pl.Buffered(1))
    return pl.pallas_call(
        _ffn_kernel,
        grid=(m // tm,),
        in_specs=[
            pl.BlockSpec((tm, D_MODEL), lambda i: (i, 0)),
            _layer_spec((1, D_MODEL), l, 1),
            pl.BlockSpec((None, D_MODEL, 2 * D_FF), lambda i: (l, 0, 0), **resident),
            pl.BlockSpec((None, D_FF, D_MODEL), lambda i: (l, 0, 0), **resident),
            _layer_spec((1, D_MODEL), l, 1),
        ],
        out_specs=pl.BlockSpec((tm, D_MODEL), lambda i: (i, 0)),
        out_shape=jax.ShapeDtypeStruct((m, D_MODEL), F32),
        scratch_shapes=[pltpu.VMEM((tm, D_FF), BF16)],
        compiler_params=_params(("arbitrary",)),
        name="ffn",
    )(x2d, p["norm_pre_ffn"], p["w_ffn_in"], p["w_ffn_out"], p["norm_post_ffn"])


def _front_short(x, conv_all, s_all, p, l, prev):
    n, t, _ = x.shape
    proj, ab = _inproj(x.reshape(n * t, D_MODEL), p, l)
    return _delta_dec(proj.reshape(n, t, AB_OFF), ab.reshape(n, t, LANES), conv_all, s_all, p, l, prev)


def _trunk_layer(x, p, l, front):
    n, t, _ = x.shape
    m = n * t
    x2d = x.reshape(m, D_MODEL)
    o_a, new_buf, s_new = front(x)
    x1, v_rows = _mix(o_a.reshape(m, D_QK), x2d, p, l, t)
    x2 = _ffn(x1, p, l)
    return x2.reshape(n, t, D_MODEL), s_new, new_buf, v_rows.reshape(n, -1, D_B)


def _prepare_params(t_short, norm_pre_mix, w_in, conv_w, a_log, dt_bias, delta_norm_w, sgu_ln_w,
                    sgu_ln_b, w_spatial, b_spatial, w_proj_a, w_proj_b, w_out, norm_post_mix,
                    norm_pre_ffn, w_ffn_in, w_ffn_out, norm_post_ffn):
    depth = w_in.shape[0]
    w_bf = w_in.astype(BF16)
    w_hi = w_bf[:, :, AB_OFF + 2 * N_HEADS:]
    row = lambda v: v.reshape(depth, 1, -1)
    lanes = lambda v: jnp.pad(v, ((0, 0), (0, LANES - v.shape[1]))).reshape(depth, 1, LANES)
    bs_t = jnp.pad(jnp.swapaxes(b_spatial, 1, 2), ((0, 0), (0, 0), (0, LANES - N_GROUPS)))
    rep = MLP_CHUNK // t_short
    idx = jnp.arange(MLP_CHUNK)
    same_block = (idx[:, None] // t_short) == (idx[None, :] // t_short)
    onehot = (idx[:, None] % t_short == jnp.arange(t_short)[None, :]).astype(F32)
    tiled = jnp.einsum("ri,lgij,cj->lgrc", onehot, w_spatial[:, :, :t_short, :t_short], onehot,
                       precision=_HI)
    ws_short = jnp.where(same_block, tiled, 0.0)
    return dict(
        norm_pre_mix=row(norm_pre_mix), w_lo=w_bf, w_hi=w_hi, w_ab=w_bf,
        conv_w=conv_w, a_log=lanes(a_log), dt_bias=lanes(dt_bias), delta_norm_w=row(delta_norm_w),
        sgu_ln_w=row(sgu_ln_w), sgu_ln_b=row(sgu_ln_b), w_spatial=w_spatial, bs_t=bs_t,
        ws_short=ws_short, bs_short=jnp.tile(bs_t[:, :t_short], (1, rep, 1)),
        w_proj_a=w_proj_a.astype(BF16), w_proj_b=w_proj_b.astype(BF16), w_out=w_out.astype(BF16),
        norm_post_mix=row(norm_post_mix), norm_pre_ffn=row(norm_pre_ffn),
        w_ffn_in=w_ffn_in.astype(BF16), w_ffn_out=w_ffn_out.astype(BF16),
        norm_post_ffn=row(norm_post_ffn))


def kernel(x_prompt, x_sample, state_delta, state_conv, norm_pre_mix, w_in, conv_w, a_log, dt_bias,
           delta_norm_w, sgu_ln_w, sgu_ln_b, w_spatial, b_spatial, w_proj_a, w_proj_b, w_out,
           norm_post_mix, norm_pre_ffn, w_ffn_in, w_ffn_out, norm_post_ffn):
    depth = w_in.shape[0]
    nb, seq, _ = x_prompt.shape
    ndec, dec_seq, _ = x_sample.shape
    assert seq % DELTA_CHUNK == 0 and seq % MLP_CHUNK == 0
    assert dec_seq % SUBLANES == 0 and dec_seq < DELTA_CHUNK and ndec % DEC_SEQS_PER_STEP == 0
    p = _prepare_params(dec_seq, norm_pre_mix, w_in, conv_w, a_log, dt_bias, delta_norm_w, sgu_ln_w,
                        sgu_ln_b, w_spatial, b_spatial, w_proj_a, w_proj_b, w_out, norm_post_mix,
                        norm_pre_ffn, w_ffn_in, w_ffn_out, norm_post_ffn)
    y_p, y_s = x_prompt, x_sample
    conv0 = jnp.zeros((nb, CONV_W - 1, D_CONV), x_prompt.dtype)
    s_zero = jnp.zeros((nb, N_HEADS, HEAD_D, HEAD_D), state_delta.dtype)
    sd_p, sc_p, cv_p, cv_s = [], [], [], []
    prev_s = None
    for l in range(depth):
        y_p, s_new, buf_new, v_rows = _trunk_layer(
            y_p, p, l, lambda x: _delta(x, conv0, s_zero, p, l))
        sd_p.append(s_new)
        sc_p.append(buf_new)
        cv_p.append(v_rows)
        y_s, sd_s, sc_s, v_rows = _trunk_layer(
            y_s, p, l, lambda x: _front_short(x, state_conv, state_delta, p, l, prev_s))
        prev_s = (sc_s, sd_s)
        cv_s.append(v_rows)
    return (y_p, y_s, jnp.stack(sd_p), jnp.stack(sc_p), jnp.stack(cv_p),
            sd_s, sc_s, jnp.stack(cv_s))
```
